```python
import jax
import jax.numpy as jnp
from jax import lax
import numpy as np

D_MODEL = 1024
BATCH = 8
SEQ = 2048
DEPTH = 4

GRID_W = 64
CTX_LEN = 256
EPS = 1e-6
NEG_INF = -1e30

BRANCH_W = 512
N_BRANCH = 3
CONV_K = 3
ATT_HEADS = 8
ATT_KV_HEADS = 2
ATT_GROUP = ATT_HEADS // ATT_KV_HEADS
HEAD_DIM = 64
WINDOW = 128
Q_BLOCK = 128
ROPE_BASE = 10000.0
GLA_HEADS = 4
GLA_DK = 64
GLA_DV = 128
GLA_RANK = 16
GLA_TAU = 16.0
GLA_CHUNK = 64
D_FF = 3584
N_EXPERTS = 8
TOP_K = 2
N_DENSE = (DEPTH + 1) // 2
N_MOE = DEPTH // 2

PROJ_SPLITS = (BRANCH_W, BRANCH_W, BRANCH_W,
               ATT_HEADS * HEAD_DIM, ATT_KV_HEADS * HEAD_DIM, ATT_KV_HEADS * HEAD_DIM,
               GLA_HEADS * GLA_DK, GLA_HEADS * GLA_DK, GLA_HEADS * GLA_DV, GLA_HEADS * GLA_DV,
               2 * GLA_RANK, N_BRANCH * D_MODEL)
D_IN = sum(PROJ_SPLITS)

kernel_name = 'hybrid_dit_conv_swa_gla_moe'


def rmsnorm(x, g):
    xf = x.astype(jnp.float32)
    y = xf * lax.rsqrt(jnp.mean(xf * xf, axis=-1, keepdims=True) + EPS)
    return (y * g.astype(jnp.float32)).astype(x.dtype)


def modulate(h, shift, scale):
    return h * (1 + scale) + shift


def split_proj(p):
    outs = []
    off = 0
    for w in PROJ_SPLITS:
        outs.append(p[..., off:off + w])
        off += w
    return outs


def axial_rope_angles(n):
    rows = n // GRID_W
    row = jnp.repeat(jnp.arange(rows, dtype=jnp.float32), GRID_W)
    col = jnp.tile(jnp.arange(GRID_W, dtype=jnp.float32), rows)
    half = HEAD_DIM // 2
    inv_freq = ROPE_BASE ** (-jnp.arange(0, half, 2, dtype=jnp.float32) / half)
    ang_r = row[:, None] * inv_freq
    ang_c = col[:, None] * inv_freq
    return (jnp.cos(ang_r), jnp.sin(ang_r), jnp.cos(ang_c), jnp.sin(ang_c))


def rope_half(x, cos, sin):
    x1, x2 = jnp.split(x, 2, axis=-1)
    cos = cos[:, None, :]
    sin = sin[:, None, :]
    return jnp.concatenate([x1 * cos - x2 * sin, x2 * cos + x1 * sin], axis=-1)


def axial_rope(x, angles):
    cos_r, sin_r, cos_c, sin_c = angles
    xf = x.astype(jnp.float32)
    half = HEAD_DIM // 2
    y = jnp.concatenate([rope_half(xf[..., :half], cos_r, sin_r),
                         rope_half(xf[..., half:], cos_c, sin_c)], axis=-1)
    return y.astype(x.dtype)


def short_conv(b_gate, c_gate, u, w):
    z = c_gate * u
    zp = jnp.pad(z, ((0, 0), (1, 1), (0, 0)))
    y = zp[:, :-2] * w[0] + zp[:, 1:-1] * w[1] + zp[:, 2:] * w[2]
    return b_gate * y


def window_attention(q, k, v, k_ctx, v_ctx, sink):
    bsz, n = q.shape[:2]
    nb = n // Q_BLOCK
    qb = q.reshape(bsz, nb, Q_BLOCK, ATT_KV_HEADS, ATT_GROUP, HEAD_DIM)

    def blocks_with_halo(t):
        tp = jnp.pad(t, ((0, 0), (Q_BLOCK, Q_BLOCK), (0, 0), (0, 0)))
        tp = tp.reshape(bsz, nb + 2, Q_BLOCK, ATT_KV_HEADS, HEAD_DIM)
        return jnp.concatenate([tp[:, :-2], tp[:, 1:-1], tp[:, 2:]], axis=2)

    kw = blocks_with_halo(k)
    vw = blocks_with_halo(v)
    scale = HEAD_DIM ** -0.5
    s_lat = jnp.einsum('bnqkgd,bnjkd->bkgnqj', qb, kw, preferred_element_type=jnp.float32) * scale
    s_ctx = jnp.einsum('bnqkgd,bjkd->bkgnqj', qb, k_ctx, preferred_element_type=jnp.float32) * scale
    qpos = jnp.arange(nb)[:, None, None] * Q_BLOCK + jnp.arange(Q_BLOCK)[None, :, None]
    kpos = (jnp.arange(nb)[:, None, None] - 1) * Q_BLOCK + jnp.arange(3 * Q_BLOCK)[None, None, :]
    valid = (jnp.abs(kpos - qpos) <= WINDOW) & (kpos >= 0) & (kpos < n)
    s_lat = jnp.where(valid, s_lat, NEG_INF)
    s_sink = jnp.broadcast_to(sink.astype(jnp.float32).reshape(ATT_KV_HEADS, ATT_GROUP, 1, 1, 1),
                              s_lat.shape[:-1] + (1,))
    p = jax.nn.softmax(jnp.concatenate([s_lat, s_ctx, s_sink], axis=-1), axis=-1)
    nk = 3 * Q_BLOCK
    p_lat = p[..., :nk].astype(v.dtype)
    p_ctx = p[..., nk:nk + k_ctx.shape[1]].astype(v.dtype)
    o = (jnp.einsum('bkgnqj,bnjkd->bnqkgd', p_lat, vw)
         + jnp.einsum('bkgnqj,bjkd->bnqkgd', p_ctx, v_ctx))
    return o.reshape(bsz, n, ATT_HEADS * HEAD_DIM)


def context_attention(q, k, v, sink):
    bsz, m = q.shape[:2]
    qg = q.reshape(bsz, m, ATT_KV_HEADS, ATT_GROUP, HEAD_DIM)
    s = jnp.einsum('bqkgd,bjkd->bkgqj', qg, k, preferred_element_type=jnp.float32) * (HEAD_DIM ** -0.5)
    s_sink = jnp.broadcast_to(sink.astype(jnp.float32).reshape(ATT_KV_HEADS, ATT_GROUP, 1, 1),
                              s.shape[:-1] + (1,))
    p = jax.nn.softmax(jnp.concatenate([s, s_sink], axis=-1), axis=-1)
    o = jnp.einsum('bkgqj,bjkd->bqkgd', p[..., :m].astype(v.dtype), v)
    return o.reshape(bsz, m, ATT_HEADS * HEAD_DIM)


def gla_chunked(q, k, v, log_a, s0, with_output):
    bsz, n, h, _ = q.shape
    dv = v.shape[-1]
    nc = n // GLA_CHUNK

    def chunks(t):
        return t.reshape(bsz, nc, GLA_CHUNK, h, t.shape[-1])

    q, k, v, log_a = chunks(q), chunks(k), chunks(v), chunks(log_a)
    b = jnp.cumsum(log_a, axis=2)
    b_last = b[:, :, -1]
    k_end = k * jnp.exp(b_last[:, :, None] - b)
    d_state = jnp.einsum('bcshd,bcshv->cbhdv', k_end, v)
    decay = jnp.exp(b_last).transpose(1, 0, 2, 3)[..., None]

    def step(s, inp):
        dec, ds = inp
        return dec * s + ds, s

    s_final, s_enter = lax.scan(step, s0, (decay, d_state))
    if not with_output:
        return None, s_final
    q_dec = q * jnp.exp(b)
    k_inv = k * jnp.exp(-b)
    lower = jnp.tril(jnp.ones((GLA_CHUNK, GLA_CHUNK), dtype=bool))
    att = jnp.where(lower, jnp.einsum('bclhd,bcshd->bchls', q_dec, k_inv), 0.0)
    o = (jnp.einsum('bchls,bcshv->bclhv', att, v)
         + jnp.einsum('bclhd,cbhdv->bclhv', q_dec, s_enter))
    return o.reshape(bsz, n, h, dv), s_final


def gla_inputs(q, k, v, a, w2, bias):
    bsz, n, _ = q.shape
    f32 = jnp.float32
    qf = q.astype(f32).reshape(bsz, n, GLA_HEADS, GLA_DK) * (GLA_DK ** -0.5)
    kf = k.astype(f32).reshape(bsz, n, GLA_HEADS, GLA_DK)
    vf = v.astype(f32).reshape(bsz, n, GLA_HEADS, GLA_DV)
    af = a.astype(f32)
    log_a = []
    for d in range(2):
        z = af[..., d * GLA_RANK:(d + 1) * GLA_RANK] @ w2[d].astype(f32) + bias[d].astype(f32)
        log_a.append((jax.nn.log_sigmoid(z) / GLA_TAU).reshape(bsz, n, GLA_HEADS, GLA_DK))
    return qf, kf, vf, log_a


def gla_bidir(qf, kf, vf, log_a, s0_fwd, s0_bwd, with_output):
    rev = lambda t: t[:, ::-1]
    o_f, s_f = gla_chunked(qf, kf, vf, log_a[0], s0_fwd, with_output)
    o_b, s_b = gla_chunked(rev(qf), rev(kf), rev(vf), rev(log_a[1]), s0_bwd, with_output)
    o = o_f + rev(o_b) if with_output else None
    return o, s_f, s_b


def gla_output(o, r, g):
    bsz, n = o.shape[:2]
    o = o * lax.rsqrt(jnp.mean(o * o, axis=-1, keepdims=True) + EPS)
    o = o.reshape(bsz, n, GLA_HEADS * GLA_DV) * g.astype(jnp.float32)
    return (o * jax.nn.silu(r.astype(jnp.float32))).astype(r.dtype)


def merge_branches(branches, gate_logits, w_branch, w_out):
    y = jnp.stack(branches, axis=2)
    y = jnp.einsum('btim,imd->btid', y, w_branch)
    g = jax.nn.sigmoid(gate_logits.reshape(gate_logits.shape[:-1] + (N_BRANCH, D_MODEL)))
    return jnp.sum(g * y, axis=2) @ w_out


def parallel_mixers(hx, hc, angles, w_in, conv_w, sink, gla_w2, gla_b, gla_g, w_branch, w_out, ctx_out):
    bsz, n, _ = hx.shape
    m = hc.shape[1]
    (cb_x, cc_x, cu_x, qa_x, ka_x, va_x, qg_x, kg_x, vg_x, rg_x, ag_x, gate_x) = split_proj(hx @ w_in)
    (cb_c, cc_c, cu_c, qa_c, ka_c, va_c, qg_c, kg_c, vg_c, rg_c, ag_c, gate_c) = split_proj(hc @ w_in)
    ya_x = short_conv(cb_x, cc_x, cu_x, conv_w)
    k_ctx = ka_c.reshape(bsz, m, ATT_KV_HEADS, HEAD_DIM)
    v_ctx = va_c.reshape(bsz, m, ATT_KV_HEADS, HEAD_DIM)
    q_x = axial_rope(qa_x.reshape(bsz, n, ATT_HEADS, HEAD_DIM), angles)
    k_x = axial_rope(ka_x.reshape(bsz, n, ATT_KV_HEADS, HEAD_DIM), angles)
    v_x = va_x.reshape(bsz, n, ATT_KV_HEADS, HEAD_DIM)
    yb_x = window_attention(q_x, k_x, v_x, k_ctx, v_ctx, sink)
    s0 = jnp.zeros((bsz, GLA_HEADS, GLA_DK, GLA_DV), jnp.float32)
    o_c, s_f, s_b = gla_bidir(*gla_inputs(qg_c, kg_c, vg_c, ag_c, gla_w2, gla_b), s0, s0, ctx_out)
    o_x, _, _ = gla_bidir(*gla_inputs(qg_x, kg_x, vg_x, ag_x, gla_w2, gla_b), s_f, s_b, True)
    yc_x = gla_output(o_x, rg_x, gla_g)
    y_x = merge_branches((ya_x, yb_x, yc_x), gate_x, w_branch, w_out)
    if not ctx_out:
        return y_x, None
    ya_c = short_conv(cb_c, cc_c, cu_c, conv_w)
    yb_c = context_attention(qa_c.reshape(bsz, m, ATT_HEADS, HEAD_DIM), k_ctx, v_ctx, sink)
    yc_c = gla_output(o_c, rg_c, gla_g)
    y_c = merge_branches((ya_c, yb_c, yc_c), gate_c, w_branch, w_out)
    return y_x, y_c


def swiglu(h, w1, w3, w2):
    return (jax.nn.silu(h @ w1) * (h @ w3)) @ w2


def moe_swiglu(h, router_w, w1, w3, w2):
    logits = (h @ router_w).astype(jnp.float32)
    top_v, top_i = lax.top_k(logits, TOP_K)
    top_p = jax.nn.softmax(top_v, axis=-1)
    gates = jnp.sum(jax.nn.one_hot(top_i, N_EXPERTS, dtype=jnp.float32) * top_p[..., None], axis=-2)
    gates = gates.astype(h.dtype)
    out = jnp.zeros_like(h)
    for e in range(N_EXPERTS):
        out = out + gates[..., e:e + 1] * swiglu(h, w1[e], w3[e], w2[e])
    return out


def channel_mixer(h, l, ffn_w1, ffn_w3, ffn_w2, router_w, moe_w1, moe_w3, moe_w2):
    j = l // 2
    if l % 2 == 0:
        return swiglu(h, ffn_w1[j], ffn_w3[j], ffn_w2[j])
    return moe_swiglu(h, router_w[j], moe_w1[j], moe_w3[j], moe_w2[j])


def setup_inputs(seed: int = 0) -> dict:
    key = jax.random.key(seed)
    ks = jax.random.split(key, 24)
    D = D_MODEL

    def nrm(k, shape, s):
        return jax.random.normal(k, shape, jnp.float32) * s

    return {
        'x': nrm(ks[0], (BATCH, SEQ, D), 1.0),
        'c': nrm(ks[1], (BATCH, D), 1.0),
        'ctx': nrm(ks[2], (BATCH, CTX_LEN, D), 1.0),
        'c_ctx': nrm(ks[3], (D,), 1.0),
        'w_mod': nrm(ks[4], (DEPTH, D, 6 * D), 0.5 * D ** -0.5),
        'b_mod': nrm(ks[5], (DEPTH, 6 * D), 0.02),
        'norm1_g': 1.0 + nrm(ks[6], (DEPTH, D), 0.1),
        'norm2_g': 1.0 + nrm(ks[7], (DEPTH, D), 0.1),
        'w_in': nrm(ks[8], (DEPTH, D, D_IN), D ** -0.5),
        'conv_w': nrm(ks[9], (DEPTH, CONV_K, BRANCH_W), CONV_K ** -0.5),
        'attn_sink': nrm(ks[10], (DEPTH, ATT_HEADS), 0.5),
        'gla_w2': nrm(ks[11], (DEPTH, 2, GLA_RANK, GLA_HEADS * GLA_DK), GLA_RANK ** -0.5),
        'gla_b': 1.0 + nrm(ks[12], (DEPTH, 2, GLA_HEADS * GLA_DK), 0.5),
        'gla_norm_g': 1.0 + nrm(ks[13], (DEPTH, GLA_HEADS * GLA_DV), 0.1),
        'w_branch': nrm(ks[14], (DEPTH, N_BRANCH, BRANCH_W, D), BRANCH_W ** -0.5),
        'w_out': nrm(ks[15], (DEPTH, D, D), D ** -0.5),
        'ffn_w1': nrm(ks[16], (N_DENSE, D, D_FF), D ** -0.5),
        'ffn_w3': nrm(ks[17], (N_DENSE, D, D_FF), D ** -0.5),
        'ffn_w2': nrm(ks[18], (N_DENSE, D_FF, D), D_FF ** -0.5),
        'router_w': nrm(ks[19], (N_MOE, D, N_EXPERTS), D ** -0.5),
        'moe_w1': nrm(ks[20], (N_MOE, N_EXPERTS, D, D_FF), D ** -0.5),
        'moe_w3': nrm(ks[21], (N_MOE, N_EXPERTS, D, D_FF), D ** -0.5),
        'moe_w2': nrm(ks[22], (N_MOE, N_EXPERTS, D_FF, D), D_FF ** -0.5),
        'final_norm_g': 1.0 + nrm(ks[23], (D,), 0.1),
    }


def reference(x, c, ctx, c_ctx, w_mod, b_mod, norm1_g, norm2_g, w_in, conv_w, attn_sink, gla_w2, gla_b,
              gla_norm_g, w_branch, w_out, ffn_w1, ffn_w3, ffn_w2, router_w, moe_w1, moe_w3, moe_w2,
              final_norm_g):
    angles = axial_rope_angles(x.shape[1])
    silu_c = jax.nn.silu(c)
    silu_cc = jax.nn.silu(c_ctx)
    for l in range(DEPTH):
        last = l == DEPTH - 1
        mod_x = jnp.split((silu_c @ w_mod[l] + b_mod[l])[:, None, :], 6, axis=-1)
        mod_c = jnp.split(silu_cc @ w_mod[l] + b_mod[l], 6, axis=-1)
        hx = modulate(rmsnorm(x, norm1_g[l]), mod_x[0], mod_x[1])
        hc = modulate(rmsnorm(ctx, norm1_g[l]), mod_c[0], mod_c[1])
        y_x, y_c = parallel_mixers(hx, hc, angles, w_in[l], conv_w[l], attn_sink[l], gla_w2[l], gla_b[l],
                                   gla_norm_g[l], w_branch[l], w_out[l], not last)
        x = x + mod_x[2] * y_x
        hx2 = modulate(rmsnorm(x, norm2_g[l]), mod_x[3], mod_x[4])
        x = x + mod_x[5] * channel_mixer(hx2, l, ffn_w1, ffn_w3, ffn_w2, router_w, moe_w1, moe_w3, moe_w2)
        if not last:
            ctx = ctx + mod_c[2] * y_c
            hc2 = modulate(rmsnorm(ctx, norm2_g[l]), mod_c[3], mod_c[4])
            ctx = ctx + mod_c[5] * channel_mixer(hc2, l, ffn_w1, ffn_w3, ffn_w2, router_w, moe_w1, moe_w3, moe_w2)
    return rmsnorm(x, final_norm_g)
```

```python
import functools

import jax
import jax.numpy as jnp
from jax import lax
from jax.experimental import pallas as pl
from jax.experimental.pallas import tpu as pltpu

F32 = jnp.float32
BF16 = jnp.bfloat16

D = 1024
SEQ = 2048
CTX = 256
ROWS = CTX + SEQ
GRID_W = 64
EPS = 1e-6
NEG_INF = -1e30

BRANCH_W = 512
ATT_HEADS = 8
ATT_KV = 2
ATT_GROUP = ATT_HEADS // ATT_KV
HEAD_DIM = 64
QB = 128
ROPE_BASE = 10000.0
GLA_HEADS = 4
GLA_DK = 64
GLA_DV = 128
GLA_RANK = 16
GLA_TAU = 16.0
GLA_CHUNK = 64
D_FF = 3584
N_EXPERTS = 8
TOP_K = 2

C_CB, C_CC, C_CU = 0, 512, 1024
C_QA, C_KA, C_VA = 1536, 2048, 2176
C_QG, C_KG, C_AG = 2304, 2560, 2816
C_VG, C_RG, C_GATE = 3072, 3584, 4096
DP = 7168
AG_W = 128

TM = 256
TILES_PER_BATCH = ROWS // TM
PROJ_TN = 1024
FFN_TM = 768
FFN_TF = 512
VMEM_LIMIT = 56 * 1024 * 1024


def _params(*sem):
    return pltpu.CompilerParams(dimension_semantics=sem, vmem_limit_bytes=VMEM_LIMIT)


def _dot(a, b):
    return jnp.dot(a, b, preferred_element_type=F32)


def _dot_nt(a, b):
    return lax.dot_general(a, b, (((1,), (1,)), ((), ())), preferred_element_type=F32)


def _dot_tn(a, b):
    return lax.dot_general(a, b, (((0,), (0,)), ((), ())), preferred_element_type=F32)


def _norm_mod(x, g, shift, scale):
    h = x * lax.rsqrt(jnp.mean(x * x, axis=-1, keepdims=True) + EPS) * g
    return h * (1.0 + scale) + shift


def _mod_kernel(s_ref, w_ref, b_ref, o_ref):
    s = s_ref[...]
    s = s * jax.nn.sigmoid(s)
    o_ref[...] = _dot(s.astype(BF16), w_ref[...].astype(BF16)) + b_ref[...]


def _modulation(cond, w_mod, b_mod):
    depth = w_mod.shape[0]
    nrow = cond.shape[0]
    return pl.pallas_call(
        _mod_kernel,
        grid=(depth, 6),
        in_specs=[
            pl.BlockSpec((nrow, D), lambda l, j: (0, 0)),
            pl.BlockSpec((None, D, D), lambda l, j: (l, 0, j)),
            pl.BlockSpec((None, 1, D), lambda l, j: (l, 0, j)),
        ],
        out_specs=pl.BlockSpec((None, nrow, D), lambda l, j: (l, 0, j)),
        out_shape=jax.ShapeDtypeStruct((depth, nrow, 6 * D), F32),
        compiler_params=_params("parallel", "parallel"),
        name="modulation",
    )(cond, w_mod, b_mod.reshape(depth, 1, 6 * D))


def _mod_row(i, bsz):
    return jnp.where(i % TILES_PER_BATCH == 0, bsz, i // TILES_PER_BATCH)


def _mod_spec(j, bsz):
    return pl.BlockSpec((None, 1, D), lambda i: (_mod_row(i, bsz), 0, j))


def _proj_in_kernel(x_ref, g_ref, sh_ref, sc_ref, w_ref, o_ref):
    h = _norm_mod(x_ref[...], g_ref[...], sh_ref[...], sc_ref[...]).astype(BF16)
    for n0 in range(0, DP, PROJ_TN):
        o_ref[:, n0:n0 + PROJ_TN] = _dot(h, w_ref[:, n0:n0 + PROJ_TN]).astype(BF16)


def _proj_in(x, g, mod, w, bsz):
    r = x.shape[0]
    return pl.pallas_call(
        _proj_in_kernel,
        grid=(r // TM,),
        in_specs=[
            pl.BlockSpec((TM, D), lambda i: (i, 0)),
            pl.BlockSpec((1, D), lambda i: (0, 0)),
            _mod_spec(0, bsz),
            _mod_spec(1, bsz),
            pl.BlockSpec((D, DP), lambda i: (0, 0), pipeline_mode=pl.Buffered(1)),
        ],
        out_specs=pl.BlockSpec((TM, DP), lambda i: (i, 0)),
        out_shape=jax.ShapeDtypeStruct((r, DP), BF16),
        compiler_params=_params("parallel"),
        name="proj_in",
    )(x, g, mod, mod, w)


HALO = 16


def _conv_kernel(cb_ref, cc_ref, cu_ref, ccp_ref, cup_ref, ccn_ref, cun_ref, w_ref, o_ref):
    j = pl.program_id(0) % TILES_PER_BATCH
    first = (j == 0) | (j == 1)
    last = (j == 0) | (j == TILES_PER_BATCH - 1)
    z = cc_ref[...].astype(F32) * cu_ref[...].astype(F32)
    zp = ccp_ref[HALO - 1:HALO, :].astype(F32) * cup_ref[HALO - 1:HALO, :].astype(F32)
    zn = ccn_ref[0:1, :].astype(F32) * cun_ref[0:1, :].astype(F32)
    zp = jnp.where(first, 0.0, zp)
    zn = jnp.where(last, 0.0, zn)
    row = lax.broadcasted_iota(jnp.int32, z.shape, 0)
    z_prev = jnp.where(row == 0, zp, pltpu.roll(z, 1, 0))
    z_next = jnp.where(row == TM - 1, zn, pltpu.roll(z, TM - 1, 0))
    w = w_ref[...]
    y = z_prev * w[0:1] + z * w[1:2] + z_next * w[2:3]
    o_ref[...] = (cb_ref[...].astype(F32) * y).astype(BF16)


def _conv(p, conv_w):
    r = p.shape[0]
    hb = TM // HALO
    nh = r // HALO
    col = lambda c: c // BRANCH_W
    tile = lambda c: pl.BlockSpec((TM, BRANCH_W), lambda i: (i, col(c)))
    prev = lambda c: pl.BlockSpec((HALO, BRANCH_W), lambda i: (jnp.maximum(i * hb - 1, 0), col(c)))
    nxt = lambda c: pl.BlockSpec((HALO, BRANCH_W), lambda i: (jnp.minimum((i + 1) * hb, nh - 1), col(c)))
    return pl.pallas_call(
        _conv_kernel,
        grid=(r // TM,),
        in_specs=[tile(C_CB), tile(C_CC), tile(C_CU), prev(C_CC), prev(C_CU), nxt(C_CC), nxt(C_CU),
                  pl.BlockSpec((3, BRANCH_W), lambda i: (0, 0))],
        out_specs=pl.BlockSpec((TM, BRANCH_W), lambda i: (i, 0)),
        out_shape=jax.ShapeDtypeStruct((r, BRANCH_W), BF16),
        compiler_params=_params("parallel"),
        name="short_conv",
    )(p, p, p, p, p, p, p, conv_w)


NQB = ROWS // QB
NCB = CTX // QB
NLK = 3 * QB


def _rope(x, cos, sin):
    w = x.shape[-1]
    lane = lax.broadcasted_iota(jnp.int32, x.shape, 1)
    partner = jnp.where((lane % 32) < 16, pltpu.roll(x, w - 16, 1), pltpu.roll(x, 16, 1))
    return x * cos + partner * sin


def _attn_kernel(sink_ref, q_ref, kp_ref, ko_ref, kn_ref, vp_ref, vo_ref, vn_ref, kc_ref, vc_ref,
                 cq_ref, sq_ref, cp_ref, sp_ref, cn_ref, sn_ref, o_ref):
    n = pl.program_id(1)
    is_ctx = n < NCB
    cq = jnp.tile(cq_ref[...], (1, ATT_HEADS // 2))
    sq = jnp.tile(sq_ref[...], (1, ATT_HEADS // 2))
    qr = (_rope(q_ref[...].astype(F32), cq, sq) * (HEAD_DIM ** -0.5)).astype(BF16)
    k_p = _rope(kp_ref[...].astype(F32), cp_ref[...], sp_ref[...]).astype(BF16)
    k_o = _rope(ko_ref[...].astype(F32), cq_ref[...], sq_ref[...]).astype(BF16)
    k_n = _rope(kn_ref[...].astype(F32), cn_ref[...], sn_ref[...]).astype(BF16)
    kcat = jnp.concatenate([k_p, k_o, k_n, kc_ref[...]], axis=0)
    vcat = jnp.concatenate([vp_ref[...], vo_ref[...], vn_ref[...], vc_ref[...]], axis=0)
    nk = NLK + CTX
    i = lax.broadcasted_iota(jnp.int32, (QB, nk), 0)
    j = lax.broadcasted_iota(jnp.int32, (QB, nk), 1)
    far = 4 * QB
    pen_p = jnp.where(n > NCB, 0, far)
    pen_o = jnp.where(is_ctx, far, 0)
    pen_n = jnp.where((n >= NCB) & (n < NQB - 1), 0, far)
    valid = (((j < QB) & (j >= i + pen_p))
             | ((j >= QB) & (j < 2 * QB) & (j >= pen_o))
             | ((j >= 2 * QB) & (j < NLK) & (j - 2 * QB + pen_n <= i))
             | (j >= NLK))
    outs = []
    for h in range(ATT_HEADS):
        kv = h // ATT_GROUP
        kk = kcat[:, kv * HEAD_DIM:(kv + 1) * HEAD_DIM]
        vv = vcat[:, kv * HEAD_DIM:(kv + 1) * HEAD_DIM]
        s = _dot_nt(qr[:, h * HEAD_DIM:(h + 1) * HEAD_DIM], kk)
        s = jnp.where(valid, s, NEG_INF)
        sk = sink_ref[h]
        m = jnp.maximum(jnp.max(s, axis=-1, keepdims=True), sk)
        p = jnp.exp(s - m)
        den = jnp.sum(p, axis=-1, keepdims=True) + jnp.exp(sk - m)
        outs.append(_dot(p.astype(BF16), vv) / den)
    o_ref[...] = jnp.concatenate(outs, axis=1).astype(BF16)


def _rope_tables():
    half = HEAD_DIM // 2
    t = jnp.arange(SEQ)
    row = (t // GRID_W).astype(F32)
    col = (t % GRID_W).astype(F32)
    inv_freq = ROPE_BASE ** (-jnp.arange(0, half, 2, dtype=F32) / half)
    ar = row[:, None] * inv_freq
    ac = col[:, None] * inv_freq
    cos = jnp.concatenate([jnp.cos(ar), jnp.cos(ar), jnp.cos(ac), jnp.cos(ac)], axis=1)
    sin = jnp.concatenate([-jnp.sin(ar), jnp.sin(ar), -jnp.sin(ac), jnp.sin(ac)], axis=1)
    cos = jnp.concatenate([cos, jnp.ones((QB, HEAD_DIM), F32)], axis=0)
    sin = jnp.concatenate([sin, jnp.zeros((QB, HEAD_DIM), F32)], axis=0)
    return jnp.tile(cos, (1, 2)), jnp.tile(sin, (1, 2))


def _attention(p, sink, cos, sin, bsz):
    r = p.shape[0]
    nlb = SEQ // QB
    lo, hi = NCB, NQB - 1

    def kblk(shift, c):
        return pl.BlockSpec((QB, 2 * HEAD_DIM),
                            lambda b, n: (b * NQB + jnp.clip(n + shift, lo, hi), c // (2 * HEAD_DIM)))

    def tblk(shift):
        def idx(b, n):
            pos = jnp.clip(n + shift, lo, hi) - NCB
            if shift == 0:
                pos = jnp.where(n < NCB, nlb, pos)
            return (pos, 0)
        return pl.BlockSpec((QB, 2 * HEAD_DIM), idx)

    cblk = lambda c: pl.BlockSpec((CTX, 2 * HEAD_DIM), lambda b, n: (b * (ROWS // CTX), c // (2 * HEAD_DIM)))
    return pl.pallas_call(
        _attn_kernel,
        grid=(bsz, NQB),
        in_specs=[
            pl.BlockSpec(memory_space=pltpu.SMEM),
            pl.BlockSpec((QB, ATT_HEADS * HEAD_DIM), lambda b, n: (b * NQB + n, C_QA // (ATT_HEADS * HEAD_DIM))),
            kblk(-1, C_KA), kblk(0, C_KA), kblk(1, C_KA),
            kblk(-1, C_VA), kblk(0, C_VA), kblk(1, C_VA),
            cblk(C_KA), cblk(C_VA),
            tblk(0), tblk(0), tblk(-1), tblk(-1), tblk(1), tblk(1),
        ],
        out_specs=pl.BlockSpec((QB, ATT_HEADS * HEAD_DIM), lambda b, n: (b * NQB + n, 0)),
        out_shape=jax.ShapeDtypeStruct((r, ATT_HEADS * HEAD_DIM), BF16),
        compiler_params=_params("parallel", "parallel"),
        name="window_attention",
    )(sink, p, p, p, p, p, p, p, p, p, cos, sin, cos, sin, cos, sin)


NCHUNK = ROWS // GLA_CHUNK
NCHUNK_CTX = CTX // GLA_CHUNK
GQ = GLA_HEADS * GLA_DK
GV = GLA_HEADS * GLA_DV


def _log_sigmoid(z):
    return jnp.minimum(z, 0.0) - jnp.log1p(jnp.exp(-jnp.abs(z)))


def _gla_kernel(q_ref, k_ref, a_ref, v_ref, r_ref, w2_ref, b2_ref, g_ref, o_ref,
                of_ref, ob_ref, sf_ref, sb_ref):
    sf_ref[...] = jnp.zeros_like(sf_ref)
    sb_ref[...] = jnp.zeros_like(sb_ref)
    ci = lax.broadcasted_iota(jnp.int32, (GLA_CHUNK, GLA_CHUNK), 0)
    cj = lax.broadcasted_iota(jnp.int32, (GLA_CHUNK, GLA_CHUNK), 1)
    bi = lax.broadcasted_iota(jnp.int32, (GV, GQ), 0)
    bj = lax.broadcasted_iota(jnp.int32, (GV, GQ), 1)
    block_diag = (bi // GLA_DV) == (bj // GLA_DK)

    def chunk(c, d, s_ref, out_ref):
        causal = (ci >= cj) if d == 0 else (ci <= cj)
        rows = pl.ds(pl.multiple_of(c * GLA_CHUNK, GLA_CHUNK), GLA_CHUNK)
        z = _dot(a_ref[rows, :], w2_ref[d]) + b2_ref[d]
        la = _log_sigmoid(z) * (1.0 / GLA_TAU)
        b = jnp.dot(causal.astype(F32), la, preferred_element_type=F32, precision=lax.Precision.HIGHEST)
        tot = b[GLA_CHUNK - 1:GLA_CHUNK] if d == 0 else b[0:1]
        q = q_ref[rows, :].astype(F32) * (GLA_DK ** -0.5)
        k = k_ref[rows, :].astype(F32)
        v = v_ref[rows, :]
        q_dec = (q * jnp.exp(b)).astype(BF16)
        k_inv = (k * jnp.exp(-b)).astype(BF16)
        k_end = (k * jnp.exp(tot - b)).astype(BF16)
        s = s_ref[...]
        o = _dot_nt(q_dec, s.astype(BF16))
        intra = []
        for h in range(GLA_HEADS):
            qs = slice(h * GLA_DK, (h + 1) * GLA_DK)
            att = jnp.where(causal, _dot_nt(q_dec[:, qs], k_inv[:, qs]), 0.0)
            intra.append(_dot(att.astype(BF16), v[:, h * GLA_DV:(h + 1) * GLA_DV]))
        out_ref[rows, :] = o + jnp.concatenate(intra, axis=1)
        ds = _dot_tn(v, k_end)
        s_ref[...] = s * jnp.exp(tot) + jnp.where(block_diag, ds, 0.0)

    def body(i, carry):
        chunk(i, 0, sf_ref, of_ref)
        cb = jnp.where(i < NCHUNK_CTX, NCHUNK_CTX - 1 - i, NCHUNK + NCHUNK_CTX - 1 - i)
        chunk(cb, 1, sb_ref, ob_ref)
        return carry

    lax.fori_loop(0, NCHUNK, body, 0)

    def finish(t, carry):
        rows = pl.ds(pl.multiple_of(t * TM, TM), TM)
        o = of_ref[rows, :] + ob_ref[rows, :]
        parts = []
        for h in range(GLA_HEADS):
            oh = o[:, h * GLA_DV:(h + 1) * GLA_DV]
            parts.append(oh * lax.rsqrt(jnp.mean(oh * oh, axis=-1, keepdims=True) + EPS))
        on = jnp.concatenate(parts, axis=1) * g_ref[...]
        rg = r_ref[rows, :].astype(F32)
        o_ref[rows, :] = (on * (rg * jax.nn.sigmoid(rg))).astype(BF16)
        return carry

    lax.fori_loop(0, ROWS // TM, finish, 0)


def _gla(p, w2p, b2, g, bsz):
    r = p.shape[0]
    blk = lambda c, w: pl.BlockSpec((ROWS, w), lambda b: (b, c // w))
    return pl.pallas_call(
        _gla_kernel,
        grid=(bsz,),
        in_specs=[
            blk(C_QG, GQ), blk(C_KG, GQ), blk(C_AG, AG_W), blk(C_VG, GV), blk(C_RG, GV),
            pl.BlockSpec((2, AG_W, GQ), lambda b: (0, 0, 0)),
            pl.BlockSpec((2, 1, GQ), lambda b: (0, 0, 0)),
            pl.BlockSpec((1, GV), lambda b: (0, 0)),
        ],
        out_specs=pl.BlockSpec((ROWS, GV), lambda b: (b, 0)),
        out_shape=jax.ShapeDtypeStruct((r, GV), BF16),
        scratch_shapes=[
            pltpu.VMEM((ROWS, GV), F32), pltpu.VMEM((ROWS, GV), F32),
            pltpu.VMEM((GV, GQ), F32), pltpu.VMEM((GV, GQ), F32),
        ],
        compiler_params=_params("parallel"),
        name="gla",
    )(p, p, p, p, p, w2p, b2, g)


def _merge_kernel(ya_ref, yb_ref, yc_ref, g0_ref, g1_ref, g2_ref, x_ref, gm_ref, wb_ref, wo_ref, o_ref):
    acc = jax.nn.sigmoid(g0_ref[...].astype(F32)) * _dot(ya_ref[...], wb_ref[0])
    acc += jax.nn.sigmoid(g1_ref[...].astype(F32)) * _dot(yb_ref[...], wb_ref[1])
    acc += jax.nn.sigmoid(g2_ref[...].astype(F32)) * _dot(yc_ref[...], wb_ref[2])
    y = _dot(acc.astype(BF16), wo_ref[...])
    o_ref[...] = x_ref[...] + gm_ref[...] * y


def _merge(ya, yb, yc, p, x, mod, wb, wo, bsz):
    r = x.shape[0]
    br = pl.BlockSpec((TM, BRANCH_W), lambda i: (i, 0))
    gate = lambda k: pl.BlockSpec((TM, D), lambda i: (i, C_GATE // D + k))
    return pl.pallas_call(
        _merge_kernel,
        grid=(r // TM,),
        in_specs=[br, br, br, gate(0), gate(1), gate(2),
                  pl.BlockSpec((TM, D), lambda i: (i, 0)),
                  _mod_spec(2, bsz),
                  pl.BlockSpec((3, BRANCH_W, D), lambda i: (0, 0, 0), pipeline_mode=pl.Buffered(1)),
                  pl.BlockSpec((D, D), lambda i: (0, 0), pipeline_mode=pl.Buffered(1))],
        out_specs=pl.BlockSpec((TM, D), lambda i: (i, 0)),
        out_shape=jax.ShapeDtypeStruct((r, D), F32),
        compiler_params=_params("parallel"),
        name="merge",
    )(ya, yb, yc, p, p, p, x, mod, wb, wo)


ROUTER_W = 128


def _router_kernel(x_ref, g_ref, sh_ref, sc_ref, w_ref, o_ref):
    h = _norm_mod(x_ref[...], g_ref[...], sh_ref[...], sc_ref[...])
    o_ref[...] = jnp.dot(h, w_ref[...], preferred_element_type=F32, precision=lax.Precision.HIGHEST)


def _router(x, g, mod, w, bsz):
    r = x.shape[0]
    return pl.pallas_call(
        _router_kernel,
        grid=(r // TM,),
        in_specs=[pl.BlockSpec((TM, D), lambda i: (i, 0)),
                  pl.BlockSpec((1, D), lambda i: (0, 0)),
                  _mod_spec(3, bsz), _mod_spec(4, bsz),
                  pl.BlockSpec((D, ROUTER_W), lambda i: (0, 0))],
        out_specs=pl.BlockSpec((TM, ROUTER_W), lambda i: (i, 0)),
        out_shape=jax.ShapeDtypeStruct((r, ROUTER_W), F32),
        compiler_params=_params("parallel"),
        name="router",
    )(x, g, mod, mod, w)


FFN_TILES_PER_BATCH = ROWS // FFN_TM


def _ffn_kernel(moe, x_ref, g_ref, mx_ref, mc_ref, *rest):
    if moe:
        gates_ref, w1_ref, w3_ref, w2_ref, o_ref, h_ref, acc_ref = rest
    else:
        w1_ref, w3_ref, w2_ref, o_ref, h_ref, acc_ref = rest
    e = pl.program_id(1)
    f = pl.program_id(2)
    row = lax.broadcasted_iota(jnp.int32, (FFN_TM, 1), 0) + (pl.program_id(0) % FFN_TILES_PER_BATCH) * FFN_TM
    is_ctx = row < CTX

    def mod(j):
        return jnp.where(is_ctx, mc_ref[:, j * D:(j + 1) * D], mx_ref[:, j * D:(j + 1) * D])

    @pl.when((e == 0) & (f == 0))
    def _():
        h_ref[...] = _norm_mod(x_ref[...], g_ref[...], mod(3), mod(4)).astype(BF16)
        acc_ref[...] = jnp.zeros_like(acc_ref)

    h = h_ref[...]
    a = _dot(h, w1_ref[...])
    t = a * jax.nn.sigmoid(a) * _dot(h, w3_ref[...])
    if moe:
        lane = lax.broadcasted_iota(jnp.int32, (FFN_TM, N_EXPERTS), 1)
        t = t * jnp.sum(jnp.where(lane == e, gates_ref[...], 0.0), axis=1, keepdims=True)
    acc_ref[...] += _dot(t.astype(BF16), w2_ref[...])

    @pl.when((e == pl.num_programs(1) - 1) & (f == pl.num_programs(2) - 1))
    def _():
        o_ref[...] = x_ref[...] + mod(5) * acc_ref[...]


def _ffn(x, g, mod, w1, w3, w2, bsz, gates=None):
    r = x.shape[0]
    ne = w1.shape[0]
    moe = gates is not None
    row = pl.BlockSpec((FFN_TM, D), lambda i, e, f: (i, 0))
    in_specs = [row,
                pl.BlockSpec((1, D), lambda i, e, f: (0, 0)),
                pl.BlockSpec((None, 1, 6 * D), lambda i, e, f: (i // FFN_TILES_PER_BATCH, 0, 0)),
                pl.BlockSpec((None, 1, 6 * D), lambda i, e, f: (bsz, 0, 0))]
    args = [x, g, mod, mod]
    if moe:
        in_specs.append(pl.BlockSpec((FFN_TM, N_EXPERTS), lambda i, e, f: (i, 0)))
        args.append(gates)
    in_specs += [pl.BlockSpec((None, D, FFN_TF), lambda i, e, f: (e, 0, f)),
                 pl.BlockSpec((None, D, FFN_TF), lambda i, e, f: (e, 0, f)),
                 pl.BlockSpec((None, FFN_TF, D), lambda i, e, f: (e, f, 0))]
    args += [w1, w3, w2]
    return pl.pallas_call(
        functools.partial(_ffn_kernel, moe),
        grid=(r // FFN_TM, ne, D_FF // FFN_TF),
        in_specs=in_specs,
        out_specs=row,
        out_shape=jax.ShapeDtypeStruct((r, D), F32),
        scratch_shapes=[pltpu.VMEM((FFN_TM, D), BF16), pltpu.VMEM((FFN_TM, D), F32)],
        compiler_params=_params("parallel", "arbitrary", "arbitrary"),
        name="moe_ffn" if moe else "ffn",
    )(*args)


def _final_kernel(x_ref, g_ref, o_ref):
    x = x_ref[...]
    o_ref[...] = x * lax.rsqrt(jnp.mean(x * x, axis=-1, keepdims=True) + EPS) * g_ref[...]


def _final_norm(x, g, bsz):
    lat = SEQ // TM
    return pl.pallas_call(
        _final_kernel,
        grid=(bsz, lat),
        in_specs=[pl.BlockSpec((TM, D), lambda b, j: (b * TILES_PER_BATCH + CTX // TM + j, 0)),
                  pl.BlockSpec((1, D), lambda b, j: (0, 0))],
        out_specs=pl.BlockSpec((TM, D), lambda b, j: (b * lat + j, 0)),
        out_shape=jax.ShapeDtypeStruct((bsz * SEQ, D), F32),
        compiler_params=_params("parallel", "parallel"),
        name="final_norm",
    )(x, g)


def _layout_w_in(w_in):
    depth = w_in.shape[0]
    o_vg = 2816
    o_rg = o_vg + GV
    o_ag = o_rg + GV
    o_gate = o_ag + 2 * GLA_RANK
    z = lambda n: jnp.zeros((depth, D, n), w_in.dtype)
    parts = [w_in[..., :o_vg],
             w_in[..., o_ag:o_gate], z(C_VG - C_AG - 2 * GLA_RANK),
             w_in[..., o_vg:o_ag],
             w_in[..., o_gate:]]
    return jnp.concatenate(parts, axis=-1).astype(BF16)


def _layout_gla_w2(gla_w2):
    depth = gla_w2.shape[0]
    out = jnp.zeros((depth, 2, AG_W, GQ), F32)
    for d in range(2):
        out = out.at[:, d, d * GLA_RANK:(d + 1) * GLA_RANK, :].set(gla_w2[:, d])
    return out.astype(BF16)


def _moe_gates(logits):
    top_v, top_i = lax.top_k(logits, TOP_K)
    top_p = jax.nn.softmax(top_v, axis=-1)
    return jnp.sum(jax.nn.one_hot(top_i, N_EXPERTS, dtype=F32) * top_p[..., None], axis=-2)


def kernel(x, c, ctx, c_ctx, w_mod, b_mod, norm1_g, norm2_g, w_in, conv_w, attn_sink, gla_w2, gla_b,
           gla_norm_g, w_branch, w_out, ffn_w1, ffn_w3, ffn_w2, router_w, moe_w1, moe_w3, moe_w2,
           final_norm_g):
    bsz = x.shape[0]
    depth = w_in.shape[0]
    assert x.shape[1:] == (SEQ, D) and ctx.shape[1:] == (CTX, D)

    nrow = -(-(bsz + 1) // 8) * 8
    cond = jnp.zeros((nrow, D), F32).at[:bsz].set(c).at[bsz].set(c_ctx)
    mods = _modulation(cond, w_mod, b_mod).reshape(depth, nrow, 1, 6 * D)

    xs = jnp.concatenate([ctx, x], axis=1).reshape(bsz * ROWS, D)
    w_in_p = _layout_w_in(w_in)
    w2p = _layout_gla_w2(gla_w2)
    cos, sin = _rope_tables()
    router_p = jnp.pad(router_w, ((0, 0), (0, 0), (0, ROUTER_W - N_EXPERTS)))

    for l in range(depth):
        mod = mods[l]
        p = _proj_in(xs, norm1_g[l].reshape(1, D), mod, w_in_p[l], bsz)
        ya = _conv(p, conv_w[l])
        yb = _attention(p, attn_sink[l], cos, sin, bsz)
        yc = _gla(p, w2p[l], gla_b[l].reshape(2, 1, GQ), gla_norm_g[l].reshape(1, GV), bsz)
        xs = _merge(ya, yb, yc, p, xs, mod, w_branch[l].astype(BF16), w_out[l].astype(BF16), bsz)
        g2 = norm2_g[l].reshape(1, D)
        j = l // 2
        if l % 2 == 0:
            xs = _ffn(xs, g2, mod, ffn_w1[j:j + 1].astype(BF16), ffn_w3[j:j + 1].astype(BF16),
                      ffn_w2[j:j + 1].astype(BF16), bsz)
        else:
            logits = _router(xs, g2, mod, router_p[j], bsz)
            gates = _moe_gates(logits[:, :N_EXPERTS])
            xs = _ffn(xs, g2, mod, moe_w1[j].astype(BF16), moe_w3[j].astype(BF16),
                      moe_w2[j].astype(BF16), bsz, gates=gates)
    out = _final_norm(xs, final_norm_g.reshape(1, D), bsz)
    return out.reshape(bsz, SEQ, D)
```

```python
import jax
import jax.numpy as jnp
from jax import lax
from jax.experimental import pallas as pl
from jax.experimental.pallas import tpu as pltpu

F32 = jnp.float32
BF16 = jnp.bfloat16

D = 1024
SEQ = 2048
CTX = 256
ROWS = CTX + SEQ
GRID_W = 64
EPS = 1e-6
NEG_INF = -1e30

BRANCH_W = 512
ATT_HEADS = 8
ATT_KV = 2
ATT_GROUP = ATT_HEADS // ATT_KV
HEAD_DIM = 64
QB = 128
ROPE_BASE = 10000.0
GLA_HEADS = 4
GLA_DK = 64
GLA_DV = 128
GLA_RANK = 16
GLA_TAU = 16.0
GLA_CHUNK = 64
D_FF = 3584
N_EXPERTS = 8
TOP_K = 2

C_CB, C_CC, C_CU = 0, 512, 1024
C_QA, C_KA, C_VA = 1536, 2048, 2176
C_QG, C_KG, C_AG = 2304, 2560, 2816
C_VG, C_RG, C_GATE = 3072, 3584, 4096
DP = 7168
AG_W = 128

TM = 256
TILES_PER_BATCH = ROWS // TM
PROJ_TN = 1024
FFN_TM = 768
FFN_TF = 512
VMEM_LIMIT = 56 * 1024 * 1024


def _params(*sem):
    return pltpu.CompilerParams(dimension_semantics=sem, vmem_limit_bytes=VMEM_LIMIT)


def _dot(a, b):
    return jnp.dot(a, b, preferred_element_type=F32)


def _dot_nt(a, b):
    return lax.dot_general(a, b, (((1,), (1,)), ((), ())), preferred_element_type=F32)


def _dot_tn(a, b):
    return lax.dot_general(a, b, (((0,), (0,)), ((), ())), preferred_element_type=F32)


def _norm_mod(x, g, shift, scale):
    h = x * lax.rsqrt(jnp.mean(x * x, axis=-1, keepdims=True) + EPS) * g
    return h * (1.0 + scale) + shift


def _mod_kernel(s_ref, w_ref, b_ref, o_ref):
    s = s_ref[...]
    s = s * jax.nn.sigmoid(s)
    o_ref[...] = _dot(s.astype(BF16), w_ref[...].astype(BF16)) + b_ref[...]


def _modulation(cond, w_mod, b_mod):
    depth = w_mod.shape[0]
    nrow = cond.shape[0]
    return pl.pallas_call(
        _mod_kernel,
        grid=(depth, 6),
        in_specs=[
            pl.BlockSpec((nrow, D), lambda l, j: (0, 0)),
            pl.BlockSpec((None, D, D), lambda l, j: (l, 0, j)),
            pl.BlockSpec((None, 1, D), lambda l, j: (l, 0, j)),
        ],
        out_specs=pl.BlockSpec((None, nrow, D), lambda l, j: (l, 0, j)),
        out_shape=jax.ShapeDtypeStruct((depth, nrow, 6 * D), F32),
        compiler_params=_params("parallel", "parallel"),
        name="modulation",
    )(cond, w_mod, b_mod.reshape(depth, 1, 6 * D))


def _mod_row(i, bsz):
    return jnp.where(i % TILES_PER_BATCH == 0, bsz, i // TILES_PER_BATCH)


def _mod_spec(j, bsz):
    return pl.BlockSpec((None, 1, D), lambda i: (_mod_row(i, bsz), 0, j))


def _proj_in_kernel(x_ref, g_ref, sh_ref, sc_ref, w_ref, o_ref):
    h = _norm_mod(x_ref[...], g_ref[...], sh_ref[...], sc_ref[...]).astype(BF16)
    for n0 in range(0, DP, PROJ_TN):
        o_ref[:, n0:n0 + PROJ_TN] = _dot(h, w_ref[:, n0:n0 + PROJ_TN]).astype(BF16)


def _proj_in(x, g, mod, w, bsz):
    r = x.shape[0]
    return pl.pallas_call(
        _proj_in_kernel,
        grid=(r // TM,),
        in_specs=[
            pl.BlockSpec((TM, D), lambda i: (i, 0)),
            pl.BlockSpec((1, D), lambda i: (0, 0)),
            _mod_spec(0, bsz),
            _mod_spec(1, bsz),
            pl.BlockSpec((D, DP), lambda i: (0, 0), pipeline_mode=pl.Buffered(1)),
        ],
        out_specs=pl.BlockSpec((TM, DP), lambda i: (i, 0)),
        out_shape=jax.ShapeDtypeStruct((r, DP), BF16),
        compiler_params=_params("parallel"),
        name="proj_in",
    )(x, g, mod, mod, w)


HALO = 16


def _conv_kernel(cb_ref, cc_ref, cu_ref, ccp_ref, cup_ref, ccn_ref, cun_ref, w_ref, o_ref):
    j = pl.program_id(0) % TILES_PER_BATCH
    first = (j == 0) | (j == 1)
    last = (j == 0) | (j == TILES_PER_BATCH - 1)
    z = cc_ref[...].astype(F32) * cu_ref[...].astype(F32)
    zp = ccp_ref[HALO - 1:HALO, :].astype(F32) * cup_ref[HALO - 1:HALO, :].astype(F32)
    zn = ccn_ref[0:1, :].astype(F32) * cun_ref[0:1, :].astype(F32)
    zp = jnp.where(first, 0.0, zp)
    zn = jnp.where(last, 0.0, zn)
    row = lax.broadcasted_iota(jnp.int32, z.shape, 0)
    z_prev = jnp.where(row == 0, zp, pltpu.roll(z, 1, 0))
    z_next = jnp.where(row == TM - 1, zn, pltpu.roll(z, TM - 1, 0))
    w = w_ref[...]
    y = z_prev * w[0:1] + z * w[1:2] + z_next * w[2:3]
    o_ref[...] = (cb_ref[...].astype(F32) * y).astype(BF16)


def _conv(p, conv_w):
    r = p.shape[0]
    hb = TM // HALO
    nh = r // HALO
    col = lambda c: c // BRANCH_W
    tile = lambda c: pl.BlockSpec((TM, BRANCH_W), lambda i: (i, col(c)))
    prev = lambda c: pl.BlockSpec((HALO, BRANCH_W), lambda i: (jnp.maximum(i * hb - 1, 0), col(c)))
    nxt = lambda c: pl.BlockSpec((HALO, BRANCH_W), lambda i: (jnp.minimum((i + 1) * hb, nh - 1), col(c)))
    return pl.pallas_call(
        _conv_kernel,
        grid=(r // TM,),
        in_specs=[tile(C_CB), tile(C_CC), tile(C_CU), prev(C_CC), prev(C_CU), nxt(C_CC), nxt(C_CU),
                  pl.BlockSpec((3, BRANCH_W), lambda i: (0, 0))],
        out_specs=pl.BlockSpec((TM, BRANCH_W), lambda i: (i, 0)),
        out_shape=jax.ShapeDtypeStruct((r, BRANCH_W), BF16),
        compiler_params=_params("parallel"),
        name="short_conv",
    )(p, p, p, p, p, p, p, conv_w)


NQB = ROWS // QB
NCB = CTX // QB
NLK = 3 * QB


def _rope(x, cos, sin):
    w = x.shape[-1]
    lane = lax.broadcasted_iota(jnp.int32, x.shape, 1)
    partner = jnp.where((lane % 32) < 16, pltpu.roll(x, w - 16, 1), pltpu.roll(x, 16, 1))
    return x * cos + partner * sin


def _attn_kernel(sink_ref, q_ref, kp_ref, ko_ref, kn_ref, vp_ref, vo_ref, vn_ref, kc_ref, vc_ref,
                 cq_ref, sq_ref, cp_ref, sp_ref, cn_ref, sn_ref, o_ref):
    n = pl.program_id(1)
    is_ctx = n < NCB
    cq = jnp.tile(cq_ref[...], (1, ATT_HEADS // 2))
    sq = jnp.tile(sq_ref[...], (1, ATT_HEADS // 2))
    qr = (_rope(q_ref[...].astype(F32), cq, sq) * (HEAD_DIM ** -0.5)).astype(BF16)
    k_p = _rope(kp_ref[...].astype(F32), cp_ref[...], sp_ref[...]).astype(BF16)
    k_o = _rope(ko_ref[...].astype(F32), cq_ref[...], sq_ref[...]).astype(BF16)
    k_n = _rope(kn_ref[...].astype(F32), cn_ref[...], sn_ref[...]).astype(BF16)
    kcat = jnp.concatenate([k_p, k_o, k_n, kc_ref[...]], axis=0)
    vcat = jnp.concatenate([vp_ref[...], vo_ref[...], vn_ref[...], vc_ref[...]], axis=0)
    nk = NLK + CTX
    i = lax.broadcasted_iota(jnp.int32, (QB, nk), 0)
    j = lax.broadcasted_iota(jnp.int32, (QB, nk), 1)
    far = 4 * QB
    pen_p = jnp.where(n > NCB, 0, far)
    pen_o = jnp.where(is_ctx, far, 0)
    pen_n = jnp.where((n >= NCB) & (n < NQB - 1), 0, far)
    valid = (((j < QB) & (j >= i + pen_p))
             | ((j >= QB) & (j < 2 * QB) & (j >= pen_o))
             | ((j >= 2 * QB) & (j < NLK) & (j - 2 * QB + pen_n <= i))
             | (j >= NLK))
    outs = []
    for h in range(ATT_HEADS):
        kv = h // ATT_GROUP
        kk = kcat[:, kv * HEAD_DIM:(kv + 1) * HEAD_DIM]
        vv = vcat[:, kv * HEAD_DIM:(kv + 1) * HEAD_DIM]
        s = _dot_nt(qr[:, h * HEAD_DIM:(h + 1) * HEAD_DIM], kk)
        s = jnp.where(valid, s, NEG_INF)
        sk = sink_ref[h]
        m = jnp.maximum(jnp.max(s, axis=-1, keepdims=True), sk)
        p = jnp.exp(s - m)
        den = jnp.sum(p, axis=-1, keepdims=True) + jnp.exp(sk - m)
        outs.append(_dot(p.astype(BF16), vv) / den)
    o_ref[...] = jnp.concatenate(outs, axis=1).astype(BF16)


def _rope_tables():
    half = HEAD_DIM // 2
    t = jnp.arange(SEQ)
    row = (t // GRID_W).astype(F32)
    col = (t % GRID_W).astype(F32)
    inv_freq = ROPE_BASE ** (-jnp.arange(0, half, 2, dtype=F32) / half)
    ar = row[:, None] * inv_freq
    ac = col[:, None] * inv_freq
    cos = jnp.concatenate([jnp.cos(ar), jnp.cos(ar), jnp.cos(ac), jnp.cos(ac)], axis=1)
    sin = jnp.concatenate([-jnp.sin(ar), jnp.sin(ar), -jnp.sin(ac), jnp.sin(ac)], axis=1)
    cos = jnp.concatenate([cos, jnp.ones((QB, HEAD_DIM), F32)], axis=0)
    sin = jnp.concatenate([sin, jnp.zeros((QB, HEAD_DIM), F32)], axis=0)
    return jnp.tile(cos, (1, 2)), jnp.tile(sin, (1, 2))


def _attention(p, sink, cos, sin, bsz):
    r = p.shape[0]
    nlb = SEQ // QB
    lo, hi = NCB, NQB - 1

    def kblk(shift, c):
        return pl.BlockSpec((QB, 2 * HEAD_DIM),
                            lambda b, n: (b * NQB + jnp.clip(n + shift, lo, hi), c // (2 * HEAD_DIM)))

    def tblk(shift):
        def idx(b, n):
            pos = jnp.clip(n + shift, lo, hi) - NCB
            if shift == 0:
                pos = jnp.where(n < NCB, nlb, pos)
            return (pos, 0)
        return pl.BlockSpec((QB, 2 * HEAD_DIM), idx)

    cblk = lambda c: pl.BlockSpec((CTX, 2 * HEAD_DIM), lambda b, n: (b * (ROWS // CTX), c // (2 * HEAD_DIM)))
    return pl.pallas_call(
        _attn_kernel,
        grid=(bsz, NQB),
        in_specs=[
            pl.BlockSpec(memory_space=pltpu.SMEM),
            pl.BlockSpec((QB, ATT_HEADS * HEAD_DIM), lambda b, n: (b * NQB + n, C_QA // (ATT_HEADS * HEAD_DIM))),
            kblk(-1, C_KA), kblk(0, C_KA), kblk(1, C_KA),
            kblk(-1, C_VA), kblk(0, C_VA), kblk(1, C_VA),
            cblk(C_KA), cblk(C_VA),
            tblk(0), tblk(0), tblk(-1), tblk(-1), tblk(1), tblk(1),
        ],
        out_specs=pl.BlockSpec((QB, ATT_HEADS * HEAD_DIM), lambda b, n: (b * NQB + n, 0)),
        out_shape=jax.ShapeDtypeStruct((r, ATT_HEADS * HEAD_DIM), BF16),
        compiler_params=_params("parallel", "parallel"),
        name="window_attention",
    )(sink, p, p, p, p, p, p, p, p, p, cos, sin, cos, sin, cos, sin)


NCHUNK = ROWS // GLA_CHUNK
NCHUNK_CTX = CTX // GLA_CHUNK
GQ = GLA_HEADS * GLA_DK
GV = GLA_HEADS * GLA_DV


def _log_sigmoid(z):
    return jnp.minimum(z, 0.0) - jnp.log1p(jnp.exp(-jnp.abs(z)))


def _gla_kernel(q_ref, k_ref, a_ref, v_ref, r_ref, w2_ref, b2_ref, g_ref, o_ref,
                of_ref, ob_ref, sf_ref, sb_ref):
    sf_ref[...] = jnp.zeros_like(sf_ref)
    sb_ref[...] = jnp.zeros_like(sb_ref)
    ci = lax.broadcasted_iota(jnp.int32, (GLA_CHUNK, GLA_CHUNK), 0)
    cj = lax.broadcasted_iota(jnp.int32, (GLA_CHUNK, GLA_CHUNK), 1)
    bi = lax.broadcasted_iota(jnp.int32, (GV, GQ), 0)
    bj = lax.broadcasted_iota(jnp.int32, (GV, GQ), 1)
    block_diag = (bi // GLA_DV) == (bj // GLA_DK)

    def chunk(c, d, s_ref, out_ref):
        causal = (ci >= cj) if d == 0 else (ci <= cj)
        rows = pl.ds(pl.multiple_of(c * GLA_CHUNK, GLA_CHUNK), GLA_CHUNK)
        z = _dot(a_ref[rows, :], w2_ref[d]) + b2_ref[d]
        la = _log_sigmoid(z) * (1.0 / GLA_TAU)
        b = jnp.dot(causal.astype(F32), la, preferred_element_type=F32, precision=lax.Precision.HIGHEST)
        tot = b[GLA_CHUNK - 1:GLA_CHUNK] if d == 0 else b[0:1]
        q = q_ref[rows, :].astype(F32) * (GLA_DK ** -0.5)
        k = k_ref[rows, :].astype(F32)
        v = v_ref[rows, :]
        q_dec = (q * jnp.exp(b)).astype(BF16)
        k_inv = (k * jnp.exp(-b)).astype(BF16)
        k_end = (k * jnp.exp(tot - b)).astype(BF16)
        s = s_ref[...]
        o = _dot_nt(q_dec, s.astype(BF16))
        intra = []
        for h in range(GLA_HEADS):
            qs = slice(h * GLA_DK, (h + 1) * GLA_DK)
            att = jnp.where(causal, _dot_nt(q_dec[:, qs], k_inv[:, qs]), 0.0)
            intra.append(_dot(att.astype(BF16), v[:, h * GLA_DV:(h + 1) * GLA_DV]))
        out_ref[rows, :] = o + jnp.concatenate(intra, axis=1)
        ds = _dot_tn(v, k_end)
        s_ref[...] = s * jnp.exp(tot) + jnp.where(block_diag, ds, 0.0)

    def body(i, carry):
        chunk(i, 0, sf_ref, of_ref)
        cb = jnp.where(i < NCHUNK_CTX, NCHUNK_CTX - 1 - i, NCHUNK + NCHUNK_CTX - 1 - i)
        chunk(cb, 1, sb_ref, ob_ref)
        return carry

    lax.fori_loop(0, NCHUNK, body, 0)

    def finish(t, carry):
        rows = pl.ds(pl.multiple_of(t * TM, TM), TM)
        o = of_ref[rows, :] + ob_ref[rows, :]
        parts = []
        for h in range(GLA_HEADS):
            oh = o[:, h * GLA_DV:(h + 1) * GLA_DV]
            parts.append(oh * lax.rsqrt(jnp.mean(oh * oh, axis=-1, keepdims=True) + EPS))
        on = jnp.concatenate(parts, axis=1) * g_ref[...]
        rg = r_ref[rows, :].astype(F32)
        o_ref[rows, :] = (on * (rg * jax.nn.sigmoid(rg))).astype(BF16)
        return carry

    lax.fori_loop(0, ROWS // TM, finish, 0)


def _gla(p, w2p, b2, g, bsz):
    r = p.shape[0]
    blk = lambda c, w: pl.BlockSpec((ROWS, w), lambda b: (b, c // w))
    return pl.pallas_call(
        _gla_kernel,
        grid=(bsz,),
        in_specs=[
            blk(C_QG, GQ), blk(C_KG, GQ), blk(C_AG, AG_W), blk(C_VG, GV), blk(C_RG, GV),
            pl.BlockSpec((2, AG_W, GQ), lambda b: (0, 0, 0)),
            pl.BlockSpec((2, 1, GQ), lambda b: (0, 0, 0)),
            pl.BlockSpec((1, GV), lambda b: (0, 0)),
        ],
        out_specs=pl.BlockSpec((ROWS, GV), lambda b: (b, 0)),
        out_shape=jax.ShapeDtypeStruct((r, GV), BF16),
        scratch_shapes=[
            pltpu.VMEM((ROWS, GV), F32), pltpu.VMEM((ROWS, GV), F32),
            pltpu.VMEM((GV, GQ), F32), pltpu.VMEM((GV, GQ), F32),
        ],
        compiler_params=_params("parallel"),
        name="gla",
    )(p, p, p, p, p, w2p, b2, g)


def _merge_kernel(ya_ref, yb_ref, yc_ref, g0_ref, g1_ref, g2_ref, x_ref, gm_ref, wb_ref, wo_ref, o_ref):
    acc = jax.nn.sigmoid(g0_ref[...].astype(F32)) * _dot(ya_ref[...], wb_ref[0])
    acc += jax.nn.sigmoid(g1_ref[...].astype(F32)) * _dot(yb_ref[...], wb_ref[1])
    acc += jax.nn.sigmoid(g2_ref[...].astype(F32)) * _dot(yc_ref[...], wb_ref[2])
    y = _dot(acc.astype(BF16), wo_ref[...])
    o_ref[...] = x_ref[...] + gm_ref[...] * y


def _merge(ya, yb, yc, p, x, mod, wb, wo, bsz):
    r = x.shape[0]
    br = pl.BlockSpec((TM, BRANCH_W), lambda i: (i, 0))
    gate = lambda k: pl.BlockSpec((TM, D), lambda i: (i, C_GATE // D + k))
    return pl.pallas_call(
        _merge_kernel,
        grid=(r // TM,),
        in_specs=[br, br, br, gate(0), gate(1), gate(2),
                  pl.BlockSpec((TM, D), lambda i: (i, 0)),
                  _mod_spec(2, bsz),
                  pl.BlockSpec((3, BRANCH_W, D), lambda i: (0, 0, 0), pipeline_mode=pl.Buffered(1)),
                  pl.BlockSpec((D, D), lambda i: (0, 0), pipeline_mode=pl.Buffered(1))],
        out_specs=pl.BlockSpec((TM, D), lambda i: (i, 0)),
        out_shape=jax.ShapeDtypeStruct((r, D), F32),
        compiler_params=_params("parallel"),
        name="merge",
    )(ya, yb, yc, p, p, p, x, mod, wb, wo)


ROUTER_W = 128


def _router_kernel(x_ref, g_ref, sh_ref, sc_ref, w_ref, o_ref, h_ref):
    h = _norm_mod(x_ref[...], g_ref[...], sh_ref[...], sc_ref[...])
    h_ref[...] = h
    o_ref[...] = jnp.dot(h, w_ref[...], preferred_element_type=F32, precision=lax.Precision.HIGHEST)


def _router(x, g, mod, w, bsz):
    r = x.shape[0]
    return pl.pallas_call(
        _router_kernel,
        grid=(r // TM,),
        in_specs=[pl.BlockSpec((TM, D), lambda i: (i, 0)),
                  pl.BlockSpec((1, D), lambda i: (0, 0)),
                  _mod_spec(3, bsz), _mod_spec(4, bsz),
                  pl.BlockSpec((D, ROUTER_W), lambda i: (0, 0))],
        out_specs=[pl.BlockSpec((TM, ROUTER_W), lambda i: (i, 0)),
                   pl.BlockSpec((TM, D), lambda i: (i, 0))],
        out_shape=[jax.ShapeDtypeStruct((r, ROUTER_W), F32), jax.ShapeDtypeStruct((r, D), F32)],
        compiler_params=_params("parallel"),
        name="router",
    )(x, g, mod, mod, w)


MOE_T = 896
MOE_NF = D_FF // FFN_TF
MOE_CH = MOE_T // MOE_NF


def _moe_num_tiles(r):
    return (TOP_K * r + N_EXPERTS * (MOE_T - 1)) // MOE_T


def _route(logits, r):
    nt = _moe_num_tiles(r)
    top_v, top_i = lax.top_k(logits, TOP_K)
    top_p = jax.nn.softmax(top_v, axis=-1)
    e_flat = top_i.reshape(-1)
    onehot = (e_flat[:, None] == jnp.arange(N_EXPERTS)[None, :]).astype(jnp.int32)
    csum = jnp.cumsum(onehot, axis=0)
    rank = jnp.sum(onehot * (csum - 1), axis=1)
    counts = csum[-1]
    tiles_e = (counts + MOE_T - 1) // MOE_T
    tile_end = jnp.cumsum(tiles_e)
    n_active = tile_end[-1]
    slot = (tile_end - tiles_e)[e_flat] * MOE_T + rank
    slot_token = jnp.zeros((nt * MOE_T,), jnp.int32).at[slot].set(jnp.arange(TOP_K * r, dtype=jnp.int32) // TOP_K)
    tile_id = jnp.minimum(jnp.arange(nt), n_active - 1)
    tile_expert = jnp.sum((tile_id[:, None] >= tile_end[None, :]).astype(jnp.int32), axis=1)
    return (tile_expert.astype(jnp.int32), n_active.reshape(1).astype(jnp.int32),
            slot_token.reshape(nt, 1, MOE_T), slot.astype(jnp.int32), top_p)


def _moe_kernel(te_ref, na_ref, tok0_ref, tokn_ref, h_hbm, w1_ref, w3_ref, w2_ref, o_ref,
                gbuf, hs_ref, acc_ref, sem):
    i = pl.program_id(0)
    f = pl.program_id(1)
    active = i < na_ref[0]
    slot = i % 2

    def issue(tok_ref, buf, r0, nrows):
        def body(r, carry):
            pltpu.make_async_copy(h_hbm.at[pl.ds(tok_ref[0, r], 1)], gbuf.at[buf, pl.ds(r, 1)], sem.at[buf]).start()
            return carry
        lax.fori_loop(r0, r0 + nrows, body, 0)

    @pl.when((i == 0) & (f == 0))
    def _():
        issue(tok0_ref, 0, 0, MOE_T)

    @pl.when(active & (f == 0))
    def _():
        pltpu.make_async_copy(h_hbm.at[pl.ds(0, MOE_T)], gbuf.at[slot], sem.at[slot]).wait()
        hs_ref[...] = gbuf[slot].astype(BF16)
        acc_ref[...] = jnp.zeros_like(acc_ref)

    @pl.when(i + 1 < na_ref[0])
    def _():
        issue(tokn_ref, 1 - slot, f * MOE_CH, MOE_CH)

    @pl.when(active)
    def _():
        h = hs_ref[...]
        a = _dot(h, w1_ref[...])
        t = a * jax.nn.sigmoid(a) * _dot(h, w3_ref[...])
        acc_ref[...] += _dot(t.astype(BF16), w2_ref[...])

    @pl.when(f == MOE_NF - 1)
    def _():
        o_ref[...] = jnp.where(active, acc_ref[...], 0.0)


def _moe_experts(h, tile_expert, n_active, slot_token, w1, w3, w2):
    r = h.shape[0]
    nt = _moe_num_tiles(r)
    hidden = lambda i, f, te, na: jnp.where(i < na[0], f, MOE_NF - 1)
    grid_spec = pltpu.PrefetchScalarGridSpec(
        num_scalar_prefetch=2,
        grid=(nt, MOE_NF),
        in_specs=[
            pl.BlockSpec((None, 1, MOE_T), lambda i, f, te, na: (0, 0, 0), memory_space=pltpu.SMEM),
            pl.BlockSpec((None, 1, MOE_T), lambda i, f, te, na: (jnp.minimum(i + 1, nt - 1), 0, 0),
                         memory_space=pltpu.SMEM),
            pl.BlockSpec(memory_space=pl.ANY),
            pl.BlockSpec((None, D, FFN_TF), lambda i, f, te, na: (te[i], 0, hidden(i, f, te, na))),
            pl.BlockSpec((None, D, FFN_TF), lambda i, f, te, na: (te[i], 0, hidden(i, f, te, na))),
            pl.BlockSpec((None, FFN_TF, D), lambda i, f, te, na: (te[i], hidden(i, f, te, na), 0)),
        ],
        out_specs=pl.BlockSpec((MOE_T, D), lambda i, f, te, na: (i, 0)),
        scratch_shapes=[pltpu.VMEM((2, MOE_T, D), F32), pltpu.VMEM((MOE_T, D), BF16),
                        pltpu.VMEM((MOE_T, D), F32), pltpu.SemaphoreType.DMA((2,))],
    )
    return pl.pallas_call(
        _moe_kernel,
        grid_spec=grid_spec,
        out_shape=jax.ShapeDtypeStruct((nt * MOE_T, D), F32),
        compiler_params=_params("arbitrary", "arbitrary"),
        name="moe_experts",
    )(tile_expert, n_active, slot_token, slot_token, h, w1, w3, w2)


def _combine_kernel(pos0_ref, posn_ref, y_hbm, x_ref, p_ref, gm_ref, o_ref, ybuf, sem):
    i = pl.program_id(0)
    slot = i % 2

    def issue(pos_ref, buf):
        def body(r, carry):
            for k in range(TOP_K):
                pltpu.make_async_copy(y_hbm.at[pl.ds(pos_ref[0, TOP_K * r + k], 1)],
                                      ybuf.at[buf, pl.ds(k * TM + r, 1)], sem.at[buf]).start()
            return carry
        lax.fori_loop(0, TM, body, 0)

    @pl.when(i == 0)
    def _():
        issue(pos0_ref, 0)

    @pl.when(i + 1 < pl.num_programs(0))
    def _():
        issue(posn_ref, 1 - slot)

    pltpu.make_async_copy(y_hbm.at[pl.ds(0, TOP_K * TM)], ybuf.at[slot], sem.at[slot]).wait()
    p = p_ref[...]
    y = p[:, 0:1] * ybuf[slot, 0:TM, :]
    for k in range(1, TOP_K):
        y += p[:, k:k + 1] * ybuf[slot, k * TM:(k + 1) * TM, :]
    o_ref[...] = x_ref[...] + gm_ref[...] * y


def _moe_combine(y, pos, top_p, x, mod, bsz):
    r = x.shape[0]
    nt = r // TM
    pos = pos.reshape(nt, 1, TOP_K * TM)
    return pl.pallas_call(
        _combine_kernel,
        grid=(nt,),
        in_specs=[
            pl.BlockSpec((None, 1, TOP_K * TM), lambda i: (0, 0, 0), memory_space=pltpu.SMEM),
            pl.BlockSpec((None, 1, TOP_K * TM), lambda i: (jnp.minimum(i + 1, nt - 1), 0, 0),
                         memory_space=pltpu.SMEM),
            pl.BlockSpec(memory_space=pl.ANY),
            pl.BlockSpec((TM, D), lambda i: (i, 0)),
            pl.BlockSpec((TM, TOP_K), lambda i: (i, 0)),
            _mod_spec(5, bsz),
        ],
        out_specs=pl.BlockSpec((TM, D), lambda i: (i, 0)),
        out_shape=jax.ShapeDtypeStruct((r, D), F32),
        scratch_shapes=[pltpu.VMEM((2, TOP_K * TM, D), F32), pltpu.SemaphoreType.DMA((2,))],
        compiler_params=_params("arbitrary"),
        name="moe_combine",
    )(pos, pos, y, x, top_p, mod)


FFN_TILES_PER_BATCH = ROWS // FFN_TM


def _ffn_kernel(x_ref, g_ref, mx_ref, mc_ref, w1_ref, w3_ref, w2_ref, o_ref, h_ref, acc_ref):
    f = pl.program_id(1)
    row = lax.broadcasted_iota(jnp.int32, (FFN_TM, 1), 0) + (pl.program_id(0) % FFN_TILES_PER_BATCH) * FFN_TM
    is_ctx = row < CTX

    def mod(j):
        return jnp.where(is_ctx, mc_ref[:, j * D:(j + 1) * D], mx_ref[:, j * D:(j + 1) * D])

    @pl.when(f == 0)
    def _():
        h_ref[...] = _norm_mod(x_ref[...], g_ref[...], mod(3), mod(4)).astype(BF16)
        acc_ref[...] = jnp.zeros_like(acc_ref)

    h = h_ref[...]
    a = _dot(h, w1_ref[...])
    t = a * jax.nn.sigmoid(a) * _dot(h, w3_ref[...])
    acc_ref[...] += _dot(t.astype(BF16), w2_ref[...])

    @pl.when(f == pl.num_programs(1) - 1)
    def _():
        o_ref[...] = x_ref[...] + mod(5) * acc_ref[...]


def _ffn(x, g, mod, w1, w3, w2, bsz):
    r = x.shape[0]
    row = pl.BlockSpec((FFN_TM, D), lambda i, f: (i, 0))
    return pl.pallas_call(
        _ffn_kernel,
        grid=(r // FFN_TM, D_FF // FFN_TF),
        in_specs=[row,
                  pl.BlockSpec((1, D), lambda i, f: (0, 0)),
                  pl.BlockSpec((None, 1, 6 * D), lambda i, f: (i // FFN_TILES_PER_BATCH, 0, 0)),
                  pl.BlockSpec((None, 1, 6 * D), lambda i, f: (bsz, 0, 0)),
                  pl.BlockSpec((D, FFN_TF), lambda i, f: (0, f)),
                  pl.BlockSpec((D, FFN_TF), lambda i, f: (0, f)),
                  pl.BlockSpec((FFN_TF, D), lambda i, f: (f, 0))],
        out_specs=row,
        out_shape=jax.ShapeDtypeStruct((r, D), F32),
        scratch_shapes=[pltpu.VMEM((FFN_TM, D), BF16), pltpu.VMEM((FFN_TM, D), F32)],
        compiler_params=_params("parallel", "arbitrary"),
        name="ffn",
    )(x, g, mod, mod, w1, w3, w2)


def _final_kernel(x_ref, g_ref, o_ref):
    x = x_ref[...]
    o_ref[...] = x * lax.rsqrt(jnp.mean(x * x, axis=-1, keepdims=True) + EPS) * g_ref[...]


def _final_norm(x, g, bsz):
    lat = SEQ // TM
    return pl.pallas_call(
        _final_kernel,
        grid=(bsz, lat),
        in_specs=[pl.BlockSpec((TM, D), lambda b, j: (b * TILES_PER_BATCH + CTX // TM + j, 0)),
                  pl.BlockSpec((1, D), lambda b, j: (0, 0))],
        out_specs=pl.BlockSpec((TM, D), lambda b, j: (b * lat + j, 0)),
        out_shape=jax.ShapeDtypeStruct((bsz * SEQ, D), F32),
        compiler_params=_params("parallel", "parallel"),
        name="final_norm",
    )(x, g)


def _layout_w_in(w_in):
    depth = w_in.shape[0]
    o_vg = 2816
    o_rg = o_vg + GV
    o_ag = o_rg + GV
    o_gate = o_ag + 2 * GLA_RANK
    z = lambda n: jnp.zeros((depth, D, n), w_in.dtype)
    parts = [w_in[..., :o_vg],
             w_in[..., o_ag:o_gate], z(C_VG - C_AG - 2 * GLA_RANK),
             w_in[..., o_vg:o_ag],
             w_in[..., o_gate:]]
    return jnp.concatenate(parts, axis=-1).astype(BF16)


def _layout_gla_w2(gla_w2):
    depth = gla_w2.shape[0]
    out = jnp.zeros((depth, 2, AG_W, GQ), F32)
    for d in range(2):
        out = out.at[:, d, d * GLA_RANK:(d + 1) * GLA_RANK, :].set(gla_w2[:, d])
    return out.astype(BF16)


def kernel(x, c, ctx, c_ctx, w_mod, b_mod, norm1_g, norm2_g, w_in, conv_w, attn_sink, gla_w2, gla_b,
           gla_norm_g, w_branch, w_out, ffn_w1, ffn_w3, ffn_w2, router_w, moe_w1, moe_w3, moe_w2,
           final_norm_g):
    bsz = x.shape[0]
    depth = w_in.shape[0]
    assert x.shape[1:] == (SEQ, D) and ctx.shape[1:] == (CTX, D)

    nrow = -(-(bsz + 1) // 8) * 8
    cond = jnp.zeros((nrow, D), F32).at[:bsz].set(c).at[bsz].set(c_ctx)
    mods = _modulation(cond, w_mod, b_mod).reshape(depth, nrow, 1, 6 * D)

    xs = jnp.concatenate([ctx, x], axis=1).reshape(bsz * ROWS, D)
    w_in_p = _layout_w_in(w_in)
    w2p = _layout_gla_w2(gla_w2)
    cos, sin = _rope_tables()
    router_p = jnp.pad(router_w, ((0, 0), (0, 0), (0, ROUTER_W - N_EXPERTS)))

    for l in range(depth):
        mod = mods[l]
        p = _proj_in(xs, norm1_g[l].reshape(1, D), mod, w_in_p[l], bsz)
        ya = _conv(p, conv_w[l])
        yb = _attention(p, attn_sink[l], cos, sin, bsz)
        yc = _gla(p, w2p[l], gla_b[l].reshape(2, 1, GQ), gla_norm_g[l].reshape(1, GV), bsz)
        xs = _merge(ya, yb, yc, p, xs, mod, w_branch[l].astype(BF16), w_out[l].astype(BF16), bsz)
        g2 = norm2_g[l].reshape(1, D)
        j = l // 2
        if l % 2 == 0:
            xs = _ffn(xs, g2, mod, ffn_w1[j].astype(BF16), ffn_w3[j].astype(BF16), ffn_w2[j].astype(BF16), bsz)
        else:
            logits, h2 = _router(xs, g2, mod, router_p[j], bsz)
            tile_expert, n_active, slot_token, pos, top_p = _route(logits[:, :N_EXPERTS], xs.shape[0])
            y = _moe_experts(h2, tile_expert, n_active, slot_token, moe_w1[j].astype(BF16),
                             moe_w3[j].astype(BF16), moe_w2[j].astype(BF16))
            xs = _moe_combine(y, pos, top_p, xs, mod, bsz)
    out = _final_norm(xs, final_norm_g.reshape(1, D), bsz)
    return out.reshape(bsz, SEQ, D)
```

```python
import jax
import jax.numpy as jnp
from jax import lax
from jax.experimental import pallas as pl
from jax.experimental.pallas import tpu as pltpu

F32 = jnp.float32
BF16 = jnp.bfloat16

D = 1024
SEQ = 2048
CTX = 256
ROWS = CTX + SEQ
GRID_W = 64
EPS = 1e-6
NEG_INF = -1e30
LOG2E = 1.4426950408889634

BRANCH_W = 512
ATT_HEADS = 8
ATT_KV = 2
ATT_GROUP = ATT_HEADS // ATT_KV
HEAD_DIM = 64
QB = 128
ROPE_BASE = 10000.0
GLA_HEADS = 4
GLA_DK = 64
GLA_DV = 128
GLA_RANK = 16
GLA_TAU = 16.0
GLA_CHUNK = 64
D_FF = 3584
N_EXPERTS = 8
TOP_K = 2

C_CB, C_CC, C_CU = 0, 512, 1024
C_QA, C_KA, C_VA = 1536, 2048, 2176
C_QG, C_KG, C_AG = 2304, 2560, 2816
C_VG, C_RG, C_GATE = 3072, 3584, 4096
DP = 7168
AG_W = 128

TM = 256
TILES_PER_BATCH = ROWS // TM
PROJ_TN = 1024
FFN_TM = 768
FFN_TF = 512
VMEM_LIMIT = 56 * 1024 * 1024


def _params(*sem):
    return pltpu.CompilerParams(dimension_semantics=sem, vmem_limit_bytes=VMEM_LIMIT)


def _dot(a, b):
    return jnp.dot(a, b, preferred_element_type=F32)


def _dot_nt(a, b):
    return lax.dot_general(a, b, (((1,), (1,)), ((), ())), preferred_element_type=F32)


def _dot_tn(a, b):
    return lax.dot_general(a, b, (((0,), (0,)), ((), ())), preferred_element_type=F32)


def _norm_mod(x, g, shift, scale):
    h = x * lax.rsqrt(jnp.mean(x * x, axis=-1, keepdims=True) + EPS) * g
    return h * (1.0 + scale) + shift


def _mod_kernel(s_ref, w_ref, b_ref, o_ref):
    s = s_ref[...]
    s = s * jax.nn.sigmoid(s)
    o_ref[...] = _dot(s.astype(BF16), w_ref[...].astype(BF16)) + b_ref[...]


def _modulation(cond, w_mod, b_mod):
    depth = w_mod.shape[0]
    nrow = cond.shape[0]
    return pl.pallas_call(
        _mod_kernel,
        grid=(depth, 6),
        in_specs=[
            pl.BlockSpec((nrow, D), lambda l, j: (0, 0)),
            pl.BlockSpec((None, D, D), lambda l, j: (l, 0, j)),
            pl.BlockSpec((None, 1, D), lambda l, j: (l, 0, j)),
        ],
        out_specs=pl.BlockSpec((None, nrow, D), lambda l, j: (l, 0, j)),
        out_shape=jax.ShapeDtypeStruct((depth, nrow, 6 * D), F32),
        compiler_params=_params("parallel", "parallel"),
        name="modulation",
    )(cond, w_mod, b_mod.reshape(depth, 1, 6 * D))


def _mod_row(i, bsz):
    return jnp.where(i % TILES_PER_BATCH == 0, bsz, i // TILES_PER_BATCH)


def _mod_spec(j, bsz):
    return pl.BlockSpec((None, 1, D), lambda i: (_mod_row(i, bsz), 0, j))


def _proj_in_kernel(x_ref, g_ref, sh_ref, sc_ref, w_ref, o_ref):
    h = _norm_mod(x_ref[...], g_ref[...], sh_ref[...], sc_ref[...]).astype(BF16)
    for n0 in range(0, DP, PROJ_TN):
        o_ref[:, n0:n0 + PROJ_TN] = _dot(h, w_ref[:, n0:n0 + PROJ_TN]).astype(BF16)


def _proj_in(x, g, mod, w, layer, bsz):
    r = x.shape[0]
    return pl.pallas_call(
        _proj_in_kernel,
        grid=(r // TM,),
        in_specs=[
            pl.BlockSpec((TM, D), lambda i: (i, 0)),
            pl.BlockSpec((1, D), lambda i: (0, 0)),
            _mod_spec(0, bsz),
            _mod_spec(1, bsz),
            pl.BlockSpec((None, D, DP), lambda i: (layer, 0, 0), pipeline_mode=pl.Buffered(1)),
        ],
        out_specs=pl.BlockSpec((TM, DP), lambda i: (i, 0)),
        out_shape=jax.ShapeDtypeStruct((r, DP), BF16),
        compiler_params=_params("parallel"),
        name="proj_in",
    )(x, g, mod, mod, w)


HALO = 16


def _conv_kernel(cb_ref, cc_ref, cu_ref, ccp_ref, cup_ref, ccn_ref, cun_ref, w_ref, o_ref):
    j = pl.program_id(0) % TILES_PER_BATCH
    first = (j == 0) | (j == 1)
    last = (j == 0) | (j == TILES_PER_BATCH - 1)
    z = cc_ref[...].astype(F32) * cu_ref[...].astype(F32)
    zp = ccp_ref[HALO - 1:HALO, :].astype(F32) * cup_ref[HALO - 1:HALO, :].astype(F32)
    zn = ccn_ref[0:1, :].astype(F32) * cun_ref[0:1, :].astype(F32)
    zp = jnp.where(first, 0.0, zp)
    zn = jnp.where(last, 0.0, zn)
    row = lax.broadcasted_iota(jnp.int32, z.shape, 0)
    z_prev = jnp.where(row == 0, zp, pltpu.roll(z, 1, 0))
    z_next = jnp.where(row == TM - 1, zn, pltpu.roll(z, TM - 1, 0))
    w = w_ref[...]
    y = z_prev * w[0:1] + z * w[1:2] + z_next * w[2:3]
    o_ref[...] = (cb_ref[...].astype(F32) * y).astype(BF16)


def _conv(p, conv_w):
    r = p.shape[0]
    hb = TM // HALO
    nh = r // HALO
    col = lambda c: c // BRANCH_W
    tile = lambda c: pl.BlockSpec((TM, BRANCH_W), lambda i: (i, col(c)))
    prev = lambda c: pl.BlockSpec((HALO, BRANCH_W), lambda i: (jnp.maximum(i * hb - 1, 0), col(c)))
    nxt = lambda c: pl.BlockSpec((HALO, BRANCH_W), lambda i: (jnp.minimum((i + 1) * hb, nh - 1), col(c)))
    return pl.pallas_call(
        _conv_kernel,
        grid=(r // TM,),
        in_specs=[tile(C_CB), tile(C_CC), tile(C_CU), prev(C_CC), prev(C_CU), nxt(C_CC), nxt(C_CU),
                  pl.BlockSpec((3, BRANCH_W), lambda i: (0, 0))],
        out_specs=pl.BlockSpec((TM, BRANCH_W), lambda i: (i, 0)),
        out_shape=jax.ShapeDtypeStruct((r, BRANCH_W), BF16),
        compiler_params=_params("parallel"),
        name="short_conv",
    )(p, p, p, p, p, p, p, conv_w)


NQB = ROWS // QB
NCB = CTX // QB
NLK = 3 * QB


def _rope(x, cos, sin):
    w = x.shape[-1]
    lane = lax.broadcasted_iota(jnp.int32, x.shape, 1)
    partner = jnp.where((lane % 32) < 16, pltpu.roll(x, w - 16, 1), pltpu.roll(x, 16, 1))
    return x * cos + partner * sin


def _attn_kernel(sink_ref, q_ref, kp_ref, ko_ref, kn_ref, vp_ref, vo_ref, vn_ref, kc_ref, vc_ref,
                 cq_ref, sq_ref, cp_ref, sp_ref, cn_ref, sn_ref, o_ref):
    n = pl.program_id(1)
    is_ctx = n < NCB
    cq = jnp.tile(cq_ref[...], (1, ATT_HEADS // 2))
    sq = jnp.tile(sq_ref[...], (1, ATT_HEADS // 2))
    qr = (_rope(q_ref[...].astype(F32), cq, sq) * (HEAD_DIM ** -0.5 * LOG2E)).astype(BF16)
    k_p = _rope(kp_ref[...].astype(F32), cp_ref[...], sp_ref[...]).astype(BF16)
    k_o = _rope(ko_ref[...].astype(F32), cq_ref[...], sq_ref[...]).astype(BF16)
    k_n = _rope(kn_ref[...].astype(F32), cn_ref[...], sn_ref[...]).astype(BF16)
    kcat = jnp.concatenate([k_p, k_o, k_n, kc_ref[...]], axis=0)
    vcat = jnp.concatenate([vp_ref[...], vo_ref[...], vn_ref[...], vc_ref[...]], axis=0)
    i = lax.broadcasted_iota(jnp.int32, (ATT_GROUP * QB, QB), 0) % QB
    j = lax.broadcasted_iota(jnp.int32, (ATT_GROUP * QB, QB), 1)
    far = 4 * QB
    pen_p = jnp.where(n > NCB, 0, far)
    pen_n = jnp.where((n >= NCB) & (n < NQB - 1), 0, far)
    mask_p = j >= i + pen_p
    mask_n = j + pen_n <= i
    outs = [None] * ATT_HEADS
    for kv in range(ATT_KV):
        heads = range(kv * ATT_GROUP, (kv + 1) * ATT_GROUP)
        qg = jnp.concatenate([qr[:, h * HEAD_DIM:(h + 1) * HEAD_DIM] for h in heads], axis=0)
        sink = jnp.concatenate([jnp.full((QB, 1), sink_ref[h] * LOG2E, F32) for h in heads], axis=0)
        s = _dot_nt(qg, kcat[:, kv * HEAD_DIM:(kv + 1) * HEAD_DIM])
        tiles = [jnp.where(mask_p, s[:, :QB], NEG_INF),
                 jnp.where(is_ctx, NEG_INF, s[:, QB:2 * QB]),
                 jnp.where(mask_n, s[:, 2 * QB:NLK], NEG_INF)]
        tiles += [s[:, c:c + QB] for c in range(NLK, NLK + CTX, QB)]
        m = tiles[0]
        for t in tiles[1:]:
            m = jnp.maximum(m, t)
        m = jnp.maximum(jnp.max(m, axis=-1, keepdims=True), sink)
        tiles = [jnp.exp2(t - m) for t in tiles]
        acc = tiles[0]
        for t in tiles[1:]:
            acc = acc + t
        den = jnp.sum(acc, axis=-1, keepdims=True) + jnp.exp2(sink - m)
        p = jnp.concatenate(tiles, axis=1).astype(BF16)
        o = _dot(p, vcat[:, kv * HEAD_DIM:(kv + 1) * HEAD_DIM]) / den
        for g, h in enumerate(heads):
            outs[h] = o[g * QB:(g + 1) * QB]
    o_ref[...] = jnp.concatenate(outs, axis=1).astype(BF16)


def _rope_tables():
    half = HEAD_DIM // 2
    t = jnp.arange(SEQ)
    row = (t // GRID_W).astype(F32)
    col = (t % GRID_W).astype(F32)
    inv_freq = ROPE_BASE ** (-jnp.arange(0, half, 2, dtype=F32) / half)
    ar = row[:, None] * inv_freq
    ac = col[:, None] * inv_freq
    cos = jnp.concatenate([jnp.cos(ar), jnp.cos(ar), jnp.cos(ac), jnp.cos(ac)], axis=1)
    sin = jnp.concatenate([-jnp.sin(ar), jnp.sin(ar), -jnp.sin(ac), jnp.sin(ac)], axis=1)
    cos = jnp.concatenate([cos, jnp.ones((QB, HEAD_DIM), F32)], axis=0)
    sin = jnp.concatenate([sin, jnp.zeros((QB, HEAD_DIM), F32)], axis=0)
    return jnp.tile(cos, (1, 2)), jnp.tile(sin, (1, 2))


def _attention(p, sink, cos, sin, bsz):
    r = p.shape[0]
    nlb = SEQ // QB
    lo, hi = NCB, NQB - 1

    def kblk(shift, c):
        return pl.BlockSpec((QB, 2 * HEAD_DIM),
                            lambda b, n: (b * NQB + jnp.clip(n + shift, lo, hi), c // (2 * HEAD_DIM)))

    def tblk(shift):
        def idx(b, n):
            pos = jnp.clip(n + shift, lo, hi) - NCB
            if shift == 0:
                pos = jnp.where(n < NCB, nlb, pos)
            return (pos, 0)
        return pl.BlockSpec((QB, 2 * HEAD_DIM), idx)

    cblk = lambda c: pl.BlockSpec((CTX, 2 * HEAD_DIM), lambda b, n: (b * (ROWS // CTX), c // (2 * HEAD_DIM)))
    return pl.pallas_call(
        _attn_kernel,
        grid=(bsz, NQB),
        in_specs=[
            pl.BlockSpec(memory_space=pltpu.SMEM),
            pl.BlockSpec((QB, ATT_HEADS * HEAD_DIM), lambda b, n: (b * NQB + n, C_QA // (ATT_HEADS * HEAD_DIM))),
            kblk(-1, C_KA), kblk(0, C_KA), kblk(1, C_KA),
            kblk(-1, C_VA), kblk(0, C_VA), kblk(1, C_VA),
            cblk(C_KA), cblk(C_VA),
            tblk(0), tblk(0), tblk(-1), tblk(-1), tblk(1), tblk(1),
        ],
        out_specs=pl.BlockSpec((QB, ATT_HEADS * HEAD_DIM), lambda b, n: (b * NQB + n, 0)),
        out_shape=jax.ShapeDtypeStruct((r, ATT_HEADS * HEAD_DIM), BF16),
        compiler_params=_params("parallel", "parallel"),
        name="window_attention",
    )(sink, p, p, p, p, p, p, p, p, p, cos, sin, cos, sin, cos, sin)


NCHUNK = ROWS // GLA_CHUNK
NCHUNK_CTX = CTX // GLA_CHUNK
GQ = GLA_HEADS * GLA_DK
GV = GLA_HEADS * GLA_DV
GLA_BLK = 256


def _log_sigmoid(z):
    return jnp.minimum(z, 0.0) - jnp.log(1.0 + jnp.exp(-jnp.abs(z)))


def _gla_kernel(q_ref, k_ref, a_ref, v_ref, r_ref, w2_ref, b2_ref, g_ref, o_ref,
                of_ref, ob_ref, qd_ref, ke_ref, dec_ref, sf_ref, sb_ref):
    ri = lax.broadcasted_iota(jnp.int32, (GLA_BLK, GLA_BLK), 0)
    rj = lax.broadcasted_iota(jnp.int32, (GLA_BLK, GLA_BLK), 1)
    same_chunk = (ri // GLA_CHUNK) == (rj // GLA_CHUNK)
    causal = (same_chunk & (ri >= rj), same_chunk & (ri <= rj))
    lane_head = lax.broadcasted_iota(jnp.int32, (GLA_BLK, GQ), 1) // GLA_DK
    outs = (of_ref, ob_ref)

    def precompute(blk, carry):
        rows = pl.ds(pl.multiple_of(blk * GLA_BLK, GLA_BLK), GLA_BLK)
        a = a_ref[rows, :]
        q = q_ref[rows, :].astype(F32) * (GLA_DK ** -0.5)
        k = k_ref[rows, :].astype(F32)
        v = v_ref[rows, :]
        for d in range(2):
            la = _log_sigmoid(_dot(a, w2_ref[d]) + b2_ref[d]) * (1.0 / GLA_TAU)
            sel = jnp.concatenate([causal[d], same_chunk], axis=0).astype(BF16)
            hi = la.astype(BF16)
            r1 = la - hi.astype(F32)
            mid = r1.astype(BF16)
            lo = (r1 - mid.astype(F32)).astype(BF16)
            sums = _dot(sel, jnp.concatenate([hi, mid, lo], axis=1))
            sums = sums[:, :GQ] + sums[:, GQ:2 * GQ] + sums[:, 2 * GQ:]
            b, tot = sums[:GLA_BLK], sums[GLA_BLK:]
            q_dec = q * jnp.exp(b)
            k_inv = (k * jnp.exp(-b)).astype(BF16)
            qd_ref[d, rows, :] = q_dec.astype(BF16)
            ke_ref[d, rows, :] = (k * jnp.exp(tot - b)).astype(BF16)
            decay = jnp.exp(tot)
            for j in range(GLA_BLK // GLA_CHUNK):
                dec_ref[d, pl.ds(blk * (GLA_BLK // GLA_CHUNK) + j, 1), :] = decay[j * GLA_CHUNK:j * GLA_CHUNK + 1]
            intra = []
            for h in range(GLA_HEADS):
                qh = jnp.where(lane_head == h, q_dec, 0.0).astype(BF16)
                att = jnp.where(causal[d], _dot_nt(qh, k_inv), 0.0)
                intra.append(_dot(att.astype(BF16), v[:, h * GLA_DV:(h + 1) * GLA_DV]))
            outs[d][rows, :] = jnp.concatenate(intra, axis=1)
        return carry

    lax.fori_loop(0, ROWS // GLA_BLK, precompute, 0, unroll=3)

    sf_ref[...] = jnp.zeros_like(sf_ref)
    sb_ref[...] = jnp.zeros_like(sb_ref)
    bi = lax.broadcasted_iota(jnp.int32, (GV, GQ), 0)
    bj = lax.broadcasted_iota(jnp.int32, (GV, GQ), 1)
    block_diag = (bi // GLA_DV) == (bj // GLA_DK)

    def scan(i, carry):
        order = (i, jnp.where(i < NCHUNK_CTX, NCHUNK_CTX - 1 - i, NCHUNK + NCHUNK_CTX - 1 - i))
        for d, s_ref in enumerate((sf_ref, sb_ref)):
            c = order[d]
            rows = pl.ds(pl.multiple_of(c * GLA_CHUNK, GLA_CHUNK), GLA_CHUNK)
            s = s_ref[...]
            outs[d][rows, :] += _dot_nt(qd_ref[d, rows, :], s.astype(BF16))
            ds = _dot_tn(v_ref[rows, :], ke_ref[d, rows, :])
            s_ref[...] = s * dec_ref[d, pl.ds(c, 1), :] + jnp.where(block_diag, ds, 0.0)
        return carry

    lax.fori_loop(0, NCHUNK, scan, 0, unroll=2)

    def finish(t, carry):
        rows = pl.ds(pl.multiple_of(t * TM, TM), TM)
        o = of_ref[rows, :] + ob_ref[rows, :]
        parts = []
        for h in range(GLA_HEADS):
            oh = o[:, h * GLA_DV:(h + 1) * GLA_DV]
            parts.append(oh * lax.rsqrt(jnp.mean(oh * oh, axis=-1, keepdims=True) + EPS))
        on = jnp.concatenate(parts, axis=1) * g_ref[...]
        rg = r_ref[rows, :].astype(F32)
        o_ref[rows, :] = (on * (rg * jax.nn.sigmoid(rg))).astype(BF16)
        return carry

    lax.fori_loop(0, ROWS // TM, finish, 0)


def _gla(p, w2p, b2, g, bsz):
    r = p.shape[0]
    blk = lambda c, w: pl.BlockSpec((ROWS, w), lambda b: (b, c // w))
    return pl.pallas_call(
        _gla_kernel,
        grid=(bsz,),
        in_specs=[
            blk(C_QG, GQ), blk(C_KG, GQ), blk(C_AG, AG_W), blk(C_VG, GV), blk(C_RG, GV),
            pl.BlockSpec((2, AG_W, GQ), lambda b: (0, 0, 0)),
            pl.BlockSpec((2, 1, GQ), lambda b: (0, 0, 0)),
            pl.BlockSpec((1, GV), lambda b: (0, 0)),
        ],
        out_specs=pl.BlockSpec((ROWS, GV), lambda b: (b, 0)),
        out_shape=jax.ShapeDtypeStruct((r, GV), BF16),
        scratch_shapes=[
            pltpu.VMEM((ROWS, GV), F32), pltpu.VMEM((ROWS, GV), F32),
            pltpu.VMEM((2, ROWS, GQ), BF16), pltpu.VMEM((2, ROWS, GQ), BF16),
            pltpu.VMEM((2, NCHUNK, GQ), F32),
            pltpu.VMEM((GV, GQ), F32), pltpu.VMEM((GV, GQ), F32),
        ],
        compiler_params=_params("parallel"),
        name="gla",
    )(p, p, p, p, p, w2p, b2, g)


def _merge_kernel(ya_ref, yb_ref, yc_ref, g0_ref, g1_ref, g2_ref, x_ref, gm_ref, wb_ref, wo_ref, o_ref):
    acc = jax.nn.sigmoid(g0_ref[...].astype(F32)) * _dot(ya_ref[...], wb_ref[0])
    acc += jax.nn.sigmoid(g1_ref[...].astype(F32)) * _dot(yb_ref[...], wb_ref[1])
    acc += jax.nn.sigmoid(g2_ref[...].astype(F32)) * _dot(yc_ref[...], wb_ref[2])
    y = _dot(acc.astype(BF16), wo_ref[...])
    o_ref[...] = x_ref[...] + gm_ref[...] * y


def _merge(ya, yb, yc, p, x, mod, wb, wo, layer, bsz):
    r = x.shape[0]
    br = pl.BlockSpec((TM, BRANCH_W), lambda i: (i, 0))
    gate = lambda k: pl.BlockSpec((TM, D), lambda i: (i, C_GATE // D + k))
    return pl.pallas_call(
        _merge_kernel,
        grid=(r // TM,),
        in_specs=[br, br, br, gate(0), gate(1), gate(2),
                  pl.BlockSpec((TM, D), lambda i: (i, 0)),
                  _mod_spec(2, bsz),
                  pl.BlockSpec((None, 3, BRANCH_W, D), lambda i: (layer, 0, 0, 0), pipeline_mode=pl.Buffered(1)),
                  pl.BlockSpec((None, D, D), lambda i: (layer, 0, 0), pipeline_mode=pl.Buffered(1))],
        out_specs=pl.BlockSpec((TM, D), lambda i: (i, 0)),
        out_shape=jax.ShapeDtypeStruct((r, D), F32),
        compiler_params=_params("parallel"),
        name="merge",
    )(ya, yb, yc, p, p, p, x, mod, wb, wo)


ROUTER_W = 128


def _router_kernel(x_ref, g_ref, sh_ref, sc_ref, w_ref, o_ref, h_ref):
    h = _norm_mod(x_ref[...], g_ref[...], sh_ref[...], sc_ref[...])
    h_ref[...] = h
    o_ref[...] = jnp.dot(h, w_ref[...], preferred_element_type=F32, precision=lax.Precision.HIGHEST)


def _router(x, g, mod, w, bsz):
    r = x.shape[0]
    return pl.pallas_call(
        _router_kernel,
        grid=(r // TM,),
        in_specs=[pl.BlockSpec((TM, D), lambda i: (i, 0)),
                  pl.BlockSpec((1, D), lambda i: (0, 0)),
                  _mod_spec(3, bsz), _mod_spec(4, bsz),
                  pl.BlockSpec((D, ROUTER_W), lambda i: (0, 0))],
        out_specs=[pl.BlockSpec((TM, ROUTER_W), lambda i: (i, 0)),
                   pl.BlockSpec((TM, D), lambda i: (i, 0))],
        out_shape=[jax.ShapeDtypeStruct((r, ROUTER_W), F32), jax.ShapeDtypeStruct((r, D), F32)],
        compiler_params=_params("parallel"),
        name="router",
    )(x, g, mod, mod, w)


MOE_T = 896
MOE_NF = D_FF // FFN_TF
MOE_CH = MOE_T // MOE_NF


def _moe_num_tiles(r):
    return (TOP_K * r + N_EXPERTS * (MOE_T - 1)) // MOE_T


def _route(logits, r):
    nt = _moe_num_tiles(r)
    top_v, top_i = lax.top_k(logits, TOP_K)
    top_p = jax.nn.softmax(top_v, axis=-1)
    e_flat = top_i.reshape(-1)
    onehot = (e_flat[:, None] == jnp.arange(N_EXPERTS)[None, :]).astype(jnp.int32)
    csum = jnp.cumsum(onehot, axis=0)
    rank = jnp.sum(onehot * (csum - 1), axis=1)
    counts = csum[-1]
    tiles_e = (counts + MOE_T - 1) // MOE_T
    tile_end = jnp.cumsum(tiles_e)
    n_active = tile_end[-1]
    slot = (tile_end - tiles_e)[e_flat] * MOE_T + rank
    slot_token = jnp.zeros((nt * MOE_T,), jnp.int32).at[slot].set(
        jnp.arange(TOP_K * r, dtype=jnp.int32) // TOP_K, unique_indices=True, mode="promise_in_bounds")
    tile_id = jnp.minimum(jnp.arange(nt), n_active - 1)
    tile_expert = jnp.sum((tile_id[:, None] >= tile_end[None, :]).astype(jnp.int32), axis=1)
    return (tile_expert.astype(jnp.int32), n_active.reshape(1).astype(jnp.int32),
            slot_token.reshape(nt, 1, MOE_T), slot.astype(jnp.int32), top_p)


def _moe_kernel(te_ref, na_ref, tok0_ref, tokn_ref, h_hbm, w1_ref, w3_ref, w2_ref, o_ref,
                gbuf, hs_ref, acc_ref, sem):
    i = pl.program_id(0)
    f = pl.program_id(1)
    active = i < na_ref[0]
    slot = i % 2

    def issue(tok_ref, buf, r0, nrows):
        def body(r, carry):
            pltpu.make_async_copy(h_hbm.at[pl.ds(tok_ref[0, r], 1)], gbuf.at[buf, pl.ds(r, 1)], sem.at[buf]).start()
            return carry
        lax.fori_loop(r0, r0 + nrows, body, 0)

    @pl.when((i == 0) & (f == 0))
    def _():
        issue(tok0_ref, 0, 0, MOE_T)

    @pl.when(active & (f == 0))
    def _():
        pltpu.make_async_copy(h_hbm.at[pl.ds(0, MOE_T)], gbuf.at[slot], sem.at[slot]).wait()
        hs_ref[...] = gbuf[slot].astype(BF16)
        acc_ref[...] = jnp.zeros_like(acc_ref)

    has_next = i + 1 < na_ref[0]

    def step(prefetch):
        if prefetch:
            for r in range(MOE_CH):
                row = f * MOE_CH + r
                pltpu.make_async_copy(h_hbm.at[pl.ds(tokn_ref[0, row], 1)], gbuf.at[1 - slot, pl.ds(row, 1)],
                                      sem.at[1 - slot]).start()
        h = hs_ref[...]
        a = _dot(h, w1_ref[...].astype(BF16))
        t = a * jax.nn.sigmoid(a) * _dot(h, w3_ref[...].astype(BF16))
        acc_ref[...] += _dot(t.astype(BF16), w2_ref[...].astype(BF16))

    @pl.when(has_next)
    def _():
        step(True)

    @pl.when(active & jnp.logical_not(has_next))
    def _():
        step(False)

    @pl.when(f == MOE_NF - 1)
    def _():
        o_ref[...] = jnp.where(active, acc_ref[...], 0.0)


def _moe_experts(h, tile_expert, n_active, slot_token, w1, w3, w2, layer):
    r = h.shape[0]
    nt = _moe_num_tiles(r)
    hidden = lambda i, f, te, na: jnp.where(i < na[0], f, MOE_NF - 1)
    grid_spec = pltpu.PrefetchScalarGridSpec(
        num_scalar_prefetch=2,
        grid=(nt, MOE_NF),
        in_specs=[
            pl.BlockSpec((None, 1, MOE_T), lambda i, f, te, na: (0, 0, 0), memory_space=pltpu.SMEM),
            pl.BlockSpec((None, 1, MOE_T), lambda i, f, te, na: (jnp.minimum(i + 1, nt - 1), 0, 0),
                         memory_space=pltpu.SMEM),
            pl.BlockSpec(memory_space=pl.ANY),
            pl.BlockSpec((None, None, D, FFN_TF), lambda i, f, te, na: (layer, te[i], 0, hidden(i, f, te, na))),
            pl.BlockSpec((None, None, D, FFN_TF), lambda i, f, te, na: (layer, te[i], 0, hidden(i, f, te, na))),
            pl.BlockSpec((None, None, FFN_TF, D), lambda i, f, te, na: (layer, te[i], hidden(i, f, te, na), 0)),
        ],
        out_specs=pl.BlockSpec((MOE_T, D), lambda i, f, te, na: (i, 0)),
        scratch_shapes=[pltpu.VMEM((2, MOE_T, D), F32), pltpu.VMEM((MOE_T, D), BF16),
                        pltpu.VMEM((MOE_T, D), F32), pltpu.SemaphoreType.DMA((2,))],
    )
    return pl.pallas_call(
        _moe_kernel,
        grid_spec=grid_spec,
        out_shape=jax.ShapeDtypeStruct((nt * MOE_T, D), F32),
        compiler_params=_params("arbitrary", "arbitrary"),
        name="moe_experts",
    )(tile_expert, n_active, slot_token, slot_token, h, w1, w3, w2)


def _combine_kernel(pos0_ref, posn_ref, y_hbm, x_ref, p_ref, gm_ref, o_ref, ybuf, sem):
    i = pl.program_id(0)
    slot = i % 2

    def start(pos_ref, buf, r):
        for k in range(TOP_K):
            pltpu.make_async_copy(y_hbm.at[pl.ds(pos_ref[0, TOP_K * r + k], 1)],
                                  ybuf.at[buf, pl.ds(k * TM + r, 1)], sem.at[buf]).start()

    @pl.when(i == 0)
    def _():
        def body(r, carry):
            start(pos0_ref, 0, r)
            return carry
        lax.fori_loop(0, TM, body, 0)

    @pl.when(i + 1 < pl.num_programs(0))
    def _():
        for r in range(TM):
            start(posn_ref, 1 - slot, r)

    pltpu.make_async_copy(y_hbm.at[pl.ds(0, TOP_K * TM)], ybuf.at[slot], sem.at[slot]).wait()
    p = p_ref[...]
    y = p[:, 0:1] * ybuf[slot, 0:TM, :]
    for k in range(1, TOP_K):
        y += p[:, k:k + 1] * ybuf[slot, k * TM:(k + 1) * TM, :]
    o_ref[...] = x_ref[...] + gm_ref[...] * y


def _moe_combine(y, pos, top_p, x, mod, bsz):
    r = x.shape[0]
    nt = r // TM
    pos = pos.reshape(nt, 1, TOP_K * TM)
    return pl.pallas_call(
        _combine_kernel,
        grid=(nt,),
        in_specs=[
            pl.BlockSpec((None, 1, TOP_K * TM), lambda i: (0, 0, 0), memory_space=pltpu.SMEM),
            pl.BlockSpec((None, 1, TOP_K * TM), lambda i: (jnp.minimum(i + 1, nt - 1), 0, 0),
                         memory_space=pltpu.SMEM),
            pl.BlockSpec(memory_space=pl.ANY),
            pl.BlockSpec((TM, D), lambda i: (i, 0)),
            pl.BlockSpec((TM, TOP_K), lambda i: (i, 0)),
            _mod_spec(5, bsz),
        ],
        out_specs=pl.BlockSpec((TM, D), lambda i: (i, 0)),
        out_shape=jax.ShapeDtypeStruct((r, D), F32),
        scratch_shapes=[pltpu.VMEM((2, TOP_K * TM, D), F32), pltpu.SemaphoreType.DMA((2,))],
        compiler_params=_params("arbitrary"),
        name="moe_combine",
    )(pos, pos, y, x, top_p, mod)


FFN_TILES_PER_BATCH = ROWS // FFN_TM


def _ffn_kernel(x_ref, g_ref, mx_ref, mc_ref, w1_ref, w3_ref, w2_ref, o_ref, h_ref, acc_ref):
    f = pl.program_id(1)
    row = lax.broadcasted_iota(jnp.int32, (FFN_TM, 1), 0) + (pl.program_id(0) % FFN_TILES_PER_BATCH) * FFN_TM
    is_ctx = row < CTX

    def mod(j):
        return jnp.where(is_ctx, mc_ref[:, j * D:(j + 1) * D], mx_ref[:, j * D:(j + 1) * D])

    @pl.when(f == 0)
    def _():
        h_ref[...] = _norm_mod(x_ref[...], g_ref[...], mod(3), mod(4)).astype(BF16)
        acc_ref[...] = jnp.zeros_like(acc_ref)

    h = h_ref[...]
    a = _dot(h, w1_ref[...])
    t = a * jax.nn.sigmoid(a) * _dot(h, w3_ref[...])
    acc_ref[...] += _dot(t.astype(BF16), w2_ref[...])

    @pl.when(f == pl.num_programs(1) - 1)
    def _():
        o_ref[...] = x_ref[...] + mod(5) * acc_ref[...]


def _ffn(x, g, mod, w1, w3, w2, layer, bsz):
    r = x.shape[0]
    row = pl.BlockSpec((FFN_TM, D), lambda i, f: (i, 0))
    return pl.pallas_call(
        _ffn_kernel,
        grid=(r // FFN_TM, D_FF // FFN_TF),
        in_specs=[row,
                  pl.BlockSpec((1, D), lambda i, f: (0, 0)),
                  pl.BlockSpec((None, 1, 6 * D), lambda i, f: (i // FFN_TILES_PER_BATCH, 0, 0)),
                  pl.BlockSpec((None, 1, 6 * D), lambda i, f: (bsz, 0, 0)),
                  pl.BlockSpec((None, D, FFN_TF), lambda i, f: (layer, 0, f)),
                  pl.BlockSpec((None, D, FFN_TF), lambda i, f: (layer, 0, f)),
                  pl.BlockSpec((None, FFN_TF, D), lambda i, f: (layer, f, 0))],
        out_specs=row,
        out_shape=jax.ShapeDtypeStruct((r, D), F32),
        scratch_shapes=[pltpu.VMEM((FFN_TM, D), BF16), pltpu.VMEM((FFN_TM, D), F32)],
        compiler_params=_params("parallel", "arbitrary"),
        name="ffn",
    )(x, g, mod, mod, w1, w3, w2)


def _final_kernel(x_ref, g_ref, o_ref):
    x = x_ref[...]
    o_ref[...] = x * lax.rsqrt(jnp.mean(x * x, axis=-1, keepdims=True) + EPS) * g_ref[...]


def _final_norm(x, g, bsz):
    lat = SEQ // TM
    return pl.pallas_call(
        _final_kernel,
        grid=(bsz, lat),
        in_specs=[pl.BlockSpec((TM, D), lambda b, j: (b * TILES_PER_BATCH + CTX // TM + j, 0)),
                  pl.BlockSpec((1, D), lambda b, j: (0, 0))],
        out_specs=pl.BlockSpec((TM, D), lambda b, j: (b * lat + j, 0)),
        out_shape=jax.ShapeDtypeStruct((bsz * SEQ, D), F32),
        compiler_params=_params("parallel", "parallel"),
        name="final_norm",
    )(x, g)


D_IN = 6944
SRC_VG = 2816
SRC_AG = SRC_VG + 2 * GV
SRC_GATE = SRC_AG + 2 * GLA_RANK
LAYOUT_TM = 256


def _w_in_layout_kernel(w_ref, o_ref):
    o_ref[:, :C_AG] = w_ref[:, :SRC_VG].astype(BF16)
    o_ref[:, C_AG:C_AG + AG_W] = jnp.zeros((LAYOUT_TM, AG_W), BF16)
    o_ref[:, C_AG:C_AG + 2 * GLA_RANK] = w_ref[:, SRC_AG:SRC_GATE].astype(BF16)
    o_ref[:, C_AG + AG_W:C_VG] = jnp.zeros((LAYOUT_TM, C_VG - C_AG - AG_W), BF16)
    o_ref[:, C_VG:C_GATE] = w_ref[:, SRC_VG:SRC_AG].astype(BF16)
    o_ref[:, C_GATE:] = w_ref[:, SRC_GATE:].astype(BF16)


def _layout_w_in(w_in):
    depth = w_in.shape[0]
    assert w_in.shape[1:] == (D, D_IN)
    return pl.pallas_call(
        _w_in_layout_kernel,
        grid=(depth, D // LAYOUT_TM),
        in_specs=[pl.BlockSpec((None, LAYOUT_TM, D_IN), lambda l, i: (l, i, 0))],
        out_specs=pl.BlockSpec((None, LAYOUT_TM, DP), lambda l, i: (l, i, 0)),
        out_shape=jax.ShapeDtypeStruct((depth, D, DP), BF16),
        compiler_params=_params("parallel", "parallel"),
        name="w_in_layout",
    )(w_in)


def _layout_gla_w2(gla_w2):
    depth = gla_w2.shape[0]
    out = jnp.zeros((depth, 2, AG_W, GQ), F32)
    for d in range(2):
        out = out.at[:, d, d * GLA_RANK:(d + 1) * GLA_RANK, :].set(gla_w2[:, d])
    return out.astype(BF16)


def kernel(x, c, ctx, c_ctx, w_mod, b_mod, norm1_g, norm2_g, w_in, conv_w, attn_sink, gla_w2, gla_b,
           gla_norm_g, w_branch, w_out, ffn_w1, ffn_w3, ffn_w2, router_w, moe_w1, moe_w3, moe_w2,
           final_norm_g):
    bsz = x.shape[0]
    depth = w_in.shape[0]
    assert x.shape[1:] == (SEQ, D) and ctx.shape[1:] == (CTX, D)

    nrow = -(-(bsz + 1) // 8) * 8
    cond = jnp.zeros((nrow, D), F32).at[:bsz].set(c).at[bsz].set(c_ctx)
    mods = _modulation(cond, w_mod, b_mod).reshape(depth, nrow, 1, 6 * D)

    xs = jnp.concatenate([ctx, x], axis=1).reshape(bsz * ROWS, D)
    w_in_p = _layout_w_in(w_in)
    w2p = _layout_gla_w2(gla_w2)
    cos, sin = _rope_tables()
    router_p = jnp.pad(router_w, ((0, 0), (0, 0), (0, ROUTER_W - N_EXPERTS)))

    wb, wo = w_branch.astype(BF16), w_out.astype(BF16)
    f1, f3, f2 = ffn_w1.astype(BF16), ffn_w3.astype(BF16), ffn_w2.astype(BF16)

    for l in range(depth):
        mod = mods[l]
        p = _proj_in(xs, norm1_g[l].reshape(1, D), mod, w_in_p, l, bsz)
        ya = _conv(p, conv_w[l])
        yb = _attention(p, attn_sink[l], cos, sin, bsz)
        yc = _gla(p, w2p[l], gla_b[l].reshape(2, 1, GQ), gla_norm_g[l].reshape(1, GV), bsz)
        xs = _merge(ya, yb, yc, p, xs, mod, wb, wo, l, bsz)
        g2 = norm2_g[l].reshape(1, D)
        j = l // 2
        if l % 2 == 0:
            xs = _ffn(xs, g2, mod, f1, f3, f2, j, bsz)
        else:
            logits, h2 = _router(xs, g2, mod, router_p[j], bsz)
            tile_expert, n_active, slot_token, pos, top_p = _route(logits[:, :N_EXPERTS], xs.shape[0])
            y = _moe_experts(h2, tile_expert, n_active, slot_token, moe_w1, moe_w3, moe_w2, j)
            xs = _moe_combine(y, pos, top_p, xs, mod, bsz)
    out = _final_norm(xs, final_norm_g.reshape(1, D), bsz)
    return out.reshape(bsz, SEQ, D)
```

```python
import jax
import jax.numpy as jnp
from jax import lax
from jax.experimental import pallas as pl
from jax.experimental.pallas import tpu as pltpu

F32 = jnp.float32
BF16 = jnp.bfloat16

D = 1024
SEQ = 2048
CTX = 256
ROWS = CTX + SEQ
GRID_W = 64
EPS = 1e-6
NEG_INF = -1e30
LOG2E = 1.4426950408889634

BRANCH_W = 512
ATT_HEADS = 8
ATT_KV = 2
ATT_GROUP = ATT_HEADS // ATT_KV
HEAD_DIM = 64
QB = 128
ROPE_BASE = 10000.0
GLA_HEADS = 4
GLA_DK = 64
GLA_DV = 128
GLA_RANK = 16
GLA_TAU = 16.0
GLA_CHUNK = 64
D_FF = 3584
N_EXPERTS = 8
TOP_K = 2

C_CB, C_CC, C_CU = 0, 512, 1024
C_QA, C_KA, C_VA = 1536, 2048, 2176
C_QG, C_KG, C_AG = 2304, 2560, 2816
C_VG, C_RG, C_GATE = 3072, 3584, 4096
DP = 7168
AG_W = 128

TM = 256
TILES_PER_BATCH = ROWS // TM
PROJ_TN = 1024
FFN_TM = 768
FFN_TF = 896
VMEM_LIMIT = 56 * 1024 * 1024


def _params(*sem):
    return pltpu.CompilerParams(dimension_semantics=sem, vmem_limit_bytes=VMEM_LIMIT)


def _dot(a, b):
    return jnp.dot(a, b, preferred_element_type=F32)


def _dot_nt(a, b):
    return lax.dot_general(a, b, (((1,), (1,)), ((), ())), preferred_element_type=F32)


def _dot_tn(a, b):
    return lax.dot_general(a, b, (((0,), (0,)), ((), ())), preferred_element_type=F32)


def _norm_mod(x, g, shift, scale):
    h = x * lax.rsqrt(jnp.mean(x * x, axis=-1, keepdims=True) + EPS) * g
    return h * (1.0 + scale) + shift


def _mod_kernel(s_ref, w_ref, b_ref, o_ref):
    s = s_ref[...]
    s = s * jax.nn.sigmoid(s)
    o_ref[...] = _dot(s.astype(BF16), w_ref[...].astype(BF16)) + b_ref[...]


def _modulation(cond, w_mod, b_mod):
    depth = w_mod.shape[0]
    nrow = cond.shape[0]
    return pl.pallas_call(
        _mod_kernel,
        grid=(depth, 6),
        in_specs=[
            pl.BlockSpec((nrow, D), lambda l, j: (0, 0)),
            pl.BlockSpec((None, D, D), lambda l, j: (l, 0, j)),
            pl.BlockSpec((None, 1, D), lambda l, j: (l, 0, j)),
        ],
        out_specs=pl.BlockSpec((None, nrow, D), lambda l, j: (l, 0, j)),
        out_shape=jax.ShapeDtypeStruct((depth, nrow, 6 * D), F32),
        compiler_params=_params("parallel", "parallel"),
        name="modulation",
    )(cond, w_mod, b_mod.reshape(depth, 1, 6 * D))


def _mod_row(i, bsz):
    return jnp.where(i % TILES_PER_BATCH == 0, bsz, i // TILES_PER_BATCH)


def _mod_spec(j, bsz):
    return pl.BlockSpec((None, 1, D), lambda i: (_mod_row(i, bsz), 0, j))


def _rope(x, cos, sin):
    w = x.shape[-1]
    lane = lax.broadcasted_iota(jnp.int32, x.shape, 1)
    partner = jnp.where((lane % 32) < 16, pltpu.roll(x, w - 16, 1), pltpu.roll(x, 16, 1))
    return x * cos + partner * sin


AQ = ATT_HEADS * HEAD_DIM
AK = ATT_KV * HEAD_DIM


def _proj_in_kernel(x_ref, g_ref, sh_ref, sc_ref, cos_ref, sin_ref, w_ref, o_ref, q_ref, kt_ref):
    h = _norm_mod(x_ref[...], g_ref[...], sh_ref[...], sc_ref[...]).astype(BF16)
    cos, sin = cos_ref[...], sin_ref[...]
    for n0 in range(0, DP, PROJ_TN):
        r = _dot(h, w_ref[:, n0:n0 + PROJ_TN])
        o_ref[:, n0:n0 + PROJ_TN] = r.astype(BF16)
        if n0 <= C_QA < n0 + PROJ_TN:
            q = _rope(r[:, C_QA - n0:C_QA - n0 + AQ], jnp.tile(cos, (1, AQ // AK)), jnp.tile(sin, (1, AQ // AK)))
            q_ref[...] = (q * (HEAD_DIM ** -0.5 * LOG2E)).astype(BF16)
        if n0 <= C_KA < n0 + PROJ_TN:
            kt_ref[...] = _rope(r[:, C_KA - n0:C_KA - n0 + AK], cos, sin).T.astype(BF16)


def _proj_in(x, g, mod, cos, sin, w, layer, bsz):
    r = x.shape[0]
    table = pl.BlockSpec((TM, AK), lambda i: (i % TILES_PER_BATCH, 0))
    return pl.pallas_call(
        _proj_in_kernel,
        grid=(r // TM,),
        in_specs=[
            pl.BlockSpec((TM, D), lambda i: (i, 0)),
            pl.BlockSpec((1, D), lambda i: (0, 0)),
            _mod_spec(0, bsz),
            _mod_spec(1, bsz),
            table, table,
            pl.BlockSpec((None, D, DP), lambda i: (layer, 0, 0), pipeline_mode=pl.Buffered(1)),
        ],
        out_specs=[pl.BlockSpec((TM, DP), lambda i: (i, 0)),
                   pl.BlockSpec((TM, AQ), lambda i: (i, 0)),
                   pl.BlockSpec((AK, TM), lambda i: (0, i))],
        out_shape=[jax.ShapeDtypeStruct((r, DP), BF16), jax.ShapeDtypeStruct((r, AQ), BF16),
                   jax.ShapeDtypeStruct((AK, r), BF16)],
        compiler_params=_params("parallel"),
        name="proj_in",
    )(x, g, mod, mod, cos, sin, w)


CONV_W = 256


def _conv_kernel(cb_ref, cc_ref, cu_ref, w_ref, o_ref):
    z = cc_ref[...].astype(F32) * cu_ref[...].astype(F32)
    row = lax.broadcasted_iota(jnp.int32, z.shape, 0)
    z_prev = jnp.where((row == 0) | (row == CTX), 0.0, pltpu.roll(z, 1, 0))
    z_next = jnp.where((row == CTX - 1) | (row == ROWS - 1), 0.0, pltpu.roll(z, ROWS - 1, 0))
    w = w_ref[...]
    y = z_prev * w[0:1] + z * w[1:2] + z_next * w[2:3]
    o_ref[...] = (cb_ref[...].astype(F32) * y).astype(BF16)


def _conv(p, conv_w, bsz):
    r = p.shape[0]
    blk = lambda c: pl.BlockSpec((ROWS, CONV_W), lambda b, j: (b, c // CONV_W + j))
    return pl.pallas_call(
        _conv_kernel,
        grid=(bsz, BRANCH_W // CONV_W),
        in_specs=[blk(C_CB), blk(C_CC), blk(C_CU), pl.BlockSpec((3, CONV_W), lambda b, j: (0, j))],
        out_specs=pl.BlockSpec((ROWS, CONV_W), lambda b, j: (b, j)),
        out_shape=jax.ShapeDtypeStruct((r, BRANCH_W), BF16),
        compiler_params=_params("parallel", "parallel"),
        name="short_conv",
    )(p, p, p, conv_w)


NQB = ROWS // QB
NCB = CTX // QB
NLK = 3 * QB


def _attn_kernel(sink_ref, q_ref, kp_ref, ko_ref, kn_ref, kc_ref, vp_ref, vo_ref, vn_ref, vc_ref, o_ref):
    n = pl.program_id(1)
    is_ctx = n < NCB
    qr = q_ref[...]
    kt = jnp.concatenate([kp_ref[...], ko_ref[...], kn_ref[...], kc_ref[...]], axis=1)
    vcat = jnp.concatenate([vp_ref[...], vo_ref[...], vn_ref[...], vc_ref[...]], axis=0)
    i = lax.broadcasted_iota(jnp.int32, (ATT_GROUP * QB, QB), 0) % QB
    j = lax.broadcasted_iota(jnp.int32, (ATT_GROUP * QB, QB), 1)
    far = 4 * QB
    pen_p = jnp.where(n > NCB, 0, far)
    pen_n = jnp.where((n >= NCB) & (n < NQB - 1), 0, far)
    mask_p = j >= i + pen_p
    mask_n = j + pen_n <= i
    outs = [None] * ATT_HEADS
    for kv in range(ATT_KV):
        heads = range(kv * ATT_GROUP, (kv + 1) * ATT_GROUP)
        qg = jnp.concatenate([qr[:, h * HEAD_DIM:(h + 1) * HEAD_DIM] for h in heads], axis=0)
        sink = jnp.concatenate([jnp.full((QB, 1), sink_ref[h] * LOG2E, F32) for h in heads], axis=0)
        s = _dot(qg, kt[kv * HEAD_DIM:(kv + 1) * HEAD_DIM])
        tiles = [jnp.where(mask_p, s[:, :QB], NEG_INF),
                 jnp.where(is_ctx, NEG_INF, s[:, QB:2 * QB]),
                 jnp.where(mask_n, s[:, 2 * QB:NLK], NEG_INF)]
        tiles += [s[:, c:c + QB] for c in range(NLK, NLK + CTX, QB)]
        m = tiles[0]
        for t in tiles[1:]:
            m = jnp.maximum(m, t)
        m = jnp.maximum(jnp.max(m, axis=-1, keepdims=True), sink)
        tiles = [jnp.exp2(t - m) for t in tiles]
        acc = tiles[0]
        for t in tiles[1:]:
            acc = acc + t
        den = jnp.sum(acc, axis=-1, keepdims=True) + jnp.exp2(sink - m)
        p = jnp.concatenate(tiles, axis=1).astype(BF16)
        o = _dot(p, vcat[:, kv * HEAD_DIM:(kv + 1) * HEAD_DIM]) / den
        for g, h in enumerate(heads):
            outs[h] = o[g * QB:(g + 1) * QB]
    o_ref[...] = jnp.concatenate(outs, axis=1).astype(BF16)


def _rope_tables():
    half = HEAD_DIM // 2
    t = jnp.arange(SEQ)
    row = (t // GRID_W).astype(F32)
    col = (t % GRID_W).astype(F32)
    inv_freq = ROPE_BASE ** (-jnp.arange(0, half, 2, dtype=F32) / half)
    ar = row[:, None] * inv_freq
    ac = col[:, None] * inv_freq
    cos = jnp.concatenate([jnp.cos(ar), jnp.cos(ar), jnp.cos(ac), jnp.cos(ac)], axis=1)
    sin = jnp.concatenate([-jnp.sin(ar), jnp.sin(ar), -jnp.sin(ac), jnp.sin(ac)], axis=1)
    cos = jnp.concatenate([jnp.ones((CTX, HEAD_DIM), F32), cos], axis=0)
    sin = jnp.concatenate([jnp.zeros((CTX, HEAD_DIM), F32), sin], axis=0)
    return jnp.tile(cos, (1, ATT_KV)), jnp.tile(sin, (1, ATT_KV))


def _attention(p, qr, kt, sink, bsz):
    r = p.shape[0]
    lo, hi = NCB, NQB - 1
    near = lambda b, n, shift: b * NQB + jnp.clip(n + shift, lo, hi)
    kblk = lambda shift: pl.BlockSpec((AK, QB), lambda b, n: (0, near(b, n, shift)))
    vblk = lambda shift: pl.BlockSpec((QB, AK), lambda b, n: (near(b, n, shift), C_VA // AK))
    return pl.pallas_call(
        _attn_kernel,
        grid=(bsz, NQB),
        in_specs=[
            pl.BlockSpec(memory_space=pltpu.SMEM),
            pl.BlockSpec((QB, AQ), lambda b, n: (b * NQB + n, 0)),
            kblk(-1), kblk(0), kblk(1),
            pl.BlockSpec((AK, CTX), lambda b, n: (0, b * (ROWS // CTX))),
            vblk(-1), vblk(0), vblk(1),
            pl.BlockSpec((CTX, AK), lambda b, n: (b * (ROWS // CTX), C_VA // AK)),
        ],
        out_specs=pl.BlockSpec((QB, AQ), lambda b, n: (b * NQB + n, 0)),
        out_shape=jax.ShapeDtypeStruct((r, AQ), BF16),
        compiler_params=_params("parallel", "parallel"),
        name="window_attention",
    )(sink, qr, kt, kt, kt, kt, p, p, p, p)


NCHUNK = ROWS // GLA_CHUNK
NCHUNK_CTX = CTX // GLA_CHUNK
GQ = GLA_HEADS * GLA_DK
GV = GLA_HEADS * GLA_DV
GLA_BLK = 256


def _log_sigmoid(z):
    return jnp.minimum(z, 0.0) - jnp.log(1.0 + jnp.exp(-jnp.abs(z)))


def _gla_kernel(q_ref, k_ref, a_ref, v_ref, r_ref, w2_ref, b2_ref, g_ref, o_ref,
                of_ref, ob_ref, qd_ref, ke_ref, dec_ref, sf_ref, sb_ref):
    ri = lax.broadcasted_iota(jnp.int32, (GLA_BLK, GLA_BLK), 0)
    rj = lax.broadcasted_iota(jnp.int32, (GLA_BLK, GLA_BLK), 1)
    same_chunk = (ri // GLA_CHUNK) == (rj // GLA_CHUNK)
    causal = (same_chunk & (ri >= rj), same_chunk & (ri <= rj))
    lane_head = lax.broadcasted_iota(jnp.int32, (GLA_BLK, GQ), 1) // GLA_DK
    outs = (of_ref, ob_ref)

    def precompute(blk, carry):
        rows = pl.ds(pl.multiple_of(blk * GLA_BLK, GLA_BLK), GLA_BLK)
        a = a_ref[rows, :]
        q = q_ref[rows, :].astype(F32) * (GLA_DK ** -0.5)
        k = k_ref[rows, :].astype(F32)
        v = v_ref[rows, :]
        for d in range(2):
            la = _log_sigmoid(_dot(a, w2_ref[d]) + b2_ref[d]) * (1.0 / GLA_TAU)
            sel = jnp.concatenate([causal[d], same_chunk], axis=0).astype(BF16)
            hi = la.astype(BF16)
            r1 = la - hi.astype(F32)
            mid = r1.astype(BF16)
            lo = (r1 - mid.astype(F32)).astype(BF16)
            sums = _dot(sel, jnp.concatenate([hi, mid, lo], axis=1))
            sums = sums[:, :GQ] + sums[:, GQ:2 * GQ] + sums[:, 2 * GQ:]
            b, tot = sums[:GLA_BLK], sums[GLA_BLK:]
            q_dec = q * jnp.exp(b)
            k_inv = (k * jnp.exp(-b)).astype(BF16)
            qd_ref[d, rows, :] = q_dec.astype(BF16)
            ke_ref[d, rows, :] = (k * jnp.exp(tot - b)).astype(BF16)
            decay = jnp.exp(tot)
            for j in range(GLA_BLK // GLA_CHUNK):
                dec_ref[d, pl.ds(blk * (GLA_BLK // GLA_CHUNK) + j, 1), :] = decay[j * GLA_CHUNK:j * GLA_CHUNK + 1]
            intra = []
            for h in range(GLA_HEADS):
                qh = jnp.where(lane_head == h, q_dec, 0.0).astype(BF16)
                att = jnp.where(causal[d], _dot_nt(qh, k_inv), 0.0)
                intra.append(_dot(att.astype(BF16), v[:, h * GLA_DV:(h + 1) * GLA_DV]))
            outs[d][rows, :] = jnp.concatenate(intra, axis=1)
        return carry

    lax.fori_loop(0, ROWS // GLA_BLK, precompute, 0, unroll=3)

    sf_ref[...] = jnp.zeros_like(sf_ref)
    sb_ref[...] = jnp.zeros_like(sb_ref)
    bi = lax.broadcasted_iota(jnp.int32, (GV, GQ), 0)
    bj = lax.broadcasted_iota(jnp.int32, (GV, GQ), 1)
    block_diag = (bi // GLA_DV) == (bj // GLA_DK)

    def scan(i, carry):
        order = (i, jnp.where(i < NCHUNK_CTX, NCHUNK_CTX - 1 - i, NCHUNK + NCHUNK_CTX - 1 - i))
        for d, s_ref in enumerate((sf_ref, sb_ref)):
            c = order[d]
            rows = pl.ds(pl.multiple_of(c * GLA_CHUNK, GLA_CHUNK), GLA_CHUNK)
            s = s_ref[...]
            outs[d][rows, :] += _dot_nt(qd_ref[d, rows, :], s.astype(BF16))
            ds = _dot_tn(v_ref[rows, :], ke_ref[d, rows, :])
            s_ref[...] = s * dec_ref[d, pl.ds(c, 1), :] + jnp.where(block_diag, ds, 0.0)
        return carry

    lax.fori_loop(0, NCHUNK, scan, 0, unroll=2)

    def finish(t, carry):
        rows = pl.ds(pl.multiple_of(t * TM, TM), TM)
        o = of_ref[rows, :] + ob_ref[rows, :]
        parts = []
        for h in range(GLA_HEADS):
            oh = o[:, h * GLA_DV:(h + 1) * GLA_DV]
            parts.append(oh * lax.rsqrt(jnp.mean(oh * oh, axis=-1, keepdims=True) + EPS))
        on = jnp.concatenate(parts, axis=1) * g_ref[...]
        rg = r_ref[rows, :].astype(F32)
        o_ref[rows, :] = (on * (rg * jax.nn.sigmoid(rg))).astype(BF16)
        return carry

    lax.fori_loop(0, ROWS // TM, finish, 0)


def _gla(p, w2p, b2, g, bsz):
    r = p.shape[0]
    blk = lambda c, w: pl.BlockSpec((ROWS, w), lambda b: (b, c // w))
    return pl.pallas_call(
        _gla_kernel,
        grid=(bsz,),
        in_specs=[
            blk(C_QG, GQ), blk(C_KG, GQ), blk(C_AG, AG_W), blk(C_VG, GV), blk(C_RG, GV),
            pl.BlockSpec((2, AG_W, GQ), lambda b: (0, 0, 0)),
            pl.BlockSpec((2, 1, GQ), lambda b: (0, 0, 0)),
            pl.BlockSpec((1, GV), lambda b: (0, 0)),
        ],
        out_specs=pl.BlockSpec((ROWS, GV), lambda b: (b, 0)),
        out_shape=jax.ShapeDtypeStruct((r, GV), BF16),
        scratch_shapes=[
            pltpu.VMEM((ROWS, GV), F32), pltpu.VMEM((ROWS, GV), F32),
            pltpu.VMEM((2, ROWS, GQ), BF16), pltpu.VMEM((2, ROWS, GQ), BF16),
            pltpu.VMEM((2, NCHUNK, GQ), F32),
            pltpu.VMEM((GV, GQ), F32), pltpu.VMEM((GV, GQ), F32),
        ],
        compiler_params=_params("parallel"),
        name="gla",
    )(p, p, p, p, p, w2p, b2, g)


def _merge_kernel(ya_ref, yb_ref, yc_ref, g0_ref, g1_ref, g2_ref, x_ref, gm_ref, wb_ref, wo_ref, o_ref):
    acc = jax.nn.sigmoid(g0_ref[...].astype(F32)) * _dot(ya_ref[...], wb_ref[0])
    acc += jax.nn.sigmoid(g1_ref[...].astype(F32)) * _dot(yb_ref[...], wb_ref[1])
    acc += jax.nn.sigmoid(g2_ref[...].astype(F32)) * _dot(yc_ref[...], wb_ref[2])
    y = _dot(acc.astype(BF16), wo_ref[...])
    o_ref[...] = x_ref[...] + gm_ref[...] * y


def _merge(ya, yb, yc, p, x, mod, wb, wo, layer, bsz):
    r = x.shape[0]
    br = pl.BlockSpec((TM, BRANCH_W), lambda i: (i, 0))
    gate = lambda k: pl.BlockSpec((TM, D), lambda i: (i, C_GATE // D + k))
    return pl.pallas_call(
        _merge_kernel,
        grid=(r // TM,),
        in_specs=[br, br, br, gate(0), gate(1), gate(2),
                  pl.BlockSpec((TM, D), lambda i: (i, 0)),
                  _mod_spec(2, bsz),
                  pl.BlockSpec((None, 3, BRANCH_W, D), lambda i: (layer, 0, 0, 0), pipeline_mode=pl.Buffered(1)),
                  pl.BlockSpec((None, D, D), lambda i: (layer, 0, 0), pipeline_mode=pl.Buffered(1))],
        out_specs=pl.BlockSpec((TM, D), lambda i: (i, 0)),
        out_shape=jax.ShapeDtypeStruct((r, D), F32),
        compiler_params=_params("parallel"),
        name="merge",
    )(ya, yb, yc, p, p, p, x, mod, wb, wo)


ROUTER_W = 128


def _router_kernel(x_ref, g_ref, sh_ref, sc_ref, w_ref, o_ref, h_ref):
    h = _norm_mod(x_ref[...], g_ref[...], sh_ref[...], sc_ref[...])
    h_ref[...] = h
    o_ref[...] = jnp.dot(h, w_ref[...], preferred_element_type=F32, precision=lax.Precision.HIGHEST)


def _router(x, g, mod, w, bsz):
    r = x.shape[0]
    return pl.pallas_call(
        _router_kernel,
        grid=(r // TM,),
        in_specs=[pl.BlockSpec((TM, D), lambda i: (i, 0)),
                  pl.BlockSpec((1, D), lambda i: (0, 0)),
                  _mod_spec(3, bsz), _mod_spec(4, bsz),
                  pl.BlockSpec((D, ROUTER_W), lambda i: (0, 0))],
        out_specs=[pl.BlockSpec((TM, ROUTER_W), lambda i: (i, 0)),
                   pl.BlockSpec((TM, D), lambda i: (i, 0))],
        out_shape=[jax.ShapeDtypeStruct((r, ROUTER_W), F32), jax.ShapeDtypeStruct((r, D), F32)],
        compiler_params=_params("parallel"),
        name="router",
    )(x, g, mod, mod, w)


MOE_T = 896
MOE_TF = 512
MOE_NF = D_FF // MOE_TF
MOE_CH = MOE_T // MOE_NF


def _moe_num_tiles(r):
    return (TOP_K * r + N_EXPERTS * (MOE_T - 1)) // MOE_T


def _route(logits, r):
    nt = _moe_num_tiles(r)
    top_v, top_i = lax.top_k(logits, TOP_K)
    top_p = jax.nn.softmax(top_v, axis=-1)
    e_flat = top_i.reshape(-1)
    onehot = (e_flat[:, None] == jnp.arange(N_EXPERTS)[None, :]).astype(jnp.int32)
    csum = jnp.cumsum(onehot, axis=0)
    rank = jnp.sum(onehot * (csum - 1), axis=1)
    counts = csum[-1]
    tiles_e = (counts + MOE_T - 1) // MOE_T
    tile_end = jnp.cumsum(tiles_e)
    n_active = tile_end[-1]
    tile_first = tile_end - tiles_e
    slot = tile_first[e_flat] * MOE_T + rank
    tile_id = jnp.minimum(jnp.arange(nt), n_active - 1)
    tile_expert = jnp.sum((tile_id[:, None] >= tile_end[None, :]).astype(jnp.int32), axis=1)
    na = TOP_K * r
    keys = jnp.sort(e_flat.astype(jnp.int32) * na + jnp.arange(na, dtype=jnp.int32))
    sorted_tok = jnp.concatenate([(keys % na) // TOP_K, jnp.zeros((MOE_T,), jnp.int32)])
    group_start = jnp.cumsum(counts) - counts
    src = group_start[tile_expert] + (tile_id - tile_first[tile_expert]) * MOE_T
    slot_token = jax.vmap(lambda s: lax.dynamic_slice(sorted_tok, (s,), (MOE_T,)))(src)
    return (tile_expert.astype(jnp.int32), n_active.reshape(1).astype(jnp.int32),
            slot_token.reshape(nt, 1, MOE_T), slot.astype(jnp.int32), top_p)


def _moe_kernel(te_ref, na_ref, tok0_ref, tokn_ref, h_hbm, w1_ref, w3_ref, w2_ref, o_ref,
                gbuf, hs_ref, acc_ref, sem):
    i = pl.program_id(0)
    f = pl.program_id(1)
    active = i < na_ref[0]
    slot = i % 2

    def issue(tok_ref, buf, r0, nrows):
        def body(r, carry):
            pltpu.make_async_copy(h_hbm.at[pl.ds(tok_ref[0, r], 1)], gbuf.at[buf, pl.ds(r, 1)], sem.at[buf]).start()
            return carry
        lax.fori_loop(r0, r0 + nrows, body, 0)

    @pl.when((i == 0) & (f == 0))
    def _():
        issue(tok0_ref, 0, 0, MOE_T)

    @pl.when(active & (f == 0))
    def _():
        pltpu.make_async_copy(h_hbm.at[pl.ds(0, MOE_T)], gbuf.at[slot], sem.at[slot]).wait()
        hs_ref[...] = gbuf[slot].astype(BF16)
        acc_ref[...] = jnp.zeros_like(acc_ref)

    has_next = i + 1 < na_ref[0]

    def step(prefetch):
        if prefetch:
            for r in range(MOE_CH):
                row = f * MOE_CH + r
                pltpu.make_async_copy(h_hbm.at[pl.ds(tokn_ref[0, row], 1)], gbuf.at[1 - slot, pl.ds(row, 1)],
                                      sem.at[1 - slot]).start()
        h = hs_ref[...]
        a = _dot(h, w1_ref[...].astype(BF16))
        t = a * jax.nn.sigmoid(a) * _dot(h, w3_ref[...].astype(BF16))
        acc_ref[...] += _dot(t.astype(BF16), w2_ref[...].astype(BF16))

    @pl.when(has_next)
    def _():
        step(True)

    @pl.when(active & jnp.logical_not(has_next))
    def _():
        step(False)

    @pl.when(f == MOE_NF - 1)
    def _():
        o_ref[...] = jnp.where(active, acc_ref[...], 0.0)


def _moe_experts(h, tile_expert, n_active, slot_token, w1, w3, w2, layer):
    r = h.shape[0]
    nt = _moe_num_tiles(r)
    hidden = lambda i, f, te, na: jnp.where(i < na[0], f, MOE_NF - 1)
    grid_spec = pltpu.PrefetchScalarGridSpec(
        num_scalar_prefetch=2,
        grid=(nt, MOE_NF),
        in_specs=[
            pl.BlockSpec((None, 1, MOE_T), lambda i, f, te, na: (0, 0, 0), memory_space=pltpu.SMEM),
            pl.BlockSpec((None, 1, MOE_T), lambda i, f, te, na: (jnp.minimum(i + 1, nt - 1), 0, 0),
                         memory_space=pltpu.SMEM),
            pl.BlockSpec(memory_space=pl.ANY),
            pl.BlockSpec((None, None, D, MOE_TF), lambda i, f, te, na: (layer, te[i], 0, hidden(i, f, te, na))),
            pl.BlockSpec((None, None, D, MOE_TF), lambda i, f, te, na: (layer, te[i], 0, hidden(i, f, te, na))),
            pl.BlockSpec((None, None, MOE_TF, D), lambda i, f, te, na: (layer, te[i], hidden(i, f, te, na), 0)),
        ],
        out_specs=pl.BlockSpec((MOE_T, D), lambda i, f, te, na: (i, 0)),
        scratch_shapes=[pltpu.VMEM((2, MOE_T, D), F32), pltpu.VMEM((MOE_T, D), BF16),
                        pltpu.VMEM((MOE_T, D), F32), pltpu.SemaphoreType.DMA((2,))],
    )
    return pl.pallas_call(
        _moe_kernel,
        grid_spec=grid_spec,
        out_shape=jax.ShapeDtypeStruct((nt * MOE_T, D), F32),
        compiler_params=_params("arbitrary", "arbitrary"),
        name="moe_experts",
    )(tile_expert, n_active, slot_token, slot_token, h, w1, w3, w2)


def _combine_kernel(pos0_ref, posn_ref, y_hbm, x_ref, p_ref, gm_ref, o_ref, ybuf, sem):
    i = pl.program_id(0)
    slot = i % 2

    def start(pos_ref, buf, r):
        for k in range(TOP_K):
            pltpu.make_async_copy(y_hbm.at[pl.ds(pos_ref[0, TOP_K * r + k], 1)],
                                  ybuf.at[buf, pl.ds(k * TM + r, 1)], sem.at[buf]).start()

    @pl.when(i == 0)
    def _():
        def body(r, carry):
            start(pos0_ref, 0, r)
            return carry
        lax.fori_loop(0, TM, body, 0)

    @pl.when(i + 1 < pl.num_programs(0))
    def _():
        for r in range(TM):
            start(posn_ref, 1 - slot, r)

    pltpu.make_async_copy(y_hbm.at[pl.ds(0, TOP_K * TM)], ybuf.at[slot], sem.at[slot]).wait()
    p = p_ref[...]
    y = p[:, 0:1] * ybuf[slot, 0:TM, :]
    for k in range(1, TOP_K):
        y += p[:, k:k + 1] * ybuf[slot, k * TM:(k + 1) * TM, :]
    o_ref[...] = x_ref[...] + gm_ref[...] * y


def _moe_combine(y, pos, top_p, x, mod, bsz):
    r = x.shape[0]
    nt = r // TM
    pos = pos.reshape(nt, 1, TOP_K * TM)
    return pl.pallas_call(
        _combine_kernel,
        grid=(nt,),
        in_specs=[
            pl.BlockSpec((None, 1, TOP_K * TM), lambda i: (0, 0, 0), memory_space=pltpu.SMEM),
            pl.BlockSpec((None, 1, TOP_K * TM), lambda i: (jnp.minimum(i + 1, nt - 1), 0, 0),
                         memory_space=pltpu.SMEM),
            pl.BlockSpec(memory_space=pl.ANY),
            pl.BlockSpec((TM, D), lambda i: (i, 0)),
            pl.BlockSpec((TM, TOP_K), lambda i: (i, 0)),
            _mod_spec(5, bsz),
        ],
        out_specs=pl.BlockSpec((TM, D), lambda i: (i, 0)),
        out_shape=jax.ShapeDtypeStruct((r, D), F32),
        scratch_shapes=[pltpu.VMEM((2, TOP_K * TM, D), F32), pltpu.SemaphoreType.DMA((2,))],
        compiler_params=_params("arbitrary"),
        name="moe_combine",
    )(pos, pos, y, x, top_p, mod)


FFN_TILES_PER_BATCH = ROWS // FFN_TM


def _ffn_kernel(x_ref, g_ref, mx_ref, mc_ref, w1_ref, w3_ref, w2_ref, o_ref, h_ref, acc_ref):
    f = pl.program_id(1)
    row = lax.broadcasted_iota(jnp.int32, (FFN_TM, 1), 0) + (pl.program_id(0) % FFN_TILES_PER_BATCH) * FFN_TM
    is_ctx = row < CTX

    def mod(j):
        return jnp.where(is_ctx, mc_ref[:, j * D:(j + 1) * D], mx_ref[:, j * D:(j + 1) * D])

    @pl.when(f == 0)
    def _():
        h_ref[...] = _norm_mod(x_ref[...], g_ref[...], mod(3), mod(4)).astype(BF16)
        acc_ref[...] = jnp.zeros_like(acc_ref)

    h = h_ref[...]
    a = _dot(h, w1_ref[...])
    t = a * jax.nn.sigmoid(a) * _dot(h, w3_ref[...])
    acc_ref[...] += _dot(t.astype(BF16), w2_ref[...])

    @pl.when(f == pl.num_programs(1) - 1)
    def _():
        o_ref[...] = x_ref[...] + mod(5) * acc_ref[...]


def _ffn(x, g, mod, w1, w3, w2, layer, bsz):
    r = x.shape[0]
    row = pl.BlockSpec((FFN_TM, D), lambda i, f: (i, 0))
    return pl.pallas_call(
        _ffn_kernel,
        grid=(r // FFN_TM, D_FF // FFN_TF),
        in_specs=[row,
                  pl.BlockSpec((1, D), lambda i, f: (0, 0)),
                  pl.BlockSpec((None, 1, 6 * D), lambda i, f: (i // FFN_TILES_PER_BATCH, 0, 0)),
                  pl.BlockSpec((None, 1, 6 * D), lambda i, f: (bsz, 0, 0)),
                  pl.BlockSpec((None, D, FFN_TF), lambda i, f: (layer, 0, f)),
                  pl.BlockSpec((None, D, FFN_TF), lambda i, f: (layer, 0, f)),
                  pl.BlockSpec((None, FFN_TF, D), lambda i, f: (layer, f, 0))],
        out_specs=row,
        out_shape=jax.ShapeDtypeStruct((r, D), F32),
        scratch_shapes=[pltpu.VMEM((FFN_TM, D), BF16), pltpu.VMEM((FFN_TM, D), F32)],
        compiler_params=_params("parallel", "arbitrary"),
        name="ffn",
    )(x, g, mod, mod, w1, w3, w2)


def _final_kernel(x_ref, g_ref, o_ref):
    x = x_ref[...]
    o_ref[...] = x * lax.rsqrt(jnp.mean(x * x, axis=-1, keepdims=True) + EPS) * g_ref[...]


def _final_norm(x, g, bsz):
    lat = SEQ // TM
    return pl.pallas_call(
        _final_kernel,
        grid=(bsz, lat),
        in_specs=[pl.BlockSpec((TM, D), lambda b, j: (b * TILES_PER_BATCH + CTX // TM + j, 0)),
                  pl.BlockSpec((1, D), lambda b, j: (0, 0))],
        out_specs=pl.BlockSpec((TM, D), lambda b, j: (b * lat + j, 0)),
        out_shape=jax.ShapeDtypeStruct((bsz * SEQ, D), F32),
        compiler_params=_params("parallel", "parallel"),
        name="final_norm",
    )(x, g)


D_IN = 6944
SRC_VG = 2816
SRC_AG = SRC_VG + 2 * GV
SRC_GATE = SRC_AG + 2 * GLA_RANK
LAYOUT_TM = 256


def _w_in_layout_kernel(w_ref, o_ref):
    o_ref[:, :C_AG] = w_ref[:, :SRC_VG].astype(BF16)
    o_ref[:, C_AG:C_AG + AG_W] = jnp.zeros((LAYOUT_TM, AG_W), BF16)
    o_ref[:, C_AG:C_AG + 2 * GLA_RANK] = w_ref[:, SRC_AG:SRC_GATE].astype(BF16)
    o_ref[:, C_AG + AG_W:C_VG] = jnp.zeros((LAYOUT_TM, C_VG - C_AG - AG_W), BF16)
    o_ref[:, C_VG:C_GATE] = w_ref[:, SRC_VG:SRC_AG].astype(BF16)
    o_ref[:, C_GATE:] = w_ref[:, SRC_GATE:].astype(BF16)


def _layout_w_in(w_in):
    depth = w_in.shape[0]
    assert w_in.shape[1:] == (D, D_IN)
    return pl.pallas_call(
        _w_in_layout_kernel,
        grid=(depth, D // LAYOUT_TM),
        in_specs=[pl.BlockSpec((None, LAYOUT_TM, D_IN), lambda l, i: (l, i, 0))],
        out_specs=pl.BlockSpec((None, LAYOUT_TM, DP), lambda l, i: (l, i, 0)),
        out_shape=jax.ShapeDtypeStruct((depth, D, DP), BF16),
        compiler_params=_params("parallel", "parallel"),
        name="w_in_layout",
    )(w_in)


def _layout_gla_w2(gla_w2):
    depth = gla_w2.shape[0]
    out = jnp.zeros((depth, 2, AG_W, GQ), F32)
    for d in range(2):
        out = out.at[:, d, d * GLA_RANK:(d + 1) * GLA_RANK, :].set(gla_w2[:, d])
    return out.astype(BF16)


def kernel(x, c, ctx, c_ctx, w_mod, b_mod, norm1_g, norm2_g, w_in, conv_w, attn_sink, gla_w2, gla_b,
           gla_norm_g, w_branch, w_out, ffn_w1, ffn_w3, ffn_w2, router_w, moe_w1, moe_w3, moe_w2,
           final_norm_g):
    bsz = x.shape[0]
    depth = w_in.shape[0]
    assert x.shape[1:] == (SEQ, D) and ctx.shape[1:] == (CTX, D)

    nrow = -(-(bsz + 1) // 8) * 8
    cond = jnp.zeros((nrow, D), F32).at[:bsz].set(c).at[bsz].set(c_ctx)
    mods = _modulation(cond, w_mod, b_mod).reshape(depth, nrow, 1, 6 * D)

    xs = jnp.concatenate([ctx, x], axis=1).reshape(bsz * ROWS, D)
    w_in_p = _layout_w_in(w_in)
    w2p = _layout_gla_w2(gla_w2)
    cos, sin = _rope_tables()
    router_p = jnp.pad(router_w, ((0, 0), (0, 0), (0, ROUTER_W - N_EXPERTS)))

    wb, wo = w_branch.astype(BF16), w_out.astype(BF16)
    f1, f3, f2 = ffn_w1.astype(BF16), ffn_w3.astype(BF16), ffn_w2.astype(BF16)

    for l in range(depth):
        mod = mods[l]
        p, qr, kt = _proj_in(xs, norm1_g[l].reshape(1, D), mod, cos, sin, w_in_p, l, bsz)
        ya = _conv(p, conv_w[l], bsz)
        yb = _attention(p, qr, kt, attn_sink[l], bsz)
        yc = _gla(p, w2p[l], gla_b[l].reshape(2, 1, GQ), gla_norm_g[l].reshape(1, GV), bsz)
        xs = _merge(ya, yb, yc, p, xs, mod, wb, wo, l, bsz)
        g2 = norm2_g[l].reshape(1, D)
        j = l // 2
        if l % 2 == 0:
            xs = _ffn(xs, g2, mod, f1, f3, f2, j, bsz)
        else:
            logits, h2 = _router(xs, g2, mod, router_p[j], bsz)
            tile_expert, n_active, slot_token, pos, top_p = _route(logits[:, :N_EXPERTS], xs.shape[0])
            y = _moe_experts(h2, tile_expert, n_active, slot_token, moe_w1, moe_w3, moe_w2, j)
            xs = _moe_combine(y, pos, top_p, xs, mod, bsz)
    out = _final_norm(xs, final_norm_g.reshape(1, D), bsz)
    return out.reshape(bsz, SEQ, D)
```

```python
import functools

import jax
import jax.numpy as jnp
from jax import lax
from jax.experimental import pallas as pl
from jax.experimental.pallas import tpu as pltpu

F32 = jnp.float32
BF16 = jnp.bfloat16

D = 1024
SEQ = 2048
CTX = 256
ROWS = CTX + SEQ
GRID_W = 64
EPS = 1e-6
NEG_INF = -1e30
LOG2E = 1.4426950408889634

BRANCH_W = 512
ATT_HEADS = 8
ATT_KV = 2
ATT_GROUP = ATT_HEADS // ATT_KV
HEAD_DIM = 64
QB = 128
ROPE_BASE = 10000.0
GLA_HEADS = 4
GLA_DK = 64
GLA_DV = 128
GLA_RANK = 16
GLA_TAU = 16.0
GLA_CHUNK = 64
D_FF = 3584
N_EXPERTS = 8
TOP_K = 2

C_CB, C_CC, C_CU = 0, 512, 1024
C_QA, C_KA, C_VA = 1536, 2048, 2176
C_QG, C_KG, C_AG = 2304, 2560, 2816
C_VG, C_RG, C_GATE = 3072, 3584, 4096
DP = 7168
AG_W = 128

TM = 256
TILES_PER_BATCH = ROWS // TM
PROJ_TN = 1024
FFN_TM = 768
FFN_TF = 512
VMEM_LIMIT = 56 * 1024 * 1024


def _params(*sem):
    return pltpu.CompilerParams(dimension_semantics=sem, vmem_limit_bytes=VMEM_LIMIT)


def _dot(a, b):
    return jnp.dot(a, b, preferred_element_type=F32)


def _dot_nt(a, b):
    return lax.dot_general(a, b, (((1,), (1,)), ((), ())), preferred_element_type=F32)


def _dot_tn(a, b):
    return lax.dot_general(a, b, (((0,), (0,)), ((), ())), preferred_element_type=F32)


def _norm_mod(x, g, shift, scale):
    h = x * lax.rsqrt(jnp.mean(x * x, axis=-1, keepdims=True) + EPS) * g
    return h * (1.0 + scale) + shift


def _mod_kernel(s_ref, w_ref, b_ref, o_ref):
    s = s_ref[...]
    s = s * jax.nn.sigmoid(s)
    o_ref[...] = _dot(s.astype(BF16), w_ref[...].astype(BF16)) + b_ref[...]


def _modulation(cond, w_mod, b_mod):
    depth = w_mod.shape[0]
    nrow = cond.shape[0]
    return pl.pallas_call(
        _mod_kernel,
        grid=(depth, 6),
        in_specs=[
            pl.BlockSpec((nrow, D), lambda l, j: (0, 0)),
            pl.BlockSpec((None, D, D), lambda l, j: (l, 0, j)),
            pl.BlockSpec((None, 1, D), lambda l, j: (l, 0, j)),
        ],
        out_specs=pl.BlockSpec((None, nrow, D), lambda l, j: (l, 0, j)),
        out_shape=jax.ShapeDtypeStruct((depth, nrow, 6 * D), F32),
        compiler_params=_params("parallel", "parallel"),
        name="modulation",
    )(cond, w_mod, b_mod.reshape(depth, 1, 6 * D))


def _mod_row(i, bsz):
    return jnp.where(i % TILES_PER_BATCH == 0, bsz, i // TILES_PER_BATCH)


def _mod_spec(j, bsz):
    return pl.BlockSpec((None, 1, D), lambda i: (_mod_row(i, bsz), 0, j))


def _rope(x, cos, sin):
    w = x.shape[-1]
    lane = lax.broadcasted_iota(jnp.int32, x.shape, 1)
    partner = jnp.where((lane % 32) < 16, pltpu.roll(x, w - 16, 1), pltpu.roll(x, 16, 1))
    return x * cos + partner * sin


AQ = ATT_HEADS * HEAD_DIM
AK = ATT_KV * HEAD_DIM


def _moe_residual(x_ref, y0_ref, y1_ref, p_ref, gm_ref):
    p = p_ref[...]
    return x_ref[...] + gm_ref[...] * (p[:, 0:1] * y0_ref[...] + p[:, 1:2] * y1_ref[...])


def _proj_in_kernel(moe_in, *refs):
    if moe_in:
        x_ref, y0_ref, y1_ref, p_ref, gm_ref = refs[:5]
        g_ref, sh_ref, sc_ref, cos_ref, sin_ref, w_ref, o_ref, q_ref, kt_ref, xo_ref = refs[5:]
        x = _moe_residual(x_ref, y0_ref, y1_ref, p_ref, gm_ref)
        xo_ref[...] = x
    else:
        x_ref, g_ref, sh_ref, sc_ref, cos_ref, sin_ref, w_ref, o_ref, q_ref, kt_ref = refs
        x = x_ref[...]
    h = _norm_mod(x, g_ref[...], sh_ref[...], sc_ref[...]).astype(BF16)
    cos, sin = cos_ref[...], sin_ref[...]
    for n0 in range(0, DP, PROJ_TN):
        r = _dot(h, w_ref[:, n0:n0 + PROJ_TN])
        o_ref[:, n0:n0 + PROJ_TN] = r.astype(BF16)
        if n0 <= C_QA < n0 + PROJ_TN:
            q = _rope(r[:, C_QA - n0:C_QA - n0 + AQ], jnp.tile(cos, (1, AQ // AK)), jnp.tile(sin, (1, AQ // AK)))
            q_ref[...] = (q * (HEAD_DIM ** -0.5 * LOG2E)).astype(BF16)
        if n0 <= C_KA < n0 + PROJ_TN:
            kt_ref[...] = _rope(r[:, C_KA - n0:C_KA - n0 + AK], cos, sin).T.astype(BF16)


def _moe_in_specs(r, tile, gate_spec):
    return [pl.BlockSpec((TM, D), lambda *ix: (tile(*ix), 0)),
            pl.BlockSpec((TM, D), lambda *ix: (r // TM + tile(*ix), 0)),
            pl.BlockSpec((TM, TOP_K), lambda *ix: (tile(*ix), 0)),
            gate_spec]


def _proj_in(x, g, mod, cos, sin, w, layer, bsz, moe_in=None):
    r = x.shape[0]
    table = pl.BlockSpec((TM, AK), lambda i: (i % TILES_PER_BATCH, 0))
    row = pl.BlockSpec((TM, D), lambda i: (i, 0))
    in_specs, args = [row], [x]
    out_specs = [pl.BlockSpec((TM, DP), lambda i: (i, 0)),
                 pl.BlockSpec((TM, AQ), lambda i: (i, 0)),
                 pl.BlockSpec((AK, TM), lambda i: (0, i))]
    out_shape = [jax.ShapeDtypeStruct((r, DP), BF16), jax.ShapeDtypeStruct((r, AQ), BF16),
                 jax.ShapeDtypeStruct((AK, r), BF16)]
    if moe_in is not None:
        y, top_p, prev_mod = moe_in
        in_specs += _moe_in_specs(r, lambda i: i, _mod_spec(5, bsz))
        args += [y, y, top_p, prev_mod]
        out_specs.append(row)
        out_shape.append(jax.ShapeDtypeStruct((r, D), F32))
    in_specs += [pl.BlockSpec((1, D), lambda i: (0, 0)), _mod_spec(0, bsz), _mod_spec(1, bsz), table, table,
                 pl.BlockSpec((None, D, DP), lambda i: (layer, 0, 0), pipeline_mode=pl.Buffered(1))]
    args += [g, mod, mod, cos, sin, w]
    return pl.pallas_call(
        functools.partial(_proj_in_kernel, moe_in is not None),
        grid=(r // TM,),
        in_specs=in_specs,
        out_specs=out_specs,
        out_shape=out_shape,
        compiler_params=_params("parallel"),
        name="proj_in",
    )(*args)


CONV_W = 256


def _conv_kernel(cb_ref, cc_ref, cu_ref, w_ref, o_ref):
    z = cc_ref[...].astype(F32) * cu_ref[...].astype(F32)
    row = lax.broadcasted_iota(jnp.int32, z.shape, 0)
    z_prev = jnp.where((row == 0) | (row == CTX), 0.0, pltpu.roll(z, 1, 0))
    z_next = jnp.where((row == CTX - 1) | (row == ROWS - 1), 0.0, pltpu.roll(z, ROWS - 1, 0))
    w = w_ref[...]
    y = z_prev * w[0:1] + z * w[1:2] + z_next * w[2:3]
    o_ref[...] = (cb_ref[...].astype(F32) * y).astype(BF16)


def _conv(p, conv_w, bsz):
    r = p.shape[0]
    blk = lambda c: pl.BlockSpec((ROWS, CONV_W), lambda b, j: (b, c // CONV_W + j))
    return pl.pallas_call(
        _conv_kernel,
        grid=(bsz, BRANCH_W // CONV_W),
        in_specs=[blk(C_CB), blk(C_CC), blk(C_CU), pl.BlockSpec((3, CONV_W), lambda b, j: (0, j))],
        out_specs=pl.BlockSpec((ROWS, CONV_W), lambda b, j: (b, j)),
        out_shape=jax.ShapeDtypeStruct((r, BRANCH_W), BF16),
        compiler_params=_params("parallel", "parallel"),
        name="short_conv",
    )(p, p, p, conv_w)


NQB = ROWS // QB
NCB = CTX // QB
NLK = 3 * QB


def _attn_kernel(sink_ref, q_ref, kp_ref, ko_ref, kn_ref, kc_ref, vp_ref, vo_ref, vn_ref, vc_ref, o_ref):
    n = pl.program_id(1)
    is_ctx = n < NCB
    qr = q_ref[...]
    kt = jnp.concatenate([kp_ref[...], ko_ref[...], kn_ref[...], kc_ref[...]], axis=1)
    vcat = jnp.concatenate([vp_ref[...], vo_ref[...], vn_ref[...], vc_ref[...]], axis=0)
    i = lax.broadcasted_iota(jnp.int32, (ATT_GROUP * QB, QB), 0) % QB
    j = lax.broadcasted_iota(jnp.int32, (ATT_GROUP * QB, QB), 1)
    far = 4 * QB
    pen_p = jnp.where(n > NCB, 0, far)
    pen_n = jnp.where((n >= NCB) & (n < NQB - 1), 0, far)
    mask_p = j >= i + pen_p
    mask_n = j + pen_n <= i
    outs = [None] * ATT_HEADS
    for kv in range(ATT_KV):
        heads = range(kv * ATT_GROUP, (kv + 1) * ATT_GROUP)
        qg = jnp.concatenate([qr[:, h * HEAD_DIM:(h + 1) * HEAD_DIM] for h in heads], axis=0)
        sink = jnp.concatenate([jnp.full((QB, 1), sink_ref[h] * LOG2E, F32) for h in heads], axis=0)
        s = _dot(qg, kt[kv * HEAD_DIM:(kv + 1) * HEAD_DIM])
        tiles = [jnp.where(mask_p, s[:, :QB], NEG_INF),
                 jnp.where(is_ctx, NEG_INF, s[:, QB:2 * QB]),
                 jnp.where(mask_n, s[:, 2 * QB:NLK], NEG_INF)]
        tiles += [s[:, c:c + QB] for c in range(NLK, NLK + CTX, QB)]
        m = tiles[0]
        for t in tiles[1:]:
            m = jnp.maximum(m, t)
        m = jnp.maximum(jnp.max(m, axis=-1, keepdims=True), sink)
        tiles = [jnp.exp2(t - m) for t in tiles]
        acc = tiles[0]
        for t in tiles[1:]:
            acc = acc + t
        den = jnp.sum(acc, axis=-1, keepdims=True) + jnp.exp2(sink - m)
        p = jnp.concatenate(tiles, axis=1).astype(BF16)
        o = _dot(p, vcat[:, kv * HEAD_DIM:(kv + 1) * HEAD_DIM]) / den
        for g, h in enumerate(heads):
            outs[h] = o[g * QB:(g + 1) * QB]
    o_ref[...] = jnp.concatenate(outs, axis=1).astype(BF16)


def _rope_tables():
    half = HEAD_DIM // 2
    t = jnp.arange(SEQ)
    row = (t // GRID_W).astype(F32)
    col = (t % GRID_W).astype(F32)
    inv_freq = ROPE_BASE ** (-jnp.arange(0, half, 2, dtype=F32) / half)
    ar = row[:, None] * inv_freq
    ac = col[:, None] * inv_freq
    cos = jnp.concatenate([jnp.cos(ar), jnp.cos(ar), jnp.cos(ac), jnp.cos(ac)], axis=1)
    sin = jnp.concatenate([-jnp.sin(ar), jnp.sin(ar), -jnp.sin(ac), jnp.sin(ac)], axis=1)
    cos = jnp.concatenate([jnp.ones((CTX, HEAD_DIM), F32), cos], axis=0)
    sin = jnp.concatenate([jnp.zeros((CTX, HEAD_DIM), F32), sin], axis=0)
    return jnp.tile(cos, (1, ATT_KV)), jnp.tile(sin, (1, ATT_KV))


def _attention(p, qr, kt, sink, bsz):
    r = p.shape[0]
    lo, hi = NCB, NQB - 1
    near = lambda b, n, shift: b * NQB + jnp.clip(n + shift, lo, hi)
    kblk = lambda shift: pl.BlockSpec((AK, QB), lambda b, n: (0, near(b, n, shift)))
    vblk = lambda shift: pl.BlockSpec((QB, AK), lambda b, n: (near(b, n, shift), C_VA // AK))
    return pl.pallas_call(
        _attn_kernel,
        grid=(bsz, NQB),
        in_specs=[
            pl.BlockSpec(memory_space=pltpu.SMEM),
            pl.BlockSpec((QB, AQ), lambda b, n: (b * NQB + n, 0)),
            kblk(-1), kblk(0), kblk(1),
            pl.BlockSpec((AK, CTX), lambda b, n: (0, b * (ROWS // CTX))),
            vblk(-1), vblk(0), vblk(1),
            pl.BlockSpec((CTX, AK), lambda b, n: (b * (ROWS // CTX), C_VA // AK)),
        ],
        out_specs=pl.BlockSpec((QB, AQ), lambda b, n: (b * NQB + n, 0)),
        out_shape=jax.ShapeDtypeStruct((r, AQ), BF16),
        compiler_params=_params("parallel", "parallel"),
        name="window_attention",
    )(sink, qr, kt, kt, kt, kt, p, p, p, p)


NCHUNK = ROWS // GLA_CHUNK
NCHUNK_CTX = CTX // GLA_CHUNK
GQ = GLA_HEADS * GLA_DK
GV = GLA_HEADS * GLA_DV
GLA_BLK = 256


def _log_sigmoid(z):
    return jnp.minimum(z, 0.0) - jnp.log(1.0 + jnp.exp(-jnp.abs(z)))


def _gla_kernel(q_ref, k_ref, a_ref, v_ref, r_ref, w2_ref, b2_ref, g_ref, o_ref,
                of_ref, ob_ref, qd_ref, ke_ref, dec_ref, sf_ref, sb_ref):
    ri = lax.broadcasted_iota(jnp.int32, (GLA_BLK, GLA_BLK), 0)
    rj = lax.broadcasted_iota(jnp.int32, (GLA_BLK, GLA_BLK), 1)
    same_chunk = (ri // GLA_CHUNK) == (rj // GLA_CHUNK)
    causal = (same_chunk & (ri >= rj), same_chunk & (ri <= rj))
    lane_head = lax.broadcasted_iota(jnp.int32, (GLA_BLK, GQ), 1) // GLA_DK
    outs = (of_ref, ob_ref)

    def precompute(blk, carry):
        rows = pl.ds(pl.multiple_of(blk * GLA_BLK, GLA_BLK), GLA_BLK)
        a = a_ref[rows, :]
        q = q_ref[rows, :].astype(F32) * (GLA_DK ** -0.5)
        k = k_ref[rows, :].astype(F32)
        v = v_ref[rows, :]
        for d in range(2):
            la = _log_sigmoid(_dot(a, w2_ref[d]) + b2_ref[d]) * (1.0 / GLA_TAU)
            sel = jnp.concatenate([causal[d], same_chunk], axis=0).astype(BF16)
            hi = la.astype(BF16)
            r1 = la - hi.astype(F32)
            mid = r1.astype(BF16)
            lo = (r1 - mid.astype(F32)).astype(BF16)
            sums = _dot(sel, jnp.concatenate([hi, mid, lo], axis=1))
            sums = sums[:, :GQ] + sums[:, GQ:2 * GQ] + sums[:, 2 * GQ:]
            b, tot = sums[:GLA_BLK], sums[GLA_BLK:]
            q_dec = q * jnp.exp(b)
            k_inv = (k * jnp.exp(-b)).astype(BF16)
            qd_ref[d, rows, :] = q_dec.astype(BF16)
            ke_ref[d, rows, :] = (k * jnp.exp(tot - b)).astype(BF16)
            decay = jnp.exp(tot)
            for j in range(GLA_BLK // GLA_CHUNK):
                dec_ref[d, pl.ds(blk * (GLA_BLK // GLA_CHUNK) + j, 1), :] = decay[j * GLA_CHUNK:j * GLA_CHUNK + 1]
            intra = []
            for h in range(GLA_HEADS):
                qh = jnp.where(lane_head == h, q_dec, 0.0).astype(BF16)
                att = jnp.where(causal[d], _dot_nt(qh, k_inv), 0.0)
                intra.append(_dot(att.astype(BF16), v[:, h * GLA_DV:(h + 1) * GLA_DV]))
            outs[d][rows, :] = jnp.concatenate(intra, axis=1)
        return carry

    lax.fori_loop(0, ROWS // GLA_BLK, precompute, 0, unroll=3)

    sf_ref[...] = jnp.zeros_like(sf_ref)
    sb_ref[...] = jnp.zeros_like(sb_ref)
    bi = lax.broadcasted_iota(jnp.int32, (GV, GQ), 0)
    bj = lax.broadcasted_iota(jnp.int32, (GV, GQ), 1)
    block_diag = (bi // GLA_DV) == (bj // GLA_DK)

    def scan(i, carry):
        order = (i, jnp.where(i < NCHUNK_CTX, NCHUNK_CTX - 1 - i, NCHUNK + NCHUNK_CTX - 1 - i))
        for d, s_ref in enumerate((sf_ref, sb_ref)):
            c = order[d]
            rows = pl.ds(pl.multiple_of(c * GLA_CHUNK, GLA_CHUNK), GLA_CHUNK)
            s = s_ref[...]
            outs[d][rows, :] += _dot_nt(qd_ref[d, rows, :], s.astype(BF16))
            ds = _dot_tn(v_ref[rows, :], ke_ref[d, rows, :])
            s_ref[...] = s * dec_ref[d, pl.ds(c, 1), :] + jnp.where(block_diag, ds, 0.0)
        return carry

    lax.fori_loop(0, NCHUNK, scan, 0, unroll=2)

    def finish(t, carry):
        rows = pl.ds(pl.multiple_of(t * TM, TM), TM)
        o = of_ref[rows, :] + ob_ref[rows, :]
        parts = []
        for h in range(GLA_HEADS):
            oh = o[:, h * GLA_DV:(h + 1) * GLA_DV]
            parts.append(oh * lax.rsqrt(jnp.mean(oh * oh, axis=-1, keepdims=True) + EPS))
        on = jnp.concatenate(parts, axis=1) * g_ref[...]
        rg = r_ref[rows, :].astype(F32)
        o_ref[rows, :] = (on * (rg * jax.nn.sigmoid(rg))).astype(BF16)
        return carry

    lax.fori_loop(0, ROWS // TM, finish, 0)


def _gla(p, w2p, b2, g, bsz):
    r = p.shape[0]
    blk = lambda c, w: pl.BlockSpec((ROWS, w), lambda b: (b, c // w))
    return pl.pallas_call(
        _gla_kernel,
        grid=(bsz,),
        in_specs=[
            blk(C_QG, GQ), blk(C_KG, GQ), blk(C_AG, AG_W), blk(C_VG, GV), blk(C_RG, GV),
            pl.BlockSpec((2, AG_W, GQ), lambda b: (0, 0, 0)),
            pl.BlockSpec((2, 1, GQ), lambda b: (0, 0, 0)),
            pl.BlockSpec((1, GV), lambda b: (0, 0)),
        ],
        out_specs=pl.BlockSpec((ROWS, GV), lambda b: (b, 0)),
        out_shape=jax.ShapeDtypeStruct((r, GV), BF16),
        scratch_shapes=[
            pltpu.VMEM((ROWS, GV), F32), pltpu.VMEM((ROWS, GV), F32),
            pltpu.VMEM((2, ROWS, GQ), BF16), pltpu.VMEM((2, ROWS, GQ), BF16),
            pltpu.VMEM((2, NCHUNK, GQ), F32),
            pltpu.VMEM((GV, GQ), F32), pltpu.VMEM((GV, GQ), F32),
        ],
        compiler_params=_params("parallel"),
        name="gla",
    )(p, p, p, p, p, w2p, b2, g)


def _merge_kernel(ya_ref, yb_ref, yc_ref, g0_ref, g1_ref, g2_ref, x_ref, gm_ref, wb_ref, wo_ref, o_ref):
    acc = jax.nn.sigmoid(g0_ref[...].astype(F32)) * _dot(ya_ref[...], wb_ref[0])
    acc += jax.nn.sigmoid(g1_ref[...].astype(F32)) * _dot(yb_ref[...], wb_ref[1])
    acc += jax.nn.sigmoid(g2_ref[...].astype(F32)) * _dot(yc_ref[...], wb_ref[2])
    y = _dot(acc.astype(BF16), wo_ref[...])
    o_ref[...] = x_ref[...] + gm_ref[...] * y


def _merge(ya, yb, yc, p, x, mod, wb, wo, layer, bsz):
    r = x.shape[0]
    br = pl.BlockSpec((TM, BRANCH_W), lambda i: (i, 0))
    gate = lambda k: pl.BlockSpec((TM, D), lambda i: (i, C_GATE // D + k))
    return pl.pallas_call(
        _merge_kernel,
        grid=(r // TM,),
        in_specs=[br, br, br, gate(0), gate(1), gate(2),
                  pl.BlockSpec((TM, D), lambda i: (i, 0)),
                  _mod_spec(2, bsz),
                  pl.BlockSpec((None, 3, BRANCH_W, D), lambda i: (layer, 0, 0, 0), pipeline_mode=pl.Buffered(1)),
                  pl.BlockSpec((None, D, D), lambda i: (layer, 0, 0), pipeline_mode=pl.Buffered(1))],
        out_specs=pl.BlockSpec((TM, D), lambda i: (i, 0)),
        out_shape=jax.ShapeDtypeStruct((r, D), F32),
        compiler_params=_params("parallel"),
        name="merge",
    )(ya, yb, yc, p, p, p, x, mod, wb, wo)


ROUTER_W = 128


def _router_kernel(x_ref, g_ref, sh_ref, sc_ref, w_ref, o_ref, h_ref):
    h = _norm_mod(x_ref[...], g_ref[...], sh_ref[...], sc_ref[...])
    h_ref[...] = h
    o_ref[...] = jnp.dot(h, w_ref[...], preferred_element_type=F32, precision=lax.Precision.HIGHEST)


def _router(x, g, mod, w, bsz):
    r = x.shape[0]
    return pl.pallas_call(
        _router_kernel,
        grid=(r // TM,),
        in_specs=[pl.BlockSpec((TM, D), lambda i: (i, 0)),
                  pl.BlockSpec((1, D), lambda i: (0, 0)),
                  _mod_spec(3, bsz), _mod_spec(4, bsz),
                  pl.BlockSpec((D, ROUTER_W), lambda i: (0, 0))],
        out_specs=[pl.BlockSpec((TM, ROUTER_W), lambda i: (i, 0)),
                   pl.BlockSpec((TM, D), lambda i: (i, 0))],
        out_shape=[jax.ShapeDtypeStruct((r, ROUTER_W), F32), jax.ShapeDtypeStruct((r, D), F32)],
        compiler_params=_params("parallel"),
        name="router",
    )(x, g, mod, mod, w)


MOE_T = 896
MOE_TF = 512
MOE_NF = D_FF // MOE_TF
MOE_CH = MOE_T // MOE_NF
MOE_ID = 1 << 17


def _moe_num_tiles(r):
    return (TOP_K * r + N_EXPERTS * (MOE_T - 1)) // MOE_T


def _route(logits, r):
    nt = _moe_num_tiles(r)
    top_v, top_i = lax.top_k(logits, TOP_K)
    top_p = jax.nn.softmax(top_v, axis=-1)
    na = TOP_K * r
    nfill = nt * MOE_T - na
    assert na + nfill <= MOE_ID and na > MOE_T
    e_flat = top_i.reshape(-1).astype(jnp.int32)
    counts = jnp.sum((e_flat[:, None] == jnp.arange(N_EXPERTS)[None, :]).astype(jnp.int32), axis=0)
    tiles_e = (counts + MOE_T - 1) // MOE_T
    tile_end = jnp.cumsum(tiles_e)
    n_active = tile_end[-1]
    tile_id = jnp.minimum(jnp.arange(nt), n_active - 1)
    tile_expert = jnp.sum((tile_id[:, None] >= tile_end[None, :]).astype(jnp.int32), axis=1)
    fill = jnp.arange(nfill, dtype=jnp.int32)
    pad_end = jnp.cumsum(tiles_e * MOE_T - counts)
    e_fill = jnp.sum((fill[:, None] >= pad_end[None, :]).astype(jnp.int32), axis=1)
    ident = jnp.sort(jnp.concatenate([e_flat * MOE_ID + jnp.arange(na, dtype=jnp.int32),
                                      e_fill * MOE_ID + na + fill])) % MOE_ID
    real = ident < na
    src = jnp.where(real, ident // TOP_K, 0)
    dst = jnp.where(real, (ident % TOP_K) * r + ident // TOP_K, ident)
    return (tile_expert.astype(jnp.int32), n_active.reshape(1).astype(jnp.int32),
            src.reshape(nt, 1, MOE_T), dst.reshape(nt, 1, MOE_T), top_p)


def _moe_kernel(te_ref, na_ref, src0_ref, srcn_ref, dstp_ref, dstl_ref, h_hbm, w1_ref, w3_ref, w2_ref, y_hbm,
                gbuf, hs_ref, acc_ref, obuf, gsem, osem):
    i = pl.program_id(0)
    f = pl.program_id(1)
    nt = pl.num_programs(0)
    active = i < na_ref[0]
    has_next = i + 1 < na_ref[0]
    has_prev = i >= 1
    slot = i % 2

    def gather_row(src_ref, buf, row):
        pltpu.make_async_copy(h_hbm.at[pl.ds(src_ref[0, row], 1)], gbuf.at[buf, pl.ds(row, 1)], gsem.at[buf]).start()

    def scatter_row(dst_ref, buf, row):
        pltpu.make_async_copy(obuf.at[buf, pl.ds(row, 1)], y_hbm.at[pl.ds(dst_ref[0, row], 1)], osem.at[buf]).start()

    def scatter_wait(buf):
        pltpu.make_async_copy(obuf.at[buf], y_hbm.at[pl.ds(0, MOE_T)], osem.at[buf]).wait()

    @pl.when((i == 0) & (f == 0))
    def _():
        def body(r, carry):
            gather_row(src0_ref, 0, r)
            return carry
        lax.fori_loop(0, MOE_T, body, 0)

    @pl.when(active & (f == 0))
    def _():
        pltpu.make_async_copy(h_hbm.at[pl.ds(0, MOE_T)], gbuf.at[slot], gsem.at[slot]).wait()
        hs_ref[...] = gbuf[slot].astype(BF16)
        acc_ref[...] = jnp.zeros_like(acc_ref)

    def step(prefetch, flush, compute):
        for r in range(MOE_CH):
            row = f * MOE_CH + r
            if prefetch:
                gather_row(srcn_ref, 1 - slot, row)
            if flush:
                scatter_row(dstp_ref, 1 - slot, row)
        if compute:
            h = hs_ref[...]
            a = _dot(h, w1_ref[...].astype(BF16))
            t = a * jax.nn.sigmoid(a) * _dot(h, w3_ref[...].astype(BF16))
            acc_ref[...] += _dot(t.astype(BF16), w2_ref[...].astype(BF16))

    @pl.when(has_next & has_prev)
    def _():
        step(True, True, True)

    @pl.when(has_next & jnp.logical_not(has_prev))
    def _():
        step(True, False, True)

    @pl.when(active & jnp.logical_not(has_next) & has_prev)
    def _():
        step(False, True, True)

    @pl.when(jnp.logical_not(active) & has_prev)
    def _():
        step(False, True, False)

    @pl.when(f == MOE_NF - 1)
    def _():
        @pl.when(i >= 2)
        def _():
            scatter_wait(slot)
        obuf[slot] = jnp.where(active, acc_ref[...], 0.0)

        @pl.when(i == nt - 1)
        def _():
            def body(r, carry):
                scatter_row(dstl_ref, slot, r)
                return carry
            lax.fori_loop(0, MOE_T, body, 0)
            scatter_wait(1 - slot)
            scatter_wait(slot)


def _moe_experts(h, tile_expert, n_active, src, dst, w1, w3, w2, layer):
    r = h.shape[0]
    nt = _moe_num_tiles(r)
    hidden = lambda i, f, te, na: jnp.where(i < na[0], f, MOE_NF - 1)
    table = lambda idx: pl.BlockSpec((None, 1, MOE_T), lambda i, f, te, na: (idx(i), 0, 0), memory_space=pltpu.SMEM)
    grid_spec = pltpu.PrefetchScalarGridSpec(
        num_scalar_prefetch=2,
        grid=(nt, MOE_NF),
        in_specs=[
            table(lambda i: 0),
            table(lambda i: jnp.minimum(i + 1, nt - 1)),
            table(lambda i: jnp.maximum(i - 1, 0)),
            table(lambda i: nt - 1),
            pl.BlockSpec(memory_space=pl.ANY),
            pl.BlockSpec((None, None, D, MOE_TF), lambda i, f, te, na: (layer, te[i], 0, hidden(i, f, te, na))),
            pl.BlockSpec((None, None, D, MOE_TF), lambda i, f, te, na: (layer, te[i], 0, hidden(i, f, te, na))),
            pl.BlockSpec((None, None, MOE_TF, D), lambda i, f, te, na: (layer, te[i], hidden(i, f, te, na), 0)),
        ],
        out_specs=pl.BlockSpec(memory_space=pl.ANY),
        scratch_shapes=[pltpu.VMEM((2, MOE_T, D), F32), pltpu.VMEM((MOE_T, D), BF16),
                        pltpu.VMEM((MOE_T, D), F32), pltpu.VMEM((2, MOE_T, D), F32),
                        pltpu.SemaphoreType.DMA((2,)), pltpu.SemaphoreType.DMA((2,))],
    )
    return pl.pallas_call(
        _moe_kernel,
        grid_spec=grid_spec,
        out_shape=jax.ShapeDtypeStruct((nt * MOE_T, D), F32),
        compiler_params=_params("arbitrary", "arbitrary"),
        name="moe_experts",
    )(tile_expert, n_active, src, src, dst, dst, h, w1, w3, w2)


FFN_TILES_PER_BATCH = ROWS // FFN_TM


def _ffn_kernel(x_ref, g_ref, mx_ref, mc_ref, w1_ref, w3_ref, w2_ref, o_ref, h_ref, acc_ref):
    f = pl.program_id(1)
    row = lax.broadcasted_iota(jnp.int32, (FFN_TM, 1), 0) + (pl.program_id(0) % FFN_TILES_PER_BATCH) * FFN_TM
    is_ctx = row < CTX

    def mod(j):
        return jnp.where(is_ctx, mc_ref[:, j * D:(j + 1) * D], mx_ref[:, j * D:(j + 1) * D])

    @pl.when(f == 0)
    def _():
        h_ref[...] = _norm_mod(x_ref[...], g_ref[...], mod(3), mod(4)).astype(BF16)
        acc_ref[...] = jnp.zeros_like(acc_ref)

    h = h_ref[...]
    a = _dot(h, w1_ref[...])
    t = a * jax.nn.sigmoid(a) * _dot(h, w3_ref[...])
    acc_ref[...] += _dot(t.astype(BF16), w2_ref[...])

    @pl.when(f == pl.num_programs(1) - 1)
    def _():
        o_ref[...] = x_ref[...] + mod(5) * acc_ref[...]


def _ffn(x, g, mod, w1, w3, w2, layer, bsz):
    r = x.shape[0]
    row = pl.BlockSpec((FFN_TM, D), lambda i, f: (i, 0))
    return pl.pallas_call(
        _ffn_kernel,
        grid=(r // FFN_TM, D_FF // FFN_TF),
        in_specs=[row,
                  pl.BlockSpec((1, D), lambda i, f: (0, 0)),
                  pl.BlockSpec((None, 1, 6 * D), lambda i, f: (i // FFN_TILES_PER_BATCH, 0, 0)),
                  pl.BlockSpec((None, 1, 6 * D), lambda i, f: (bsz, 0, 0)),
                  pl.BlockSpec((None, D, FFN_TF), lambda i, f: (layer, 0, f)),
                  pl.BlockSpec((None, D, FFN_TF), lambda i, f: (layer, 0, f)),
                  pl.BlockSpec((None, FFN_TF, D), lambda i, f: (layer, f, 0))],
        out_specs=row,
        out_shape=jax.ShapeDtypeStruct((r, D), F32),
        scratch_shapes=[pltpu.VMEM((FFN_TM, D), BF16), pltpu.VMEM((FFN_TM, D), F32)],
        compiler_params=_params("parallel", "arbitrary"),
        name="ffn",
    )(x, g, mod, mod, w1, w3, w2)


def _final_kernel(moe_in, *refs):
    if moe_in:
        x_ref, y0_ref, y1_ref, p_ref, gm_ref, g_ref, o_ref = refs
        x = _moe_residual(x_ref, y0_ref, y1_ref, p_ref, gm_ref)
    else:
        x_ref, g_ref, o_ref = refs
        x = x_ref[...]
    o_ref[...] = x * lax.rsqrt(jnp.mean(x * x, axis=-1, keepdims=True) + EPS) * g_ref[...]


def _final_norm(x, g, bsz, moe_in=None):
    r = x.shape[0]
    lat = SEQ // TM
    tile = lambda b, j: b * TILES_PER_BATCH + CTX // TM + j
    in_specs, args = [pl.BlockSpec((TM, D), lambda b, j: (tile(b, j), 0))], [x]
    if moe_in is not None:
        y, top_p, prev_mod = moe_in
        in_specs += _moe_in_specs(r, tile, pl.BlockSpec((None, 1, D), lambda b, j: (b, 0, 5)))
        args += [y, y, top_p, prev_mod]
    in_specs.append(pl.BlockSpec((1, D), lambda b, j: (0, 0)))
    args.append(g)
    return pl.pallas_call(
        functools.partial(_final_kernel, moe_in is not None),
        grid=(bsz, lat),
        in_specs=in_specs,
        out_specs=pl.BlockSpec((TM, D), lambda b, j: (b * lat + j, 0)),
        out_shape=jax.ShapeDtypeStruct((bsz * SEQ, D), F32),
        compiler_params=_params("parallel", "parallel"),
        name="final_norm",
    )(*args)


D_IN = 6944
SRC_VG = 2816
SRC_AG = SRC_VG + 2 * GV
SRC_GATE = SRC_AG + 2 * GLA_RANK
LAYOUT_TM = 256


def _w_in_layout_kernel(w_ref, o_ref):
    o_ref[:, :C_AG] = w_ref[:, :SRC_VG].astype(BF16)
    o_ref[:, C_AG:C_AG + AG_W] = jnp.zeros((LAYOUT_TM, AG_W), BF16)
    o_ref[:, C_AG:C_AG + 2 * GLA_RANK] = w_ref[:, SRC_AG:SRC_GATE].astype(BF16)
    o_ref[:, C_AG + AG_W:C_VG] = jnp.zeros((LAYOUT_TM, C_VG - C_AG - AG_W), BF16)
    o_ref[:, C_VG:C_GATE] = w_ref[:, SRC_VG:SRC_AG].astype(BF16)
    o_ref[:, C_GATE:] = w_ref[:, SRC_GATE:].astype(BF16)


def _layout_w_in(w_in):
    depth = w_in.shape[0]
    assert w_in.shape[1:] == (D, D_IN)
    return pl.pallas_call(
        _w_in_layout_kernel,
        grid=(depth, D // LAYOUT_TM),
        in_specs=[pl.BlockSpec((None, LAYOUT_TM, D_IN), lambda l, i: (l, i, 0))],
        out_specs=pl.BlockSpec((None, LAYOUT_TM, DP), lambda l, i: (l, i, 0)),
        out_shape=jax.ShapeDtypeStruct((depth, D, DP), BF16),
        compiler_params=_params("parallel", "parallel"),
        name="w_in_layout",
    )(w_in)


def _layout_gla_w2(gla_w2):
    depth = gla_w2.shape[0]
    out = jnp.zeros((depth, 2, AG_W, GQ), F32)
    for d in range(2):
        out = out.at[:, d, d * GLA_RANK:(d + 1) * GLA_RANK, :].set(gla_w2[:, d])
    return out.astype(BF16)


def kernel(x, c, ctx, c_ctx, w_mod, b_mod, norm1_g, norm2_g, w_in, conv_w, attn_sink, gla_w2, gla_b,
           gla_norm_g, w_branch, w_out, ffn_w1, ffn_w3, ffn_w2, router_w, moe_w1, moe_w3, moe_w2,
           final_norm_g):
    bsz = x.shape[0]
    depth = w_in.shape[0]
    assert x.shape[1:] == (SEQ, D) and ctx.shape[1:] == (CTX, D)

    nrow = -(-(bsz + 1) // 8) * 8
    cond = jnp.zeros((nrow, D), F32).at[:bsz].set(c).at[bsz].set(c_ctx)
    mods = _modulation(cond, w_mod, b_mod).reshape(depth, nrow, 1, 6 * D)

    xs = jnp.concatenate([ctx, x], axis=1).reshape(bsz * ROWS, D)
    w_in_p = _layout_w_in(w_in)
    w2p = _layout_gla_w2(gla_w2)
    cos, sin = _rope_tables()
    router_p = jnp.pad(router_w, ((0, 0), (0, 0), (0, ROUTER_W - N_EXPERTS)))

    wb, wo = w_branch.astype(BF16), w_out.astype(BF16)
    f1, f3, f2 = ffn_w1.astype(BF16), ffn_w3.astype(BF16), ffn_w2.astype(BF16)

    moe_in = None
    for l in range(depth):
        mod = mods[l]
        outs = _proj_in(xs, norm1_g[l].reshape(1, D), mod, cos, sin, w_in_p, l, bsz, moe_in)
        p, qr, kt = outs[:3]
        if moe_in is not None:
            xs, moe_in = outs[3], None
        ya = _conv(p, conv_w[l], bsz)
        yb = _attention(p, qr, kt, attn_sink[l], bsz)
        yc = _gla(p, w2p[l], gla_b[l].reshape(2, 1, GQ), gla_norm_g[l].reshape(1, GV), bsz)
        xs = _merge(ya, yb, yc, p, xs, mod, wb, wo, l, bsz)
        g2 = norm2_g[l].reshape(1, D)
        j = l // 2
        if l % 2 == 0:
            xs = _ffn(xs, g2, mod, f1, f3, f2, j, bsz)
        else:
            logits, h2 = _router(xs, g2, mod, router_p[j], bsz)
            tile_expert, n_active, src, dst, top_p = _route(logits[:, :N_EXPERTS], xs.shape[0])
            y = _moe_experts(h2, tile_expert, n_active, src, dst, moe_w1, moe_w3, moe_w2, j)
            moe_in = (y, top_p, mod)
    out = _final_norm(xs, final_norm_g.reshape(1, D), bsz, moe_in)
    return out.reshape(bsz, SEQ, D)
```

```python
import functools

import jax
import jax.numpy as jnp
from jax import lax
from jax.experimental import pallas as pl
from jax.experimental.pallas import tpu as pltpu

F32 = jnp.float32
BF16 = jnp.bfloat16

D = 1024
SEQ = 2048
CTX = 256
ROWS = CTX + SEQ
GRID_W = 64
EPS = 1e-6
NEG_INF = -1e30
LOG2E = 1.4426950408889634

BRANCH_W = 512
ATT_HEADS = 8
ATT_KV = 2
ATT_GROUP = ATT_HEADS // ATT_KV
HEAD_DIM = 64
QB = 128
ROPE_BASE = 10000.0
GLA_HEADS = 4
GLA_DK = 64
GLA_DV = 128
GLA_RANK = 16
GLA_TAU = 16.0
GLA_CHUNK = 64
D_FF = 3584
N_EXPERTS = 8
TOP_K = 2

C_CB, C_CC, C_CU = 0, 512, 1024
C_QA, C_KA, C_VA = 1536, 2048, 2176
C_QG, C_KG, C_AG = 2304, 2560, 2816
C_VG, C_RG, C_GATE = 3072, 3584, 4096
DP = 7168
AG_W = 128

TM = 256
TILES_PER_BATCH = ROWS // TM
PROJ_TN = 1024
FFN_TM = 768
FFN_TF = 512
VMEM_LIMIT = 56 * 1024 * 1024


def _params(*sem):
    return pltpu.CompilerParams(dimension_semantics=sem, vmem_limit_bytes=VMEM_LIMIT)


def _dot(a, b):
    return jnp.dot(a, b, preferred_element_type=F32)


def _dot_nt(a, b):
    return lax.dot_general(a, b, (((1,), (1,)), ((), ())), preferred_element_type=F32)


def _dot_tn(a, b):
    return lax.dot_general(a, b, (((0,), (0,)), ((), ())), preferred_element_type=F32)


def _norm_mod(x, g, shift, scale):
    h = x * lax.rsqrt(jnp.mean(x * x, axis=-1, keepdims=True) + EPS) * g
    return h * (1.0 + scale) + shift


def _mod_kernel(s_ref, w_ref, b_ref, o_ref):
    s = s_ref[...]
    s = s * jax.nn.sigmoid(s)
    o_ref[...] = _dot(s.astype(BF16), w_ref[...].astype(BF16)) + b_ref[...]


def _modulation(cond, w_mod, b_mod):
    depth = w_mod.shape[0]
    nrow = cond.shape[0]
    return pl.pallas_call(
        _mod_kernel,
        grid=(depth, 6),
        in_specs=[
            pl.BlockSpec((nrow, D), lambda l, j: (0, 0)),
            pl.BlockSpec((None, D, D), lambda l, j: (l, 0, j)),
            pl.BlockSpec((None, 1, D), lambda l, j: (l, 0, j)),
        ],
        out_specs=pl.BlockSpec((None, nrow, D), lambda l, j: (l, 0, j)),
        out_shape=jax.ShapeDtypeStruct((depth, nrow, 6 * D), F32),
        compiler_params=_params("parallel", "parallel"),
        name="modulation",
    )(cond, w_mod, b_mod.reshape(depth, 1, 6 * D))


def _mod_row(i, bsz):
    return jnp.where(i % TILES_PER_BATCH == 0, bsz, i // TILES_PER_BATCH)


def _mod_spec(j, bsz):
    return pl.BlockSpec((None, 1, D), lambda i: (_mod_row(i, bsz), 0, j))


def _rope(x, cos, sin):
    w = x.shape[-1]
    lane = lax.broadcasted_iota(jnp.int32, x.shape, 1)
    partner = jnp.where((lane % 32) < 16, pltpu.roll(x, w - 16, 1), pltpu.roll(x, 16, 1))
    return x * cos + partner * sin


AQ = ATT_HEADS * HEAD_DIM
AK = ATT_KV * HEAD_DIM


def _proj_in_kernel(moe_in, *refs):
    if moe_in:
        (pos0_ref, posn_ref, y_hbm, x_ref, p_ref, gm_ref, g_ref, sh_ref, sc_ref, cos_ref, sin_ref, w_ref,
         o_ref, q_ref, kt_ref, xo_ref, ybuf, sem) = refs
        i = pl.program_id(0)
        buf = i % 2

        @pl.when(i == 0)
        def _():
            def body(r, carry):
                _expert_rows_start(pos0_ref, y_hbm, ybuf, sem, 0, r)
                return carry
            lax.fori_loop(0, TM, body, 0)

        _expert_rows_wait(y_hbm, ybuf, sem, buf)
        x = _moe_residual(x_ref, ybuf, buf, p_ref, gm_ref)
        xo_ref[...] = x
        for r in range(TM):
            _expert_rows_start(posn_ref, y_hbm, ybuf, sem, 1 - buf, r)
    else:
        x_ref, g_ref, sh_ref, sc_ref, cos_ref, sin_ref, w_ref, o_ref, q_ref, kt_ref = refs
        x = x_ref[...]
    h = _norm_mod(x, g_ref[...], sh_ref[...], sc_ref[...]).astype(BF16)
    cos, sin = cos_ref[...], sin_ref[...]
    for n0 in range(0, DP, PROJ_TN):
        r = _dot(h, w_ref[:, n0:n0 + PROJ_TN])
        o_ref[:, n0:n0 + PROJ_TN] = r.astype(BF16)
        if n0 <= C_QA < n0 + PROJ_TN:
            q = _rope(r[:, C_QA - n0:C_QA - n0 + AQ], jnp.tile(cos, (1, AQ // AK)), jnp.tile(sin, (1, AQ // AK)))
            q_ref[...] = (q * (HEAD_DIM ** -0.5 * LOG2E)).astype(BF16)
        if n0 <= C_KA < n0 + PROJ_TN:
            kt_ref[...] = _rope(r[:, C_KA - n0:C_KA - n0 + AK], cos, sin).T.astype(BF16)
    if moe_in:
        @pl.when(i == pl.num_programs(0) - 1)
        def _():
            _expert_rows_wait(y_hbm, ybuf, sem, 1 - buf)


def _moe_in_specs(n_tiles, tile, gate_spec):
    pos = lambda idx: pl.BlockSpec((None, 1, TOP_K * TM), lambda *ix: (idx(*ix), 0, 0), memory_space=pltpu.SMEM)
    return ([pos(lambda *ix: 0), pos(lambda *ix: jnp.minimum(tile(*ix) + 1, n_tiles - 1)),
             pl.BlockSpec(memory_space=pl.ANY)],
            [pl.BlockSpec((TM, TOP_K), lambda *ix: (tile(*ix), 0)), gate_spec])


def _moe_in_scratch():
    return [pltpu.VMEM((2, TOP_K * TM, D), F32), pltpu.SemaphoreType.DMA((2,))]


def _proj_in(x, g, mod, cos, sin, w, layer, bsz, moe_in=None):
    r = x.shape[0]
    nt = r // TM
    table = pl.BlockSpec((TM, AK), lambda i: (i % TILES_PER_BATCH, 0))
    row = pl.BlockSpec((TM, D), lambda i: (i, 0))
    in_specs, args, scratch = [row], [x], []
    out_specs = [pl.BlockSpec((TM, DP), lambda i: (i, 0)),
                 pl.BlockSpec((TM, AQ), lambda i: (i, 0)),
                 pl.BlockSpec((AK, TM), lambda i: (0, i))]
    out_shape = [jax.ShapeDtypeStruct((r, DP), BF16), jax.ShapeDtypeStruct((r, AQ), BF16),
                 jax.ShapeDtypeStruct((AK, r), BF16)]
    if moe_in is not None:
        y, pos, top_p, prev_mod = moe_in
        pos = pos.reshape(nt, 1, TOP_K * TM)
        head, tail = _moe_in_specs(nt, lambda i: i, _mod_spec(5, bsz))
        in_specs, args = head + in_specs + tail, [pos, pos, y, x, top_p, prev_mod]
        out_specs.append(row)
        out_shape.append(jax.ShapeDtypeStruct((r, D), F32))
        scratch = _moe_in_scratch()
    in_specs += [pl.BlockSpec((1, D), lambda i: (0, 0)), _mod_spec(0, bsz), _mod_spec(1, bsz), table, table,
                 pl.BlockSpec((None, D, DP), lambda i: (layer, 0, 0), pipeline_mode=pl.Buffered(1))]
    args += [g, mod, mod, cos, sin, w]
    return pl.pallas_call(
        functools.partial(_proj_in_kernel, moe_in is not None),
        grid=(nt,),
        in_specs=in_specs,
        out_specs=out_specs,
        out_shape=out_shape,
        scratch_shapes=scratch,
        compiler_params=_params("arbitrary" if moe_in is not None else "parallel"),
        name="proj_in",
    )(*args)


CONV_W = 256


def _conv_kernel(cb_ref, cc_ref, cu_ref, w_ref, o_ref):
    z = cc_ref[...].astype(F32) * cu_ref[...].astype(F32)
    row = lax.broadcasted_iota(jnp.int32, z.shape, 0)
    z_prev = jnp.where((row == 0) | (row == CTX), 0.0, pltpu.roll(z, 1, 0))
    z_next = jnp.where((row == CTX - 1) | (row == ROWS - 1), 0.0, pltpu.roll(z, ROWS - 1, 0))
    w = w_ref[...]
    y = z_prev * w[0:1] + z * w[1:2] + z_next * w[2:3]
    o_ref[...] = (cb_ref[...].astype(F32) * y).astype(BF16)


def _conv(p, conv_w, bsz):
    r = p.shape[0]
    blk = lambda c: pl.BlockSpec((ROWS, CONV_W), lambda b, j: (b, c // CONV_W + j))
    return pl.pallas_call(
        _conv_kernel,
        grid=(bsz, BRANCH_W // CONV_W),
        in_specs=[blk(C_CB), blk(C_CC), blk(C_CU), pl.BlockSpec((3, CONV_W), lambda b, j: (0, j))],
        out_specs=pl.BlockSpec((ROWS, CONV_W), lambda b, j: (b, j)),
        out_shape=jax.ShapeDtypeStruct((r, BRANCH_W), BF16),
        compiler_params=_params("parallel", "parallel"),
        name="short_conv",
    )(p, p, p, conv_w)


NQB = ROWS // QB
NCB = CTX // QB
NLK = 3 * QB


def _attn_kernel(sink_ref, q_ref, kp_ref, ko_ref, kn_ref, kc_ref, vp_ref, vo_ref, vn_ref, vc_ref, o_ref):
    n = pl.program_id(1)
    is_ctx = n < NCB
    qr = q_ref[...]
    kt = jnp.concatenate([kp_ref[...], ko_ref[...], kn_ref[...], kc_ref[...]], axis=1)
    vcat = jnp.concatenate([vp_ref[...], vo_ref[...], vn_ref[...], vc_ref[...]], axis=0)
    i = lax.broadcasted_iota(jnp.int32, (ATT_GROUP * QB, QB), 0) % QB
    j = lax.broadcasted_iota(jnp.int32, (ATT_GROUP * QB, QB), 1)
    far = 4 * QB
    pen_p = jnp.where(n > NCB, 0, far)
    pen_n = jnp.where((n >= NCB) & (n < NQB - 1), 0, far)
    mask_p = j >= i + pen_p
    mask_n = j + pen_n <= i
    outs = [None] * ATT_HEADS
    for kv in range(ATT_KV):
        heads = range(kv * ATT_GROUP, (kv + 1) * ATT_GROUP)
        qg = jnp.concatenate([qr[:, h * HEAD_DIM:(h + 1) * HEAD_DIM] for h in heads], axis=0)
        sink = jnp.concatenate([jnp.full((QB, 1), sink_ref[h] * LOG2E, F32) for h in heads], axis=0)
        s = _dot(qg, kt[kv * HEAD_DIM:(kv + 1) * HEAD_DIM])
        tiles = [jnp.where(mask_p, s[:, :QB], NEG_INF),
                 jnp.where(is_ctx, NEG_INF, s[:, QB:2 * QB]),
                 jnp.where(mask_n, s[:, 2 * QB:NLK], NEG_INF)]
        tiles += [s[:, c:c + QB] for c in range(NLK, NLK + CTX, QB)]
        m = tiles[0]
        for t in tiles[1:]:
            m = jnp.maximum(m, t)
        m = jnp.maximum(jnp.max(m, axis=-1, keepdims=True), sink)
        tiles = [jnp.exp2(t - m) for t in tiles]
        acc = tiles[0]
        for t in tiles[1:]:
            acc = acc + t
        den = jnp.sum(acc, axis=-1, keepdims=True) + jnp.exp2(sink - m)
        p = jnp.concatenate(tiles, axis=1).astype(BF16)
        o = _dot(p, vcat[:, kv * HEAD_DIM:(kv + 1) * HEAD_DIM]) / den
        for g, h in enumerate(heads):
            outs[h] = o[g * QB:(g + 1) * QB]
    o_ref[...] = jnp.concatenate(outs, axis=1).astype(BF16)


def _rope_tables():
    half = HEAD_DIM // 2
    t = jnp.arange(SEQ)
    row = (t // GRID_W).astype(F32)
    col = (t % GRID_W).astype(F32)
    inv_freq = ROPE_BASE ** (-jnp.arange(0, half, 2, dtype=F32) / half)
    ar = row[:, None] * inv_freq
    ac = col[:, None] * inv_freq
    cos = jnp.concatenate([jnp.cos(ar), jnp.cos(ar), jnp.cos(ac), jnp.cos(ac)], axis=1)
    sin = jnp.concatenate([-jnp.sin(ar), jnp.sin(ar), -jnp.sin(ac), jnp.sin(ac)], axis=1)
    cos = jnp.concatenate([jnp.ones((CTX, HEAD_DIM), F32), cos], axis=0)
    sin = jnp.concatenate([jnp.zeros((CTX, HEAD_DIM), F32), sin], axis=0)
    return jnp.tile(cos, (1, ATT_KV)), jnp.tile(sin, (1, ATT_KV))


def _attention(p, qr, kt, sink, bsz):
    r = p.shape[0]
    lo, hi = NCB, NQB - 1
    near = lambda b, n, shift: b * NQB + jnp.clip(n + shift, lo, hi)
    kblk = lambda shift: pl.BlockSpec((AK, QB), lambda b, n: (0, near(b, n, shift)))
    vblk = lambda shift: pl.BlockSpec((QB, AK), lambda b, n: (near(b, n, shift), C_VA // AK))
    return pl.pallas_call(
        _attn_kernel,
        grid=(bsz, NQB),
        in_specs=[
            pl.BlockSpec(memory_space=pltpu.SMEM),
            pl.BlockSpec((QB, AQ), lambda b, n: (b * NQB + n, 0)),
            kblk(-1), kblk(0), kblk(1),
            pl.BlockSpec((AK, CTX), lambda b, n: (0, b * (ROWS // CTX))),
            vblk(-1), vblk(0), vblk(1),
            pl.BlockSpec((CTX, AK), lambda b, n: (b * (ROWS // CTX), C_VA // AK)),
        ],
        out_specs=pl.BlockSpec((QB, AQ), lambda b, n: (b * NQB + n, 0)),
        out_shape=jax.ShapeDtypeStruct((r, AQ), BF16),
        compiler_params=_params("parallel", "parallel"),
        name="window_attention",
    )(sink, qr, kt, kt, kt, kt, p, p, p, p)


NCHUNK = ROWS // GLA_CHUNK
NCHUNK_CTX = CTX // GLA_CHUNK
GQ = GLA_HEADS * GLA_DK
GV = GLA_HEADS * GLA_DV
GLA_BLK = 256


def _log_sigmoid(z):
    return jnp.minimum(z, 0.0) - jnp.log(1.0 + jnp.exp(-jnp.abs(z)))


def _gla_kernel(q_ref, k_ref, a_ref, v_ref, r_ref, w2_ref, b2_ref, g_ref, o_ref,
                of_ref, ob_ref, qd_ref, ke_ref, dec_ref, sf_ref, sb_ref):
    ri = lax.broadcasted_iota(jnp.int32, (GLA_BLK, GLA_BLK), 0)
    rj = lax.broadcasted_iota(jnp.int32, (GLA_BLK, GLA_BLK), 1)
    same_chunk = (ri // GLA_CHUNK) == (rj // GLA_CHUNK)
    causal = (same_chunk & (ri >= rj), same_chunk & (ri <= rj))
    lane_head = lax.broadcasted_iota(jnp.int32, (GLA_BLK, GQ), 1) // GLA_DK
    outs = (of_ref, ob_ref)

    def precompute(blk, carry):
        rows = pl.ds(pl.multiple_of(blk * GLA_BLK, GLA_BLK), GLA_BLK)
        a = a_ref[rows, :]
        q = q_ref[rows, :].astype(F32) * (GLA_DK ** -0.5)
        k = k_ref[rows, :].astype(F32)
        v = v_ref[rows, :]
        for d in range(2):
            la = _log_sigmoid(_dot(a, w2_ref[d]) + b2_ref[d]) * (1.0 / GLA_TAU)
            sel = jnp.concatenate([causal[d], same_chunk], axis=0).astype(BF16)
            hi = la.astype(BF16)
            r1 = la - hi.astype(F32)
            mid = r1.astype(BF16)
            lo = (r1 - mid.astype(F32)).astype(BF16)
            sums = _dot(sel, jnp.concatenate([hi, mid, lo], axis=1))
            sums = sums[:, :GQ] + sums[:, GQ:2 * GQ] + sums[:, 2 * GQ:]
            b, tot = sums[:GLA_BLK], sums[GLA_BLK:]
            q_dec = q * jnp.exp(b)
            k_inv = (k * jnp.exp(-b)).astype(BF16)
            qd_ref[d, rows, :] = q_dec.astype(BF16)
            ke_ref[d, rows, :] = (k * jnp.exp(tot - b)).astype(BF16)
            decay = jnp.exp(tot)
            for j in range(GLA_BLK // GLA_CHUNK):
                dec_ref[d, pl.ds(blk * (GLA_BLK // GLA_CHUNK) + j, 1), :] = decay[j * GLA_CHUNK:j * GLA_CHUNK + 1]
            intra = []
            for h in range(GLA_HEADS):
                qh = jnp.where(lane_head == h, q_dec, 0.0).astype(BF16)
                att = jnp.where(causal[d], _dot_nt(qh, k_inv), 0.0)
                intra.append(_dot(att.astype(BF16), v[:, h * GLA_DV:(h + 1) * GLA_DV]))
            outs[d][rows, :] = jnp.concatenate(intra, axis=1)
        return carry

    lax.fori_loop(0, ROWS // GLA_BLK, precompute, 0, unroll=3)

    sf_ref[...] = jnp.zeros_like(sf_ref)
    sb_ref[...] = jnp.zeros_like(sb_ref)
    bi = lax.broadcasted_iota(jnp.int32, (GV, GQ), 0)
    bj = lax.broadcasted_iota(jnp.int32, (GV, GQ), 1)
    block_diag = (bi // GLA_DV) == (bj // GLA_DK)

    def scan(i, carry):
        order = (i, jnp.where(i < NCHUNK_CTX, NCHUNK_CTX - 1 - i, NCHUNK + NCHUNK_CTX - 1 - i))
        for d, s_ref in enumerate((sf_ref, sb_ref)):
            c = order[d]
            rows = pl.ds(pl.multiple_of(c * GLA_CHUNK, GLA_CHUNK), GLA_CHUNK)
            s = s_ref[...]
            outs[d][rows, :] += _dot_nt(qd_ref[d, rows, :], s.astype(BF16))
            ds = _dot_tn(v_ref[rows, :], ke_ref[d, rows, :])
            s_ref[...] = s * dec_ref[d, pl.ds(c, 1), :] + jnp.where(block_diag, ds, 0.0)
        return carry

    lax.fori_loop(0, NCHUNK, scan, 0, unroll=2)

    def finish(t, carry):
        rows = pl.ds(pl.multiple_of(t * TM, TM), TM)
        o = of_ref[rows, :] + ob_ref[rows, :]
        parts = []
        for h in range(GLA_HEADS):
            oh = o[:, h * GLA_DV:(h + 1) * GLA_DV]
            parts.append(oh * lax.rsqrt(jnp.mean(oh * oh, axis=-1, keepdims=True) + EPS))
        on = jnp.concatenate(parts, axis=1) * g_ref[...]
        rg = r_ref[rows, :].astype(F32)
        o_ref[rows, :] = (on * (rg * jax.nn.sigmoid(rg))).astype(BF16)
        return carry

    lax.fori_loop(0, ROWS // TM, finish, 0)


def _gla(p, w2p, b2, g, bsz):
    r = p.shape[0]
    blk = lambda c, w: pl.BlockSpec((ROWS, w), lambda b: (b, c // w))
    return pl.pallas_call(
        _gla_kernel,
        grid=(bsz,),
        in_specs=[
            blk(C_QG, GQ), blk(C_KG, GQ), blk(C_AG, AG_W), blk(C_VG, GV), blk(C_RG, GV),
            pl.BlockSpec((2, AG_W, GQ), lambda b: (0, 0, 0)),
            pl.BlockSpec((2, 1, GQ), lambda b: (0, 0, 0)),
            pl.BlockSpec((1, GV), lambda b: (0, 0)),
        ],
        out_specs=pl.BlockSpec((ROWS, GV), lambda b: (b, 0)),
        out_shape=jax.ShapeDtypeStruct((r, GV), BF16),
        scratch_shapes=[
            pltpu.VMEM((ROWS, GV), F32), pltpu.VMEM((ROWS, GV), F32),
            pltpu.VMEM((2, ROWS, GQ), BF16), pltpu.VMEM((2, ROWS, GQ), BF16),
            pltpu.VMEM((2, NCHUNK, GQ), F32),
            pltpu.VMEM((GV, GQ), F32), pltpu.VMEM((GV, GQ), F32),
        ],
        compiler_params=_params("parallel"),
        name="gla",
    )(p, p, p, p, p, w2p, b2, g)


def _merge_kernel(ya_ref, yb_ref, yc_ref, g0_ref, g1_ref, g2_ref, x_ref, gm_ref, wb_ref, wo_ref, o_ref):
    acc = jax.nn.sigmoid(g0_ref[...].astype(F32)) * _dot(ya_ref[...], wb_ref[0])
    acc += jax.nn.sigmoid(g1_ref[...].astype(F32)) * _dot(yb_ref[...], wb_ref[1])
    acc += jax.nn.sigmoid(g2_ref[...].astype(F32)) * _dot(yc_ref[...], wb_ref[2])
    y = _dot(acc.astype(BF16), wo_ref[...])
    o_ref[...] = x_ref[...] + gm_ref[...] * y


def _merge(ya, yb, yc, p, x, mod, wb, wo, layer, bsz):
    r = x.shape[0]
    br = pl.BlockSpec((TM, BRANCH_W), lambda i: (i, 0))
    gate = lambda k: pl.BlockSpec((TM, D), lambda i: (i, C_GATE // D + k))
    return pl.pallas_call(
        _merge_kernel,
        grid=(r // TM,),
        in_specs=[br, br, br, gate(0), gate(1), gate(2),
                  pl.BlockSpec((TM, D), lambda i: (i, 0)),
                  _mod_spec(2, bsz),
                  pl.BlockSpec((None, 3, BRANCH_W, D), lambda i: (layer, 0, 0, 0), pipeline_mode=pl.Buffered(1)),
                  pl.BlockSpec((None, D, D), lambda i: (layer, 0, 0), pipeline_mode=pl.Buffered(1))],
        out_specs=pl.BlockSpec((TM, D), lambda i: (i, 0)),
        out_shape=jax.ShapeDtypeStruct((r, D), F32),
        compiler_params=_params("parallel"),
        name="merge",
    )(ya, yb, yc, p, p, p, x, mod, wb, wo)


ROUTER_W = 128


def _router_kernel(x_ref, g_ref, sh_ref, sc_ref, w_ref, o_ref, h_ref):
    h = _norm_mod(x_ref[...], g_ref[...], sh_ref[...], sc_ref[...])
    h_ref[...] = h
    o_ref[...] = jnp.dot(h, w_ref[...], preferred_element_type=F32, precision=lax.Precision.HIGHEST)


def _router(x, g, mod, w, bsz):
    r = x.shape[0]
    return pl.pallas_call(
        _router_kernel,
        grid=(r // TM,),
        in_specs=[pl.BlockSpec((TM, D), lambda i: (i, 0)),
                  pl.BlockSpec((1, D), lambda i: (0, 0)),
                  _mod_spec(3, bsz), _mod_spec(4, bsz),
                  pl.BlockSpec((D, ROUTER_W), lambda i: (0, 0))],
        out_specs=[pl.BlockSpec((TM, ROUTER_W), lambda i: (i, 0)),
                   pl.BlockSpec((TM, D), lambda i: (i, 0))],
        out_shape=[jax.ShapeDtypeStruct((r, ROUTER_W), F32), jax.ShapeDtypeStruct((r, D), F32)],
        compiler_params=_params("parallel"),
        name="router",
    )(x, g, mod, mod, w)


MOE_T = 896
MOE_TF = 512
MOE_NF = D_FF // MOE_TF
MOE_CH = MOE_T // MOE_NF
MOE_ID = 1 << 17


def _moe_num_tiles(r):
    return (TOP_K * r + N_EXPERTS * (MOE_T - 1)) // MOE_T


def _route(logits, row_of=lambda t: t):
    n = logits.shape[0]
    nt = _moe_num_tiles(n)
    top_v, top_i = lax.top_k(logits, TOP_K)
    top_p = jax.nn.softmax(top_v, axis=-1)
    na = TOP_K * n
    nfill = nt * MOE_T - na
    assert na + nfill <= MOE_ID and na > MOE_T
    e_flat = top_i.reshape(-1).astype(jnp.int32)
    onehot = (e_flat[:, None] == jnp.arange(N_EXPERTS)[None, :]).astype(jnp.int32)
    csum = jnp.cumsum(onehot, axis=0)
    counts = csum[-1]
    tiles_e = (counts + MOE_T - 1) // MOE_T
    tile_end = jnp.cumsum(tiles_e)
    n_active = tile_end[-1]
    pos = jnp.sum(onehot * ((tile_end - tiles_e)[None, :] * MOE_T + csum - 1), axis=1)
    tile_id = jnp.minimum(jnp.arange(nt), n_active - 1)
    tile_expert = jnp.sum((tile_id[:, None] >= tile_end[None, :]).astype(jnp.int32), axis=1)
    fill = jnp.arange(nfill, dtype=jnp.int32)
    pad_end = jnp.cumsum(tiles_e * MOE_T - counts)
    e_fill = jnp.sum((fill[:, None] >= pad_end[None, :]).astype(jnp.int32), axis=1)
    ident = jnp.sort(jnp.concatenate([e_flat * MOE_ID + jnp.arange(na, dtype=jnp.int32),
                                      e_fill * MOE_ID + na + fill])) % MOE_ID
    src = row_of(jnp.where(ident < na, ident // TOP_K, 0))
    return (tile_expert.astype(jnp.int32), n_active.reshape(1).astype(jnp.int32),
            src.reshape(nt, 1, MOE_T), pos.astype(jnp.int32), top_p)


def _moe_kernel(te_ref, na_ref, src0_ref, srcn_ref, h_hbm, w1_ref, w3_ref, w2_ref, o_ref,
                gbuf, hs_ref, acc_ref, sem):
    i = pl.program_id(0)
    f = pl.program_id(1)
    active = i < na_ref[0]
    has_next = i + 1 < na_ref[0]
    slot = i % 2

    def gather_row(src_ref, buf, row):
        pltpu.make_async_copy(h_hbm.at[pl.ds(src_ref[0, row], 1)], gbuf.at[buf, pl.ds(row, 1)], sem.at[buf]).start()

    @pl.when((i == 0) & (f == 0))
    def _():
        def body(r, carry):
            gather_row(src0_ref, 0, r)
            return carry
        lax.fori_loop(0, MOE_T, body, 0)

    @pl.when(active & (f == 0))
    def _():
        pltpu.make_async_copy(h_hbm.at[pl.ds(0, MOE_T)], gbuf.at[slot], sem.at[slot]).wait()
        hs_ref[...] = gbuf[slot].astype(BF16)
        acc_ref[...] = jnp.zeros_like(acc_ref)

    def step(prefetch):
        if prefetch:
            for r in range(MOE_CH):
                gather_row(srcn_ref, 1 - slot, f * MOE_CH + r)
        h = hs_ref[...]
        a = _dot(h, w1_ref[...].astype(BF16))
        t = a * jax.nn.sigmoid(a) * _dot(h, w3_ref[...].astype(BF16))
        acc_ref[...] += _dot(t.astype(BF16), w2_ref[...].astype(BF16))

    @pl.when(has_next)
    def _():
        step(True)

    @pl.when(active & jnp.logical_not(has_next))
    def _():
        step(False)

    @pl.when(f == MOE_NF - 1)
    def _():
        o_ref[...] = jnp.where(active, acc_ref[...], 0.0)


def _moe_experts(h, tile_expert, n_active, src, w1, w3, w2, layer):
    nt = src.shape[0]
    hidden = lambda i, f, te, na: jnp.where(i < na[0], f, MOE_NF - 1)
    table = lambda idx: pl.BlockSpec((None, 1, MOE_T), lambda i, f, te, na: (idx(i), 0, 0), memory_space=pltpu.SMEM)
    grid_spec = pltpu.PrefetchScalarGridSpec(
        num_scalar_prefetch=2,
        grid=(nt, MOE_NF),
        in_specs=[
            table(lambda i: 0),
            table(lambda i: jnp.minimum(i + 1, nt - 1)),
            pl.BlockSpec(memory_space=pl.ANY),
            pl.BlockSpec((None, None, D, MOE_TF), lambda i, f, te, na: (layer, te[i], 0, hidden(i, f, te, na))),
            pl.BlockSpec((None, None, D, MOE_TF), lambda i, f, te, na: (layer, te[i], 0, hidden(i, f, te, na))),
            pl.BlockSpec((None, None, MOE_TF, D), lambda i, f, te, na: (layer, te[i], hidden(i, f, te, na), 0)),
        ],
        out_specs=pl.BlockSpec((MOE_T, D), lambda i, f, te, na: (i, 0)),
        scratch_shapes=[pltpu.VMEM((2, MOE_T, D), F32), pltpu.VMEM((MOE_T, D), BF16),
                        pltpu.VMEM((MOE_T, D), F32), pltpu.SemaphoreType.DMA((2,))],
    )
    return pl.pallas_call(
        _moe_kernel,
        grid_spec=grid_spec,
        out_shape=jax.ShapeDtypeStruct((nt * MOE_T, D), F32),
        compiler_params=_params("arbitrary", "arbitrary"),
        name="moe_experts",
    )(tile_expert, n_active, src, src, h, w1, w3, w2)


def _expert_rows_start(pos_ref, y_hbm, ybuf, sem, buf, r):
    for k in range(TOP_K):
        pltpu.make_async_copy(y_hbm.at[pl.ds(pos_ref[0, TOP_K * r + k], 1)],
                              ybuf.at[buf, pl.ds(k * TM + r, 1)], sem.at[buf]).start()


def _expert_rows_wait(y_hbm, ybuf, sem, buf):
    pltpu.make_async_copy(y_hbm.at[pl.ds(0, TOP_K * TM)], ybuf.at[buf], sem.at[buf]).wait()


def _moe_residual(x_ref, ybuf, buf, p_ref, gm_ref):
    p = p_ref[...]
    y = p[:, 0:1] * ybuf[buf, 0:TM, :]
    for k in range(1, TOP_K):
        y += p[:, k:k + 1] * ybuf[buf, k * TM:(k + 1) * TM, :]
    return x_ref[...] + gm_ref[...] * y


FFN_TILES_PER_BATCH = ROWS // FFN_TM


def _ffn_kernel(x_ref, g_ref, mx_ref, mc_ref, w1_ref, w3_ref, w2_ref, o_ref, h_ref, acc_ref):
    f = pl.program_id(1)
    row = lax.broadcasted_iota(jnp.int32, (FFN_TM, 1), 0) + (pl.program_id(0) % FFN_TILES_PER_BATCH) * FFN_TM
    is_ctx = row < CTX

    def mod(j):
        return jnp.where(is_ctx, mc_ref[:, j * D:(j + 1) * D], mx_ref[:, j * D:(j + 1) * D])

    @pl.when(f == 0)
    def _():
        h_ref[...] = _norm_mod(x_ref[...], g_ref[...], mod(3), mod(4)).astype(BF16)
        acc_ref[...] = jnp.zeros_like(acc_ref)

    h = h_ref[...]
    a = _dot(h, w1_ref[...])
    t = a * jax.nn.sigmoid(a) * _dot(h, w3_ref[...])
    acc_ref[...] += _dot(t.astype(BF16), w2_ref[...])

    @pl.when(f == pl.num_programs(1) - 1)
    def _():
        o_ref[...] = x_ref[...] + mod(5) * acc_ref[...]


def _ffn(x, g, mod, w1, w3, w2, layer, bsz):
    r = x.shape[0]
    row = pl.BlockSpec((FFN_TM, D), lambda i, f: (i, 0))
    return pl.pallas_call(
        _ffn_kernel,
        grid=(r // FFN_TM, D_FF // FFN_TF),
        in_specs=[row,
                  pl.BlockSpec((1, D), lambda i, f: (0, 0)),
                  pl.BlockSpec((None, 1, 6 * D), lambda i, f: (i // FFN_TILES_PER_BATCH, 0, 0)),
                  pl.BlockSpec((None, 1, 6 * D), lambda i, f: (bsz, 0, 0)),
                  pl.BlockSpec((None, D, FFN_TF), lambda i, f: (layer, 0, f)),
                  pl.BlockSpec((None, D, FFN_TF), lambda i, f: (layer, 0, f)),
                  pl.BlockSpec((None, FFN_TF, D), lambda i, f: (layer, f, 0))],
        out_specs=row,
        out_shape=jax.ShapeDtypeStruct((r, D), F32),
        scratch_shapes=[pltpu.VMEM((FFN_TM, D), BF16), pltpu.VMEM((FFN_TM, D), F32)],
        compiler_params=_params("parallel", "arbitrary"),
        name="ffn",
    )(x, g, mod, mod, w1, w3, w2)


def _final_kernel(moe_in, *refs):
    if moe_in:
        pos0_ref, posn_ref, y_hbm, x_ref, p_ref, gm_ref, g_ref, o_ref, ybuf, sem = refs
        t = pl.program_id(0) * pl.num_programs(1) + pl.program_id(1)
        buf = t % 2

        @pl.when(t == 0)
        def _():
            def body(r, carry):
                _expert_rows_start(pos0_ref, y_hbm, ybuf, sem, 0, r)
                return carry
            lax.fori_loop(0, TM, body, 0)

        @pl.when(t + 1 < pl.num_programs(0) * pl.num_programs(1))
        def _():
            for r in range(TM):
                _expert_rows_start(posn_ref, y_hbm, ybuf, sem, 1 - buf, r)

        _expert_rows_wait(y_hbm, ybuf, sem, buf)
        x = _moe_residual(x_ref, ybuf, buf, p_ref, gm_ref)
    else:
        x_ref, g_ref, o_ref = refs
        x = x_ref[...]
    o_ref[...] = x * lax.rsqrt(jnp.mean(x * x, axis=-1, keepdims=True) + EPS) * g_ref[...]


def _final_norm(x, g, bsz, moe_in=None):
    lat = SEQ // TM
    in_specs = [pl.BlockSpec((TM, D), lambda b, j: (b * TILES_PER_BATCH + CTX // TM + j, 0))]
    args, scratch = [x], []
    if moe_in is not None:
        y, pos, top_p, prev_mod = moe_in
        pos = pos.reshape(bsz * lat, 1, TOP_K * TM)
        head, tail = _moe_in_specs(bsz * lat, lambda b, j: b * lat + j,
                                   pl.BlockSpec((None, 1, D), lambda b, j: (b, 0, 5)))
        in_specs, args = head + in_specs + tail, [pos, pos, y, x, top_p, prev_mod]
        scratch = _moe_in_scratch()
    in_specs.append(pl.BlockSpec((1, D), lambda b, j: (0, 0)))
    args.append(g)
    sem = ("arbitrary", "arbitrary") if moe_in is not None else ("parallel", "parallel")
    return pl.pallas_call(
        functools.partial(_final_kernel, moe_in is not None),
        grid=(bsz, lat),
        in_specs=in_specs,
        out_specs=pl.BlockSpec((TM, D), lambda b, j: (b * lat + j, 0)),
        out_shape=jax.ShapeDtypeStruct((bsz * SEQ, D), F32),
        scratch_shapes=scratch,
        compiler_params=_params(*sem),
        name="final_norm",
    )(*args)


D_IN = 6944
SRC_VG = 2816
SRC_AG = SRC_VG + 2 * GV
SRC_GATE = SRC_AG + 2 * GLA_RANK
LAYOUT_TM = 256


def _w_in_layout_kernel(w_ref, o_ref):
    o_ref[:, :C_AG] = w_ref[:, :SRC_VG].astype(BF16)
    o_ref[:, C_AG:C_AG + AG_W] = jnp.zeros((LAYOUT_TM, AG_W), BF16)
    o_ref[:, C_AG:C_AG + 2 * GLA_RANK] = w_ref[:, SRC_AG:SRC_GATE].astype(BF16)
    o_ref[:, C_AG + AG_W:C_VG] = jnp.zeros((LAYOUT_TM, C_VG - C_AG - AG_W), BF16)
    o_ref[:, C_VG:C_GATE] = w_ref[:, SRC_VG:SRC_AG].astype(BF16)
    o_ref[:, C_GATE:] = w_ref[:, SRC_GATE:].astype(BF16)


def _layout_w_in(w_in):
    depth = w_in.shape[0]
    assert w_in.shape[1:] == (D, D_IN)
    return pl.pallas_call(
        _w_in_layout_kernel,
        grid=(depth, D // LAYOUT_TM),
        in_specs=[pl.BlockSpec((None, LAYOUT_TM, D_IN), lambda l, i: (l, i, 0))],
        out_specs=pl.BlockSpec((None, LAYOUT_TM, DP), lambda l, i: (l, i, 0)),
        out_shape=jax.ShapeDtypeStruct((depth, D, DP), BF16),
        compiler_params=_params("parallel", "parallel"),
        name="w_in_layout",
    )(w_in)


def _layout_gla_w2(gla_w2):
    depth = gla_w2.shape[0]
    out = jnp.zeros((depth, 2, AG_W, GQ), F32)
    for d in range(2):
        out = out.at[:, d, d * GLA_RANK:(d + 1) * GLA_RANK, :].set(gla_w2[:, d])
    return out.astype(BF16)


def kernel(x, c, ctx, c_ctx, w_mod, b_mod, norm1_g, norm2_g, w_in, conv_w, attn_sink, gla_w2, gla_b,
           gla_norm_g, w_branch, w_out, ffn_w1, ffn_w3, ffn_w2, router_w, moe_w1, moe_w3, moe_w2,
           final_norm_g):
    bsz = x.shape[0]
    depth = w_in.shape[0]
    assert x.shape[1:] == (SEQ, D) and ctx.shape[1:] == (CTX, D)

    nrow = -(-(bsz + 1) // 8) * 8
    cond = jnp.zeros((nrow, D), F32).at[:bsz].set(c).at[bsz].set(c_ctx)
    mods = _modulation(cond, w_mod, b_mod).reshape(depth, nrow, 1, 6 * D)

    xs = jnp.concatenate([ctx, x], axis=1).reshape(bsz * ROWS, D)
    w_in_p = _layout_w_in(w_in)
    w2p = _layout_gla_w2(gla_w2)
    cos, sin = _rope_tables()
    router_p = jnp.pad(router_w, ((0, 0), (0, 0), (0, ROUTER_W - N_EXPERTS)))

    wb, wo = w_branch.astype(BF16), w_out.astype(BF16)
    f1, f3, f2 = ffn_w1.astype(BF16), ffn_w3.astype(BF16), ffn_w2.astype(BF16)

    moe_in = None
    for l in range(depth):
        mod = mods[l]
        outs = _proj_in(xs, norm1_g[l].reshape(1, D), mod, cos, sin, w_in_p, l, bsz, moe_in)
        p, qr, kt = outs[:3]
        if moe_in is not None:
            xs, moe_in = outs[3], None
        ya = _conv(p, conv_w[l], bsz)
        yb = _attention(p, qr, kt, attn_sink[l], bsz)
        yc = _gla(p, w2p[l], gla_b[l].reshape(2, 1, GQ), gla_norm_g[l].reshape(1, GV), bsz)
        xs = _merge(ya, yb, yc, p, xs, mod, wb, wo, l, bsz)
        g2 = norm2_g[l].reshape(1, D)
        j = l // 2
        if l % 2 == 0:
            xs = _ffn(xs, g2, mod, f1, f3, f2, j, bsz)
        else:
            logits, h2 = _router(xs, g2, mod, router_p[j], bsz)
            logits = logits[:, :N_EXPERTS]
            if l == depth - 1:
                logits = logits.reshape(bsz, ROWS, N_EXPERTS)[:, CTX:].reshape(bsz * SEQ, N_EXPERTS)
                routed = _route(logits, lambda t: (t // SEQ) * ROWS + CTX + t % SEQ)
            else:
                routed = _route(logits)
            tile_expert, n_active, src, pos, top_p = routed
            y = _moe_experts(h2, tile_expert, n_active, src, moe_w1, moe_w3, moe_w2, j)
            moe_in = (y, pos, top_p, mod)
    out = _final_norm(xs, final_norm_g.reshape(1, D), bsz, moe_in)
    return out.reshape(bsz, SEQ, D)
```

```python
import functools

import jax
import jax.numpy as jnp
from jax import lax
from jax.experimental import pallas as pl
from jax.experimental.pallas import tpu as pltpu

F32 = jnp.float32
BF16 = jnp.bfloat16

D = 1024
SEQ = 2048
CTX = 256
ROWS = CTX + SEQ
GRID_W = 64
EPS = 1e-6
NEG_INF = -1e30
LOG2E = 1.4426950408889634

BRANCH_W = 512
ATT_HEADS = 8
ATT_KV = 2
ATT_GROUP = ATT_HEADS // ATT_KV
HEAD_DIM = 64
QB = 128
ROPE_BASE = 10000.0
GLA_HEADS = 4
GLA_DK = 64
GLA_DV = 128
GLA_RANK = 16
GLA_TAU = 16.0
GLA_CHUNK = 64
D_FF = 3584
N_EXPERTS = 8
TOP_K = 2

C_CB, C_CC, C_CU = 0, 512, 1024
C_QA, C_KA, C_VA = 1536, 2048, 2176
C_QG, C_KG, C_AG = 2304, 2560, 2816
C_VG, C_RG, C_GATE = 3072, 3584, 4096
DP = 7168
AG_W = 128

TM = 256
TILES_PER_BATCH = ROWS // TM
PROJ_TN = 1024
FFN_TM = 768
FFN_TF = 512
VMEM_LIMIT = 56 * 1024 * 1024


def _params(*sem):
    return pltpu.CompilerParams(dimension_semantics=sem, vmem_limit_bytes=VMEM_LIMIT)


def _dot(a, b):
    return jnp.dot(a, b, preferred_element_type=F32)


def _dot_nt(a, b):
    return lax.dot_general(a, b, (((1,), (1,)), ((), ())), preferred_element_type=F32)


def _dot_tn(a, b):
    return lax.dot_general(a, b, (((0,), (0,)), ((), ())), preferred_element_type=F32)


def _norm_mod(x, g, shift, scale):
    h = x * lax.rsqrt(jnp.mean(x * x, axis=-1, keepdims=True) + EPS) * g
    return h * (1.0 + scale) + shift


def _mod_kernel(s_ref, w_ref, b_ref, o_ref):
    s = s_ref[...]
    s = s * jax.nn.sigmoid(s)
    o_ref[...] = _dot(s.astype(BF16), w_ref[...].astype(BF16)) + b_ref[...]


def _modulation(cond, w_mod, b_mod):
    depth = w_mod.shape[0]
    nrow = cond.shape[0]
    return pl.pallas_call(
        _mod_kernel,
        grid=(depth, 6),
        in_specs=[
            pl.BlockSpec((nrow, D), lambda l, j: (0, 0)),
            pl.BlockSpec((None, D, D), lambda l, j: (l, 0, j)),
            pl.BlockSpec((None, 1, D), lambda l, j: (l, 0, j)),
        ],
        out_specs=pl.BlockSpec((None, nrow, D), lambda l, j: (l, 0, j)),
        out_shape=jax.ShapeDtypeStruct((depth, nrow, 6 * D), F32),
        compiler_params=_params("parallel", "parallel"),
        name="modulation",
    )(cond, w_mod, b_mod.reshape(depth, 1, 6 * D))


def _mod_row(i, bsz):
    return jnp.where(i % TILES_PER_BATCH == 0, bsz, i // TILES_PER_BATCH)


def _mod_spec(j, bsz):
    return pl.BlockSpec((None, 1, D), lambda i: (_mod_row(i, bsz), 0, j))


def _rope(x, cos, sin):
    w = x.shape[-1]
    lane = lax.broadcasted_iota(jnp.int32, x.shape, 1)
    partner = jnp.where((lane % 32) < 16, pltpu.roll(x, w - 16, 1), pltpu.roll(x, 16, 1))
    return x * cos + partner * sin


AQ = ATT_HEADS * HEAD_DIM
AK = ATT_KV * HEAD_DIM


def _stream_specs(x):
    if not isinstance(x, tuple):
        return [pl.BlockSpec((TM, D), lambda i: (i, 0))], [x]
    assert CTX == TM
    lat = SEQ // TM
    return ([pl.BlockSpec((TM, D), lambda i: (i // TILES_PER_BATCH, 0)),
             pl.BlockSpec((TM, D), lambda i: ((i // TILES_PER_BATCH) * lat
                                              + jnp.maximum(i % TILES_PER_BATCH - 1, 0), 0))],
            list(x))


def _stream_tile(refs):
    if len(refs) == 1:
        return refs[0][...]
    return jnp.where(pl.program_id(0) % TILES_PER_BATCH == 0, refs[0][...], refs[1][...])


def _proj_in_kernel(moe_in, *refs):
    if moe_in:
        (pos0_ref, posn_ref, y_hbm, x_ref, p_ref, gm_ref, g_ref, sh_ref, sc_ref, cos_ref, sin_ref, w_ref,
         o_ref, q_ref, kt_ref, xo_ref, ybuf, sem) = refs
        i = pl.program_id(0)
        buf = i % 2

        @pl.when(i == 0)
        def _():
            def body(r, carry):
                _expert_rows_start(pos0_ref, y_hbm, ybuf, sem, 0, r)
                return carry
            lax.fori_loop(0, TM, body, 0)

        _expert_rows_wait(y_hbm, ybuf, sem, buf)
        x = _moe_residual(x_ref, ybuf, buf, p_ref, gm_ref)
        xo_ref[...] = x
        for r in range(TM):
            _expert_rows_start(posn_ref, y_hbm, ybuf, sem, 1 - buf, r)
    else:
        *stream, g_ref, sh_ref, sc_ref, cos_ref, sin_ref, w_ref, o_ref, q_ref, kt_ref = refs
        x = _stream_tile(stream)
    h = _norm_mod(x, g_ref[...], sh_ref[...], sc_ref[...]).astype(BF16)
    cos, sin = cos_ref[...], sin_ref[...]
    for n0 in range(0, DP, PROJ_TN):
        r = _dot(h, w_ref[:, n0:n0 + PROJ_TN])
        o_ref[:, n0:n0 + PROJ_TN] = r.astype(BF16)
        if n0 <= C_QA < n0 + PROJ_TN:
            q = _rope(r[:, C_QA - n0:C_QA - n0 + AQ], jnp.tile(cos, (1, AQ // AK)), jnp.tile(sin, (1, AQ // AK)))
            q_ref[...] = (q * (HEAD_DIM ** -0.5 * LOG2E)).astype(BF16)
        if n0 <= C_KA < n0 + PROJ_TN:
            kt_ref[...] = _rope(r[:, C_KA - n0:C_KA - n0 + AK], cos, sin).T.astype(BF16)
    if moe_in:
        @pl.when(i == pl.num_programs(0) - 1)
        def _():
            _expert_rows_wait(y_hbm, ybuf, sem, 1 - buf)


def _moe_in_specs(n_tiles, tile, gate_spec):
    pos = lambda idx: pl.BlockSpec((None, 1, TOP_K * TM), lambda *ix: (idx(*ix), 0, 0), memory_space=pltpu.SMEM)
    return ([pos(lambda *ix: 0), pos(lambda *ix: jnp.minimum(tile(*ix) + 1, n_tiles - 1)),
             pl.BlockSpec(memory_space=pl.ANY)],
            [pl.BlockSpec((TM, TOP_K), lambda *ix: (tile(*ix), 0)), gate_spec])


def _moe_in_scratch():
    return [pltpu.VMEM((2, TOP_K * TM, D), F32), pltpu.SemaphoreType.DMA((2,))]


def _proj_in(x, g, mod, cos, sin, w, layer, bsz, moe_in=None):
    r = bsz * ROWS
    nt = r // TM
    table = pl.BlockSpec((TM, AK), lambda i: (i % TILES_PER_BATCH, 0))
    row = pl.BlockSpec((TM, D), lambda i: (i, 0))
    (in_specs, args), scratch = _stream_specs(x), []
    out_specs = [pl.BlockSpec((TM, DP), lambda i: (i, 0)),
                 pl.BlockSpec((TM, AQ), lambda i: (i, 0)),
                 pl.BlockSpec((AK, TM), lambda i: (0, i))]
    out_shape = [jax.ShapeDtypeStruct((r, DP), BF16), jax.ShapeDtypeStruct((r, AQ), BF16),
                 jax.ShapeDtypeStruct((AK, r), BF16)]
    if moe_in is not None:
        y, pos, top_p, prev_mod = moe_in
        pos = pos.reshape(nt, 1, TOP_K * TM)
        head, tail = _moe_in_specs(nt, lambda i: i, _mod_spec(5, bsz))
        in_specs, args = head + in_specs + tail, [pos, pos, y, x, top_p, prev_mod]
        out_specs.append(row)
        out_shape.append(jax.ShapeDtypeStruct((r, D), F32))
        scratch = _moe_in_scratch()
    in_specs += [pl.BlockSpec((1, D), lambda i: (0, 0)), _mod_spec(0, bsz), _mod_spec(1, bsz), table, table,
                 pl.BlockSpec((None, D, DP), lambda i: (layer, 0, 0), pipeline_mode=pl.Buffered(1))]
    args += [g, mod, mod, cos, sin, w]
    return pl.pallas_call(
        functools.partial(_proj_in_kernel, moe_in is not None),
        grid=(nt,),
        in_specs=in_specs,
        out_specs=out_specs,
        out_shape=out_shape,
        scratch_shapes=scratch,
        compiler_params=_params("arbitrary" if moe_in is not None else "parallel"),
        name="proj_in",
    )(*args)


CONV_W = 256


def _conv_kernel(cb_ref, cc_ref, cu_ref, w_ref, o_ref):
    z = cc_ref[...].astype(F32) * cu_ref[...].astype(F32)
    row = lax.broadcasted_iota(jnp.int32, z.shape, 0)
    z_prev = jnp.where((row == 0) | (row == CTX), 0.0, pltpu.roll(z, 1, 0))
    z_next = jnp.where((row == CTX - 1) | (row == ROWS - 1), 0.0, pltpu.roll(z, ROWS - 1, 0))
    w = w_ref[...]
    y = z_prev * w[0:1] + z * w[1:2] + z_next * w[2:3]
    o_ref[...] = (cb_ref[...].astype(F32) * y).astype(BF16)


def _conv(p, conv_w, bsz):
    r = p.shape[0]
    blk = lambda c: pl.BlockSpec((ROWS, CONV_W), lambda b, j: (b, c // CONV_W + j))
    return pl.pallas_call(
        _conv_kernel,
        grid=(bsz, BRANCH_W // CONV_W),
        in_specs=[blk(C_CB), blk(C_CC), blk(C_CU), pl.BlockSpec((3, CONV_W), lambda b, j: (0, j))],
        out_specs=pl.BlockSpec((ROWS, CONV_W), lambda b, j: (b, j)),
        out_shape=jax.ShapeDtypeStruct((r, BRANCH_W), BF16),
        compiler_params=_params("parallel", "parallel"),
        name="short_conv",
    )(p, p, p, conv_w)


NQB = ROWS // QB
NCB = CTX // QB
NLK = 3 * QB


def _attn_kernel(sink_ref, q_ref, kp_ref, ko_ref, kn_ref, kc_ref, vp_ref, vo_ref, vn_ref, vc_ref, o_ref):
    n = pl.program_id(1)
    is_ctx = n < NCB
    qr = q_ref[...]
    kt = jnp.concatenate([kp_ref[...], ko_ref[...], kn_ref[...], kc_ref[...]], axis=1)
    vcat = jnp.concatenate([vp_ref[...], vo_ref[...], vn_ref[...], vc_ref[...]], axis=0)
    i = lax.broadcasted_iota(jnp.int32, (ATT_GROUP * QB, QB), 0) % QB
    j = lax.broadcasted_iota(jnp.int32, (ATT_GROUP * QB, QB), 1)
    far = 4 * QB
    pen_p = jnp.where(n > NCB, 0, far)
    pen_n = jnp.where((n >= NCB) & (n < NQB - 1), 0, far)
    mask_p = j >= i + pen_p
    mask_n = j + pen_n <= i
    outs = [None] * ATT_HEADS
    for kv in range(ATT_KV):
        heads = range(kv * ATT_GROUP, (kv + 1) * ATT_GROUP)
        qg = jnp.concatenate([qr[:, h * HEAD_DIM:(h + 1) * HEAD_DIM] for h in heads], axis=0)
        sink = jnp.concatenate([jnp.full((QB, 1), sink_ref[h] * LOG2E, F32) for h in heads], axis=0)
        s = _dot(qg, kt[kv * HEAD_DIM:(kv + 1) * HEAD_DIM])
        tiles = [jnp.where(mask_p, s[:, :QB], NEG_INF),
                 jnp.where(is_ctx, NEG_INF, s[:, QB:2 * QB]),
                 jnp.where(mask_n, s[:, 2 * QB:NLK], NEG_INF)]
        tiles += [s[:, c:c + QB] for c in range(NLK, NLK + CTX, QB)]
        m = tiles[0]
        for t in tiles[1:]:
            m = jnp.maximum(m, t)
        m = jnp.maximum(jnp.max(m, axis=-1, keepdims=True), sink)
        tiles = [jnp.exp2(t - m) for t in tiles]
        acc = tiles[0]
        for t in tiles[1:]:
            acc = acc + t
        den = jnp.sum(acc, axis=-1, keepdims=True) + jnp.exp2(sink - m)
        p = jnp.concatenate(tiles, axis=1).astype(BF16)
        o = _dot(p, vcat[:, kv * HEAD_DIM:(kv + 1) * HEAD_DIM]) / den
        for g, h in enumerate(heads):
            outs[h] = o[g * QB:(g + 1) * QB]
    o_ref[...] = jnp.concatenate(outs, axis=1).astype(BF16)


def _rope_tables():
    half = HEAD_DIM // 2
    t = jnp.arange(SEQ)
    row = (t // GRID_W).astype(F32)
    col = (t % GRID_W).astype(F32)
    inv_freq = ROPE_BASE ** (-jnp.arange(0, half, 2, dtype=F32) / half)
    ar = row[:, None] * inv_freq
    ac = col[:, None] * inv_freq
    cos = jnp.concatenate([jnp.cos(ar), jnp.cos(ar), jnp.cos(ac), jnp.cos(ac)], axis=1)
    sin = jnp.concatenate([-jnp.sin(ar), jnp.sin(ar), -jnp.sin(ac), jnp.sin(ac)], axis=1)
    cos = jnp.concatenate([jnp.ones((CTX, HEAD_DIM), F32), cos], axis=0)
    sin = jnp.concatenate([jnp.zeros((CTX, HEAD_DIM), F32), sin], axis=0)
    return jnp.tile(cos, (1, ATT_KV)), jnp.tile(sin, (1, ATT_KV))


def _attention(p, qr, kt, sink, bsz):
    r = p.shape[0]
    lo, hi = NCB, NQB - 1
    near = lambda b, n, shift: b * NQB + jnp.clip(n + shift, lo, hi)
    kblk = lambda shift: pl.BlockSpec((AK, QB), lambda b, n: (0, near(b, n, shift)))
    vblk = lambda shift: pl.BlockSpec((QB, AK), lambda b, n: (near(b, n, shift), C_VA // AK))
    return pl.pallas_call(
        _attn_kernel,
        grid=(bsz, NQB),
        in_specs=[
            pl.BlockSpec(memory_space=pltpu.SMEM),
            pl.BlockSpec((QB, AQ), lambda b, n: (b * NQB + n, 0)),
            kblk(-1), kblk(0), kblk(1),
            pl.BlockSpec((AK, CTX), lambda b, n: (0, b * (ROWS // CTX))),
            vblk(-1), vblk(0), vblk(1),
            pl.BlockSpec((CTX, AK), lambda b, n: (b * (ROWS // CTX), C_VA // AK)),
        ],
        out_specs=pl.BlockSpec((QB, AQ), lambda b, n: (b * NQB + n, 0)),
        out_shape=jax.ShapeDtypeStruct((r, AQ), BF16),
        compiler_params=_params("parallel", "parallel"),
        name="window_attention",
    )(sink, qr, kt, kt, kt, kt, p, p, p, p)


NCHUNK = ROWS // GLA_CHUNK
NCHUNK_CTX = CTX // GLA_CHUNK
GQ = GLA_HEADS * GLA_DK
GV = GLA_HEADS * GLA_DV
GLA_BLK = 256


def _log_sigmoid(z):
    return jnp.minimum(z, 0.0) - jnp.log(1.0 + jnp.exp(-jnp.abs(z)))


def _gla_kernel(q_ref, k_ref, a_ref, v_ref, r_ref, w2_ref, b2_ref, g_ref, o_ref,
                of_ref, ob_ref, qd_ref, ke_ref, dec_ref, sf_ref, sb_ref):
    ri = lax.broadcasted_iota(jnp.int32, (GLA_BLK, GLA_BLK), 0)
    rj = lax.broadcasted_iota(jnp.int32, (GLA_BLK, GLA_BLK), 1)
    same_chunk = (ri // GLA_CHUNK) == (rj // GLA_CHUNK)
    causal = (same_chunk & (ri >= rj), same_chunk & (ri <= rj))
    lane_head = lax.broadcasted_iota(jnp.int32, (GLA_BLK, GQ), 1) // GLA_DK
    outs = (of_ref, ob_ref)

    def precompute(blk, carry):
        rows = pl.ds(pl.multiple_of(blk * GLA_BLK, GLA_BLK), GLA_BLK)
        a = a_ref[rows, :]
        q = q_ref[rows, :].astype(F32) * (GLA_DK ** -0.5)
        k = k_ref[rows, :].astype(F32)
        v = v_ref[rows, :]
        for d in range(2):
            la = _log_sigmoid(_dot(a, w2_ref[d]) + b2_ref[d]) * (1.0 / GLA_TAU)
            sel = jnp.concatenate([causal[d], same_chunk], axis=0).astype(BF16)
            hi = la.astype(BF16)
            r1 = la - hi.astype(F32)
            mid = r1.astype(BF16)
            lo = (r1 - mid.astype(F32)).astype(BF16)
            sums = _dot(sel, jnp.concatenate([hi, mid, lo], axis=1))
            sums = sums[:, :GQ] + sums[:, GQ:2 * GQ] + sums[:, 2 * GQ:]
            b, tot = sums[:GLA_BLK], sums[GLA_BLK:]
            q_dec = q * jnp.exp(b)
            k_inv = (k * jnp.exp(-b)).astype(BF16)
            qd_ref[d, rows, :] = q_dec.astype(BF16)
            ke_ref[d, rows, :] = (k * jnp.exp(tot - b)).astype(BF16)
            decay = jnp.exp(tot)
            for j in range(GLA_BLK // GLA_CHUNK):
                dec_ref[d, pl.ds(blk * (GLA_BLK // GLA_CHUNK) + j, 1), :] = decay[j * GLA_CHUNK:j * GLA_CHUNK + 1]
            intra = []
            for h in range(GLA_HEADS):
                qh = jnp.where(lane_head == h, q_dec, 0.0).astype(BF16)
                att = jnp.where(causal[d], _dot_nt(qh, k_inv), 0.0)
                intra.append(_dot(att.astype(BF16), v[:, h * GLA_DV:(h + 1) * GLA_DV]))
            outs[d][rows, :] = jnp.concatenate(intra, axis=1)
        return carry

    lax.fori_loop(0, ROWS // GLA_BLK, precompute, 0, unroll=3)

    sf_ref[...] = jnp.zeros_like(sf_ref)
    sb_ref[...] = jnp.zeros_like(sb_ref)
    bi = lax.broadcasted_iota(jnp.int32, (GV, GQ), 0)
    bj = lax.broadcasted_iota(jnp.int32, (GV, GQ), 1)
    block_diag = (bi // GLA_DV) == (bj // GLA_DK)

    def scan(i, carry):
        order = (i, jnp.where(i < NCHUNK_CTX, NCHUNK_CTX - 1 - i, NCHUNK + NCHUNK_CTX - 1 - i))
        for d, s_ref in enumerate((sf_ref, sb_ref)):
            c = order[d]
            rows = pl.ds(pl.multiple_of(c * GLA_CHUNK, GLA_CHUNK), GLA_CHUNK)
            s = s_ref[...]
            outs[d][rows, :] += _dot_nt(qd_ref[d, rows, :], s.astype(BF16))
            ds = _dot_tn(v_ref[rows, :], ke_ref[d, rows, :])
            s_ref[...] = s * dec_ref[d, pl.ds(c, 1), :] + jnp.where(block_diag, ds, 0.0)
        return carry

    lax.fori_loop(0, NCHUNK, scan, 0, unroll=2)

    def finish(t, carry):
        rows = pl.ds(pl.multiple_of(t * TM, TM), TM)
        o = of_ref[rows, :] + ob_ref[rows, :]
        parts = []
        for h in range(GLA_HEADS):
            oh = o[:, h * GLA_DV:(h + 1) * GLA_DV]
            parts.append(oh * lax.rsqrt(jnp.mean(oh * oh, axis=-1, keepdims=True) + EPS))
        on = jnp.concatenate(parts, axis=1) * g_ref[...]
        rg = r_ref[rows, :].astype(F32)
        o_ref[rows, :] = (on * (rg * jax.nn.sigmoid(rg))).astype(BF16)
        return carry

    lax.fori_loop(0, ROWS // TM, finish, 0)


def _gla(p, w2p, b2, g, bsz):
    r = p.shape[0]
    blk = lambda c, w: pl.BlockSpec((ROWS, w), lambda b: (b, c // w))
    return pl.pallas_call(
        _gla_kernel,
        grid=(bsz,),
        in_specs=[
            blk(C_QG, GQ), blk(C_KG, GQ), blk(C_AG, AG_W), blk(C_VG, GV), blk(C_RG, GV),
            pl.BlockSpec((2, AG_W, GQ), lambda b: (0, 0, 0)),
            pl.BlockSpec((2, 1, GQ), lambda b: (0, 0, 0)),
            pl.BlockSpec((1, GV), lambda b: (0, 0)),
        ],
        out_specs=pl.BlockSpec((ROWS, GV), lambda b: (b, 0)),
        out_shape=jax.ShapeDtypeStruct((r, GV), BF16),
        scratch_shapes=[
            pltpu.VMEM((ROWS, GV), F32), pltpu.VMEM((ROWS, GV), F32),
            pltpu.VMEM((2, ROWS, GQ), BF16), pltpu.VMEM((2, ROWS, GQ), BF16),
            pltpu.VMEM((2, NCHUNK, GQ), F32),
            pltpu.VMEM((GV, GQ), F32), pltpu.VMEM((GV, GQ), F32),
        ],
        compiler_params=_params("parallel"),
        name="gla",
    )(p, p, p, p, p, w2p, b2, g)


def _merge_kernel(ya_ref, yb_ref, yc_ref, g0_ref, g1_ref, g2_ref, *refs):
    *stream, gm_ref, wb_ref, wo_ref, o_ref = refs
    acc = jax.nn.sigmoid(g0_ref[...].astype(F32)) * _dot(ya_ref[...], wb_ref[0])
    acc += jax.nn.sigmoid(g1_ref[...].astype(F32)) * _dot(yb_ref[...], wb_ref[1])
    acc += jax.nn.sigmoid(g2_ref[...].astype(F32)) * _dot(yc_ref[...], wb_ref[2])
    y = _dot(acc.astype(BF16), wo_ref[...])
    o_ref[...] = _stream_tile(stream) + gm_ref[...] * y


def _merge(ya, yb, yc, p, x, mod, wb, wo, layer, bsz):
    r = p.shape[0]
    br = pl.BlockSpec((TM, BRANCH_W), lambda i: (i, 0))
    gate = lambda k: pl.BlockSpec((TM, D), lambda i: (i, C_GATE // D + k))
    stream_specs, stream_args = _stream_specs(x)
    return pl.pallas_call(
        _merge_kernel,
        grid=(r // TM,),
        in_specs=[br, br, br, gate(0), gate(1), gate(2), *stream_specs,
                  _mod_spec(2, bsz),
                  pl.BlockSpec((None, 3, BRANCH_W, D), lambda i: (layer, 0, 0, 0), pipeline_mode=pl.Buffered(1)),
                  pl.BlockSpec((None, D, D), lambda i: (layer, 0, 0), pipeline_mode=pl.Buffered(1))],
        out_specs=pl.BlockSpec((TM, D), lambda i: (i, 0)),
        out_shape=jax.ShapeDtypeStruct((r, D), F32),
        compiler_params=_params("parallel"),
        name="merge",
    )(ya, yb, yc, p, p, p, *stream_args, mod, wb, wo)


ROUTER_W = 128


def _router_kernel(x_ref, g_ref, sh_ref, sc_ref, w_ref, o_ref, h_ref):
    h = _norm_mod(x_ref[...], g_ref[...], sh_ref[...], sc_ref[...])
    h_ref[...] = h
    o_ref[...] = jnp.dot(h, w_ref[...], preferred_element_type=F32, precision=lax.Precision.HIGHEST)


def _router(x, g, mod, w, bsz):
    r = x.shape[0]
    return pl.pallas_call(
        _router_kernel,
        grid=(r // TM,),
        in_specs=[pl.BlockSpec((TM, D), lambda i: (i, 0)),
                  pl.BlockSpec((1, D), lambda i: (0, 0)),
                  _mod_spec(3, bsz), _mod_spec(4, bsz),
                  pl.BlockSpec((D, ROUTER_W), lambda i: (0, 0))],
        out_specs=[pl.BlockSpec((TM, ROUTER_W), lambda i: (i, 0)),
                   pl.BlockSpec((TM, D), lambda i: (i, 0))],
        out_shape=[jax.ShapeDtypeStruct((r, ROUTER_W), F32), jax.ShapeDtypeStruct((r, D), F32)],
        compiler_params=_params("parallel"),
        name="router",
    )(x, g, mod, mod, w)


MOE_T = 896
MOE_TF = 512
MOE_NF = D_FF // MOE_TF
MOE_CH = MOE_T // MOE_NF
MOE_ID = 1 << 17


def _moe_num_tiles(r):
    return (TOP_K * r + N_EXPERTS * (MOE_T - 1)) // MOE_T


def _route(logits, row_of=lambda t: t):
    n = logits.shape[0]
    nt = _moe_num_tiles(n)
    top_v, top_i = lax.top_k(logits, TOP_K)
    top_p = jax.nn.softmax(top_v, axis=-1)
    na = TOP_K * n
    nfill = nt * MOE_T - na
    assert na + nfill <= MOE_ID and na > MOE_T
    e_flat = top_i.reshape(-1).astype(jnp.int32)
    onehot = (e_flat[:, None] == jnp.arange(N_EXPERTS)[None, :]).astype(jnp.int32)
    csum = jnp.cumsum(onehot, axis=0)
    counts = csum[-1]
    tiles_e = (counts + MOE_T - 1) // MOE_T
    tile_end = jnp.cumsum(tiles_e)
    n_active = tile_end[-1]
    pos = jnp.sum(onehot * ((tile_end - tiles_e)[None, :] * MOE_T + csum - 1), axis=1)
    tile_id = jnp.minimum(jnp.arange(nt), n_active - 1)
    tile_expert = jnp.sum((tile_id[:, None] >= tile_end[None, :]).astype(jnp.int32), axis=1)
    fill = jnp.arange(nfill, dtype=jnp.int32)
    pad_end = jnp.cumsum(tiles_e * MOE_T - counts)
    e_fill = jnp.sum((fill[:, None] >= pad_end[None, :]).astype(jnp.int32), axis=1)
    ident = jnp.sort(jnp.concatenate([e_flat * MOE_ID + jnp.arange(na, dtype=jnp.int32),
                                      e_fill * MOE_ID + na + fill])) % MOE_ID
    src = row_of(jnp.where(ident < na, ident // TOP_K, (ident - na) % n))
    return (tile_expert.astype(jnp.int32), n_active.reshape(1).astype(jnp.int32),
            src.reshape(nt, 1, MOE_T), pos.astype(jnp.int32), top_p)


def _moe_kernel(te_ref, na_ref, src0_ref, srcn_ref, h_hbm, w1_ref, w3_ref, w2_ref, o_ref,
                gbuf, hs_ref, acc_ref, sem):
    i = pl.program_id(0)
    f = pl.program_id(1)
    active = i < na_ref[0]
    has_next = i + 1 < na_ref[0]
    slot = i % 2

    def gather_row(src_ref, buf, row):
        pltpu.make_async_copy(h_hbm.at[pl.ds(src_ref[0, row], 1)], gbuf.at[buf, pl.ds(row, 1)], sem.at[buf]).start()

    @pl.when((i == 0) & (f == 0))
    def _():
        def body(r, carry):
            gather_row(src0_ref, 0, r)
            return carry
        lax.fori_loop(0, MOE_T, body, 0)

    @pl.when(active & (f == 0))
    def _():
        pltpu.make_async_copy(h_hbm.at[pl.ds(0, MOE_T)], gbuf.at[slot], sem.at[slot]).wait()
        hs_ref[...] = gbuf[slot].astype(BF16)
        acc_ref[...] = jnp.zeros_like(acc_ref)

    def step(prefetch):
        if prefetch:
            for r in range(MOE_CH):
                gather_row(srcn_ref, 1 - slot, f * MOE_CH + r)
        h = hs_ref[...]
        a = _dot(h, w1_ref[...].astype(BF16))
        t = a * jax.nn.sigmoid(a) * _dot(h, w3_ref[...].astype(BF16))
        acc_ref[...] += _dot(t.astype(BF16), w2_ref[...].astype(BF16))

    @pl.when(has_next)
    def _():
        step(True)

    @pl.when(active & jnp.logical_not(has_next))
    def _():
        step(False)

    @pl.when(f == MOE_NF - 1)
    def _():
        o_ref[...] = jnp.where(active, acc_ref[...], 0.0)


def _moe_experts(h, tile_expert, n_active, src, w1, w3, w2, layer):
    nt = src.shape[0]
    hidden = lambda i, f, te, na: jnp.where(i < na[0], f, MOE_NF - 1)
    table = lambda idx: pl.BlockSpec((None, 1, MOE_T), lambda i, f, te, na: (idx(i), 0, 0), memory_space=pltpu.SMEM)
    grid_spec = pltpu.PrefetchScalarGridSpec(
        num_scalar_prefetch=2,
        grid=(nt, MOE_NF),
        in_specs=[
            table(lambda i: 0),
            table(lambda i: jnp.minimum(i + 1, nt - 1)),
            pl.BlockSpec(memory_space=pl.ANY),
            pl.BlockSpec((None, None, D, MOE_TF), lambda i, f, te, na: (layer, te[i], 0, hidden(i, f, te, na))),
            pl.BlockSpec((None, None, D, MOE_TF), lambda i, f, te, na: (layer, te[i], 0, hidden(i, f, te, na))),
            pl.BlockSpec((None, None, MOE_TF, D), lambda i, f, te, na: (layer, te[i], hidden(i, f, te, na), 0)),
        ],
        out_specs=pl.BlockSpec((MOE_T, D), lambda i, f, te, na: (i, 0)),
        scratch_shapes=[pltpu.VMEM((2, MOE_T, D), F32), pltpu.VMEM((MOE_T, D), BF16),
                        pltpu.VMEM((MOE_T, D), F32), pltpu.SemaphoreType.DMA((2,))],
    )
    return pl.pallas_call(
        _moe_kernel,
        grid_spec=grid_spec,
        out_shape=jax.ShapeDtypeStruct((nt * MOE_T, D), F32),
        compiler_params=_params("arbitrary", "arbitrary"),
        name="moe_experts",
    )(tile_expert, n_active, src, src, h, w1, w3, w2)


def _expert_rows_start(pos_ref, y_hbm, ybuf, sem, buf, r):
    for k in range(TOP_K):
        pltpu.make_async_copy(y_hbm.at[pl.ds(pos_ref[0, TOP_K * r + k], 1)],
                              ybuf.at[buf, pl.ds(k * TM + r, 1)], sem.at[buf]).start()


def _expert_rows_wait(y_hbm, ybuf, sem, buf):
    pltpu.make_async_copy(y_hbm.at[pl.ds(0, TOP_K * TM)], ybuf.at[buf], sem.at[buf]).wait()


def _moe_residual(x_ref, ybuf, buf, p_ref, gm_ref):
    p = p_ref[...]
    y = p[:, 0:1] * ybuf[buf, 0:TM, :]
    for k in range(1, TOP_K):
        y += p[:, k:k + 1] * ybuf[buf, k * TM:(k + 1) * TM, :]
    return x_ref[...] + gm_ref[...] * y


FFN_TILES_PER_BATCH = ROWS // FFN_TM


def _ffn_kernel(x_ref, g_ref, mx_ref, mc_ref, w1_ref, w3_ref, w2_ref, o_ref, h_ref, acc_ref):
    f = pl.program_id(1)
    row = lax.broadcasted_iota(jnp.int32, (FFN_TM, 1), 0) + (pl.program_id(0) % FFN_TILES_PER_BATCH) * FFN_TM
    is_ctx = row < CTX

    def mod(j):
        return jnp.where(is_ctx, mc_ref[:, j * D:(j + 1) * D], mx_ref[:, j * D:(j + 1) * D])

    @pl.when(f == 0)
    def _():
        h_ref[...] = _norm_mod(x_ref[...], g_ref[...], mod(3), mod(4)).astype(BF16)
        acc_ref[...] = jnp.zeros_like(acc_ref)

    h = h_ref[...]
    a = _dot(h, w1_ref[...])
    t = a * jax.nn.sigmoid(a) * _dot(h, w3_ref[...])
    acc_ref[...] += _dot(t.astype(BF16), w2_ref[...])

    @pl.when(f == pl.num_programs(1) - 1)
    def _():
        o_ref[...] = x_ref[...] + mod(5) * acc_ref[...]


def _ffn(x, g, mod, w1, w3, w2, layer, bsz):
    r = x.shape[0]
    row = pl.BlockSpec((FFN_TM, D), lambda i, f: (i, 0))
    return pl.pallas_call(
        _ffn_kernel,
        grid=(r // FFN_TM, D_FF // FFN_TF),
        in_specs=[row,
                  pl.BlockSpec((1, D), lambda i, f: (0, 0)),
                  pl.BlockSpec((None, 1, 6 * D), lambda i, f: (i // FFN_TILES_PER_BATCH, 0, 0)),
                  pl.BlockSpec((None, 1, 6 * D), lambda i, f: (bsz, 0, 0)),
                  pl.BlockSpec((None, D, FFN_TF), lambda i, f: (layer, 0, f)),
                  pl.BlockSpec((None, D, FFN_TF), lambda i, f: (layer, 0, f)),
                  pl.BlockSpec((None, FFN_TF, D), lambda i, f: (layer, f, 0))],
        out_specs=row,
        out_shape=jax.ShapeDtypeStruct((r, D), F32),
        scratch_shapes=[pltpu.VMEM((FFN_TM, D), BF16), pltpu.VMEM((FFN_TM, D), F32)],
        compiler_params=_params("parallel", "arbitrary"),
        name="ffn",
    )(x, g, mod, mod, w1, w3, w2)


def _final_kernel(moe_in, *refs):
    if moe_in:
        pos0_ref, posn_ref, y_hbm, x_ref, p_ref, gm_ref, g_ref, o_ref, ybuf, sem = refs
        t = pl.program_id(0) * pl.num_programs(1) + pl.program_id(1)
        buf = t % 2

        @pl.when(t == 0)
        def _():
            def body(r, carry):
                _expert_rows_start(pos0_ref, y_hbm, ybuf, sem, 0, r)
                return carry
            lax.fori_loop(0, TM, body, 0)

        @pl.when(t + 1 < pl.num_programs(0) * pl.num_programs(1))
        def _():
            for r in range(TM):
                _expert_rows_start(posn_ref, y_hbm, ybuf, sem, 1 - buf, r)

        _expert_rows_wait(y_hbm, ybuf, sem, buf)
        x = _moe_residual(x_ref, ybuf, buf, p_ref, gm_ref)
    else:
        x_ref, g_ref, o_ref = refs
        x = x_ref[...]
    o_ref[...] = x * lax.rsqrt(jnp.mean(x * x, axis=-1, keepdims=True) + EPS) * g_ref[...]


def _final_norm(x, g, bsz, moe_in=None):
    lat = SEQ // TM
    in_specs = [pl.BlockSpec((TM, D), lambda b, j: (b * TILES_PER_BATCH + CTX // TM + j, 0))]
    args, scratch = [x], []
    if moe_in is not None:
        y, pos, top_p, prev_mod = moe_in
        pos = pos.reshape(bsz * lat, 1, TOP_K * TM)
        head, tail = _moe_in_specs(bsz * lat, lambda b, j: b * lat + j,
                                   pl.BlockSpec((None, 1, D), lambda b, j: (b, 0, 5)))
        in_specs, args = head + in_specs + tail, [pos, pos, y, x, top_p, prev_mod]
        scratch = _moe_in_scratch()
    in_specs.append(pl.BlockSpec((1, D), lambda b, j: (0, 0)))
    args.append(g)
    sem = ("arbitrary", "arbitrary") if moe_in is not None else ("parallel", "parallel")
    return pl.pallas_call(
        functools.partial(_final_kernel, moe_in is not None),
        grid=(bsz, lat),
        in_specs=in_specs,
        out_specs=pl.BlockSpec((TM, D), lambda b, j: (b * lat + j, 0)),
        out_shape=jax.ShapeDtypeStruct((bsz * SEQ, D), F32),
        scratch_shapes=scratch,
        compiler_params=_params(*sem),
        name="final_norm",
    )(*args)


D_IN = 6944
SRC_VG = 2816
SRC_AG = SRC_VG + 2 * GV
SRC_GATE = SRC_AG + 2 * GLA_RANK
LAYOUT_TM = 256


def _w_in_layout_kernel(w_ref, o_ref):
    o_ref[:, :C_AG] = w_ref[:, :SRC_VG].astype(BF16)
    o_ref[:, C_AG:C_AG + AG_W] = jnp.zeros((LAYOUT_TM, AG_W), BF16)
    o_ref[:, C_AG:C_AG + 2 * GLA_RANK] = w_ref[:, SRC_AG:SRC_GATE].astype(BF16)
    o_ref[:, C_AG + AG_W:C_VG] = jnp.zeros((LAYOUT_TM, C_VG - C_AG - AG_W), BF16)
    o_ref[:, C_VG:C_GATE] = w_ref[:, SRC_VG:SRC_AG].astype(BF16)
    o_ref[:, C_GATE:] = w_ref[:, SRC_GATE:].astype(BF16)


def _layout_w_in(w_in):
    depth = w_in.shape[0]
    assert w_in.shape[1:] == (D, D_IN)
    return pl.pallas_call(
        _w_in_layout_kernel,
        grid=(depth, D // LAYOUT_TM),
        in_specs=[pl.BlockSpec((None, LAYOUT_TM, D_IN), lambda l, i: (l, i, 0))],
        out_specs=pl.BlockSpec((None, LAYOUT_TM, DP), lambda l, i: (l, i, 0)),
        out_shape=jax.ShapeDtypeStruct((depth, D, DP), BF16),
        compiler_params=_params("parallel", "parallel"),
        name="w_in_layout",
    )(w_in)


def _layout_gla_w2(gla_w2):
    depth = gla_w2.shape[0]
    out = jnp.zeros((depth, 2, AG_W, GQ), F32)
    for d in range(2):
        out = out.at[:, d, d * GLA_RANK:(d + 1) * GLA_RANK, :].set(gla_w2[:, d])
    return out.astype(BF16)


def kernel(x, c, ctx, c_ctx, w_mod, b_mod, norm1_g, norm2_g, w_in, conv_w, attn_sink, gla_w2, gla_b,
           gla_norm_g, w_branch, w_out, ffn_w1, ffn_w3, ffn_w2, router_w, moe_w1, moe_w3, moe_w2,
           final_norm_g):
    bsz = x.shape[0]
    depth = w_in.shape[0]
    assert x.shape[1:] == (SEQ, D) and ctx.shape[1:] == (CTX, D)

    nrow = -(-(bsz + 1) // 8) * 8
    cond = jnp.zeros((nrow, D), F32).at[:bsz].set(c).at[bsz].set(c_ctx)
    mods = _modulation(cond, w_mod, b_mod).reshape(depth, nrow, 1, 6 * D)

    xs = (ctx.reshape(bsz * CTX, D), x.reshape(bsz * SEQ, D))
    w_in_p = _layout_w_in(w_in)
    w2p = _layout_gla_w2(gla_w2)
    cos, sin = _rope_tables()
    router_p = jnp.pad(router_w, ((0, 0), (0, 0), (0, ROUTER_W - N_EXPERTS)))

    wb, wo = w_branch.astype(BF16), w_out.astype(BF16)
    f1, f3, f2 = ffn_w1.astype(BF16), ffn_w3.astype(BF16), ffn_w2.astype(BF16)

    moe_in = None
    for l in range(depth):
        mod = mods[l]
        outs = _proj_in(xs, norm1_g[l].reshape(1, D), mod, cos, sin, w_in_p, l, bsz, moe_in)
        p, qr, kt = outs[:3]
        if moe_in is not None:
            xs, moe_in = outs[3], None
        ya = _conv(p, conv_w[l], bsz)
        yb = _attention(p, qr, kt, attn_sink[l], bsz)
        yc = _gla(p, w2p[l], gla_b[l].reshape(2, 1, GQ), gla_norm_g[l].reshape(1, GV), bsz)
        xs = _merge(ya, yb, yc, p, xs, mod, wb, wo, l, bsz)
        g2 = norm2_g[l].reshape(1, D)
        j = l // 2
        if l % 2 == 0:
            xs = _ffn(xs, g2, mod, f1, f3, f2, j, bsz)
        else:
            logits, h2 = _router(xs, g2, mod, router_p[j], bsz)
            logits = logits[:, :N_EXPERTS]
            if l == depth - 1:
                logits = logits.reshape(bsz, ROWS, N_EXPERTS)[:, CTX:].reshape(bsz * SEQ, N_EXPERTS)
                routed = _route(logits, lambda t: (t // SEQ) * ROWS + CTX + t % SEQ)
            else:
                routed = _route(logits)
            tile_expert, n_active, src, pos, top_p = routed
            y = _moe_experts(h2, tile_expert, n_active, src, moe_w1, moe_w3, moe_w2, j)
            moe_in = (y, pos, top_p, mod)
    out = _final_norm(xs, final_norm_g.reshape(1, D), bsz, moe_in)
    return out.reshape(bsz, SEQ, D)
```

```python
import functools

import jax
import jax.numpy as jnp
from jax import lax
from jax.experimental import pallas as pl
from jax.experimental.pallas import tpu as pltpu

F32 = jnp.float32
BF16 = jnp.bfloat16

D = 1024
SEQ = 2048
CTX = 256
ROWS = CTX + SEQ
GRID_W = 64
EPS = 1e-6
NEG_INF = -1e30
LOG2E = 1.4426950408889634

BRANCH_W = 512
ATT_HEADS = 8
ATT_KV = 2
ATT_GROUP = ATT_HEADS // ATT_KV
HEAD_DIM = 64
QB = 128
ROPE_BASE = 10000.0
GLA_HEADS = 4
GLA_DK = 64
GLA_DV = 128
GLA_RANK = 16
GLA_TAU = 16.0
GLA_CHUNK = 64
D_FF = 3584
N_EXPERTS = 8
TOP_K = 2

C_CB, C_CC, C_CU = 0, 512, 1024
C_QA, C_KA, C_VA = 1536, 2048, 2176
C_QG, C_KG, C_AG = 2304, 2560, 2816
C_VG, C_RG, C_GATE = 3072, 3584, 4096
DP = 7168
AG_W = 128
GATE_W = 3 * D
D_IN = 6944
SRC_VG = 2816
SRC_AG = SRC_VG + 2 * 512
SRC_GATE = SRC_AG + 2 * GLA_RANK
WP = 7040

TM = 256
TILES_PER_BATCH = ROWS // TM
PROJ_TN = 1024
FFN_TF = 512
VMEM_LIMIT = 56 * 1024 * 1024


def _params(*sem):
    return pltpu.CompilerParams(dimension_semantics=sem, vmem_limit_bytes=VMEM_LIMIT)


def _dot(a, b):
    return jnp.dot(a, b, preferred_element_type=F32)


def _dot_nt(a, b):
    return lax.dot_general(a, b, (((1,), (1,)), ((), ())), preferred_element_type=F32)


def _dot_tn(a, b):
    return lax.dot_general(a, b, (((0,), (0,)), ((), ())), preferred_element_type=F32)


def _norm_mod(x, g, shift, scale):
    h = x * lax.rsqrt(jnp.mean(x * x, axis=-1, keepdims=True) + EPS) * g
    return h * (1.0 + scale) + shift


def _mod_kernel(s_ref, w_ref, b_ref, o_ref):
    s = s_ref[...]
    s = s * jax.nn.sigmoid(s)
    o_ref[...] = _dot(s.astype(BF16), w_ref[...].astype(BF16)) + b_ref[...]


def _modulation(cond, w_mod, b_mod):
    depth = w_mod.shape[0]
    nrow = cond.shape[0]
    return pl.pallas_call(
        _mod_kernel,
        grid=(depth, 6),
        in_specs=[
            pl.BlockSpec((nrow, D), lambda l, j: (0, 0)),
            pl.BlockSpec((None, D, D), lambda l, j: (l, 0, j)),
            pl.BlockSpec((None, 1, D), lambda l, j: (l, 0, j)),
        ],
        out_specs=pl.BlockSpec((None, nrow, D), lambda l, j: (l, 0, j)),
        out_shape=jax.ShapeDtypeStruct((depth, nrow, 6 * D), F32),
        compiler_params=_params("parallel", "parallel"),
        name="modulation",
    )(cond, w_mod, b_mod.reshape(depth, 1, 6 * D))


def _mod_row(i, bsz):
    return jnp.where(i % TILES_PER_BATCH == 0, bsz, i // TILES_PER_BATCH)


def _mod_spec(j, bsz):
    return pl.BlockSpec((None, 1, D), lambda i: (_mod_row(i, bsz), 0, j))


def _rope(x, cos, sin):
    w = x.shape[-1]
    lane = lax.broadcasted_iota(jnp.int32, x.shape, 1)
    partner = jnp.where((lane % 32) < 16, pltpu.roll(x, w - 16, 1), pltpu.roll(x, 16, 1))
    return x * cos + partner * sin


AQ = ATT_HEADS * HEAD_DIM
AK = ATT_KV * HEAD_DIM


def _stream_specs(x):
    if not isinstance(x, tuple):
        return [pl.BlockSpec((TM, D), lambda i: (i, 0))], [x]
    assert CTX == TM
    lat = SEQ // TM
    return ([pl.BlockSpec((TM, D), lambda i: (i // TILES_PER_BATCH, 0)),
             pl.BlockSpec((TM, D), lambda i: ((i // TILES_PER_BATCH) * lat
                                              + jnp.maximum(i % TILES_PER_BATCH - 1, 0), 0))],
            list(x))


def _stream_tile(refs):
    if len(refs) == 1:
        return refs[0][...]
    return jnp.where(pl.program_id(0) % TILES_PER_BATCH == 0, refs[0][...], refs[1][...])


def _proj_in_kernel(moe_in, *refs):
    if moe_in:
        (pos0_ref, posn_ref, y_hbm, x_ref, p_ref, gm_ref, g_ref, sh_ref, sc_ref, cos_ref, sin_ref, w_ref,
         o_ref, q_ref, kt_ref, xo_ref, ybuf, sem, wg_ref) = refs
        i = pl.program_id(0)
        buf = i % 2

        @pl.when(i == 0)
        def _():
            def body(r, carry):
                _expert_rows_start(pos0_ref, y_hbm, ybuf, sem, 0, r)
                return carry
            lax.fori_loop(0, TM, body, 0)

        _expert_rows_wait(y_hbm, ybuf, sem, buf)
        x = _moe_residual(x_ref, ybuf, buf, p_ref, gm_ref)
        xo_ref[...] = x
        for r in range(TM):
            _expert_rows_start(posn_ref, y_hbm, ybuf, sem, 1 - buf, r)
    else:
        *stream, g_ref, sh_ref, sc_ref, cos_ref, sin_ref, w_ref, o_ref, q_ref, kt_ref, wg_ref = refs
        x = _stream_tile(stream)

    @pl.when(pl.program_id(0) == 0)
    def _():
        wg_ref[...] = w_ref[:, SRC_GATE:SRC_GATE + GATE_W]

    h = _norm_mod(x, g_ref[...], sh_ref[...], sc_ref[...]).astype(BF16)
    cos, sin = cos_ref[...], sin_ref[...]
    for n0 in range(0, SRC_VG, PROJ_TN):
        n1 = min(n0 + PROJ_TN, SRC_VG)
        r = _dot(h, w_ref[:, n0:n1])
        o_ref[:, n0:n1] = r.astype(BF16)
        if n0 <= C_QA < n1:
            q = _rope(r[:, C_QA - n0:C_QA - n0 + AQ], jnp.tile(cos, (1, AQ // AK)), jnp.tile(sin, (1, AQ // AK)))
            q_ref[...] = (q * (HEAD_DIM ** -0.5 * LOG2E)).astype(BF16)
        if n0 <= C_KA < n1:
            kt_ref[...] = _rope(r[:, C_KA - n0:C_KA - n0 + AK], cos, sin).T.astype(BF16)
    r = _dot(h, w_ref[:, SRC_VG:SRC_AG + AG_W]).astype(BF16)
    o_ref[:, C_VG:C_GATE] = r[:, :SRC_AG - SRC_VG]
    o_ref[:, C_AG:C_AG + AG_W] = r[:, SRC_AG - SRC_VG:]
    o_ref[:, C_AG + AG_W:C_VG] = jnp.zeros((TM, C_VG - C_AG - AG_W), BF16)
    for n0 in range(0, GATE_W, PROJ_TN):
        o_ref[:, C_GATE + n0:C_GATE + n0 + PROJ_TN] = _dot(h, wg_ref[:, n0:n0 + PROJ_TN]).astype(BF16)
    if moe_in:
        @pl.when(i == pl.num_programs(0) - 1)
        def _():
            _expert_rows_wait(y_hbm, ybuf, sem, 1 - buf)


def _moe_in_specs(n_tiles, tile, gate_spec):
    pos = lambda idx: pl.BlockSpec((None, 1, TOP_K * TM), lambda *ix: (idx(*ix), 0, 0), memory_space=pltpu.SMEM)
    return ([pos(lambda *ix: 0), pos(lambda *ix: jnp.minimum(tile(*ix) + 1, n_tiles - 1)),
             pl.BlockSpec(memory_space=pl.ANY)],
            [pl.BlockSpec((TM, TOP_K), lambda *ix: (tile(*ix), 0)), gate_spec])


def _moe_in_scratch():
    return [pltpu.VMEM((2, TOP_K * TM, D), F32), pltpu.SemaphoreType.DMA((2,))]


def _proj_in(x, g, mod, cos, sin, w, layer, bsz, moe_in=None):
    r = bsz * ROWS
    nt = r // TM
    table = pl.BlockSpec((TM, AK), lambda i: (i % TILES_PER_BATCH, 0))
    row = pl.BlockSpec((TM, D), lambda i: (i, 0))
    (in_specs, args), scratch = _stream_specs(x), []
    out_specs = [pl.BlockSpec((TM, DP), lambda i: (i, 0)),
                 pl.BlockSpec((TM, AQ), lambda i: (i, 0)),
                 pl.BlockSpec((AK, TM), lambda i: (0, i))]
    out_shape = [jax.ShapeDtypeStruct((r, DP), BF16), jax.ShapeDtypeStruct((r, AQ), BF16),
                 jax.ShapeDtypeStruct((AK, r), BF16)]
    if moe_in is not None:
        y, pos, top_p, prev_mod = moe_in
        pos = pos.reshape(nt, 1, TOP_K * TM)
        head, tail = _moe_in_specs(nt, lambda i: i, _mod_spec(5, bsz))
        in_specs, args = head + in_specs + tail, [pos, pos, y, x, top_p, prev_mod]
        out_specs.append(row)
        out_shape.append(jax.ShapeDtypeStruct((r, D), F32))
        scratch = _moe_in_scratch()
    in_specs += [pl.BlockSpec((1, D), lambda i: (0, 0)), _mod_spec(0, bsz), _mod_spec(1, bsz), table, table,
                 pl.BlockSpec((None, D, WP), lambda i: (layer, 0, 0), pipeline_mode=pl.Buffered(1))]
    args += [g, mod, mod, cos, sin, w]
    return pl.pallas_call(
        functools.partial(_proj_in_kernel, moe_in is not None),
        grid=(nt,),
        in_specs=in_specs,
        out_specs=out_specs,
        out_shape=out_shape,
        scratch_shapes=scratch + [pltpu.VMEM((D, GATE_W), BF16)],
        compiler_params=_params("arbitrary"),
        name="proj_in",
    )(*args)


CONV_W = 256


def _conv_kernel(cb_ref, cc_ref, cu_ref, w_ref, o_ref):
    z = cc_ref[...].astype(F32) * cu_ref[...].astype(F32)
    row = lax.broadcasted_iota(jnp.int32, z.shape, 0)
    z_prev = jnp.where((row == 0) | (row == CTX), 0.0, pltpu.roll(z, 1, 0))
    z_next = jnp.where((row == CTX - 1) | (row == ROWS - 1), 0.0, pltpu.roll(z, ROWS - 1, 0))
    w = w_ref[...]
    y = z_prev * w[0:1] + z * w[1:2] + z_next * w[2:3]
    o_ref[...] = (cb_ref[...].astype(F32) * y).astype(BF16)


def _conv(p, conv_w, bsz):
    r = p.shape[0]
    blk = lambda c: pl.BlockSpec((ROWS, CONV_W), lambda b, j: (b, c // CONV_W + j))
    return pl.pallas_call(
        _conv_kernel,
        grid=(bsz, BRANCH_W // CONV_W),
        in_specs=[blk(C_CB), blk(C_CC), blk(C_CU), pl.BlockSpec((3, CONV_W), lambda b, j: (0, j))],
        out_specs=pl.BlockSpec((ROWS, CONV_W), lambda b, j: (b, j)),
        out_shape=jax.ShapeDtypeStruct((r, BRANCH_W), BF16),
        compiler_params=_params("parallel", "parallel"),
        name="short_conv",
    )(p, p, p, conv_w)


NQB = ROWS // QB
NCB = CTX // QB
NLK = 3 * QB


def _attn_kernel(sink_ref, q_ref, kp_ref, ko_ref, kn_ref, kc_ref, vp_ref, vo_ref, vn_ref, vc_ref, o_ref):
    n = pl.program_id(1)
    is_ctx = n < NCB
    qr = q_ref[...]
    kt = jnp.concatenate([kp_ref[...], ko_ref[...], kn_ref[...], kc_ref[...]], axis=1)
    vcat = jnp.concatenate([vp_ref[...], vo_ref[...], vn_ref[...], vc_ref[...]], axis=0)
    i = lax.broadcasted_iota(jnp.int32, (ATT_GROUP * QB, QB), 0) % QB
    j = lax.broadcasted_iota(jnp.int32, (ATT_GROUP * QB, QB), 1)
    far = 4 * QB
    pen_p = jnp.where(n > NCB, 0, far)
    pen_n = jnp.where((n >= NCB) & (n < NQB - 1), 0, far)
    mask_p = j >= i + pen_p
    mask_n = j + pen_n <= i
    outs = [None] * ATT_HEADS
    for kv in range(ATT_KV):
        heads = range(kv * ATT_GROUP, (kv + 1) * ATT_GROUP)
        qg = jnp.concatenate([qr[:, h * HEAD_DIM:(h + 1) * HEAD_DIM] for h in heads], axis=0)
        sink = jnp.concatenate([jnp.full((QB, 1), sink_ref[h] * LOG2E, F32) for h in heads], axis=0)
        s = _dot(qg, kt[kv * HEAD_DIM:(kv + 1) * HEAD_DIM])
        tiles = [jnp.where(mask_p, s[:, :QB], NEG_INF),
                 jnp.where(is_ctx, NEG_INF, s[:, QB:2 * QB]),
                 jnp.where(mask_n, s[:, 2 * QB:NLK], NEG_INF)]
        tiles += [s[:, c:c + QB] for c in range(NLK, NLK + CTX, QB)]
        m = tiles[0]
        for t in tiles[1:]:
            m = jnp.maximum(m, t)
        m = jnp.maximum(jnp.max(m, axis=-1, keepdims=True), sink)
        tiles = [jnp.exp2(t - m) for t in tiles]
        acc = tiles[0]
        for t in tiles[1:]:
            acc = acc + t
        den = jnp.sum(acc, axis=-1, keepdims=True) + jnp.exp2(sink - m)
        p = jnp.concatenate(tiles, axis=1).astype(BF16)
        o = _dot(p, vcat[:, kv * HEAD_DIM:(kv + 1) * HEAD_DIM]) / den
        for g, h in enumerate(heads):
            outs[h] = o[g * QB:(g + 1) * QB]
    o_ref[...] = jnp.concatenate(outs, axis=1).astype(BF16)


def _rope_tables():
    half = HEAD_DIM // 2
    t = jnp.arange(SEQ)
    row = (t // GRID_W).astype(F32)
    col = (t % GRID_W).astype(F32)
    inv_freq = ROPE_BASE ** (-jnp.arange(0, half, 2, dtype=F32) / half)
    ar = row[:, None] * inv_freq
    ac = col[:, None] * inv_freq
    cos = jnp.concatenate([jnp.cos(ar), jnp.cos(ar), jnp.cos(ac), jnp.cos(ac)], axis=1)
    sin = jnp.concatenate([-jnp.sin(ar), jnp.sin(ar), -jnp.sin(ac), jnp.sin(ac)], axis=1)
    cos = jnp.concatenate([jnp.ones((CTX, HEAD_DIM), F32), cos], axis=0)
    sin = jnp.concatenate([jnp.zeros((CTX, HEAD_DIM), F32), sin], axis=0)
    return jnp.tile(cos, (1, ATT_KV)), jnp.tile(sin, (1, ATT_KV))


def _attention(p, qr, kt, sink, bsz):
    r = p.shape[0]
    lo, hi = NCB, NQB - 1
    near = lambda b, n, shift: b * NQB + jnp.clip(n + shift, lo, hi)
    kblk = lambda shift: pl.BlockSpec((AK, QB), lambda b, n: (0, near(b, n, shift)))
    vblk = lambda shift: pl.BlockSpec((QB, AK), lambda b, n: (near(b, n, shift), C_VA // AK))
    return pl.pallas_call(
        _attn_kernel,
        grid=(bsz, NQB),
        in_specs=[
            pl.BlockSpec(memory_space=pltpu.SMEM),
            pl.BlockSpec((QB, AQ), lambda b, n: (b * NQB + n, 0)),
            kblk(-1), kblk(0), kblk(1),
            pl.BlockSpec((AK, CTX), lambda b, n: (0, b * (ROWS // CTX))),
            vblk(-1), vblk(0), vblk(1),
            pl.BlockSpec((CTX, AK), lambda b, n: (b * (ROWS // CTX), C_VA // AK)),
        ],
        out_specs=pl.BlockSpec((QB, AQ), lambda b, n: (b * NQB + n, 0)),
        out_shape=jax.ShapeDtypeStruct((r, AQ), BF16),
        compiler_params=_params("parallel", "parallel"),
        name="window_attention",
    )(sink, qr, kt, kt, kt, kt, p, p, p, p)


NCHUNK = ROWS // GLA_CHUNK
NCHUNK_CTX = CTX // GLA_CHUNK
GQ = GLA_HEADS * GLA_DK
GV = GLA_HEADS * GLA_DV
GLA_BLK = 256


def _log_sigmoid(z):
    return jnp.minimum(z, 0.0) - jnp.log(1.0 + jnp.exp(-jnp.abs(z)))


def _gla_kernel(q_ref, k_ref, a_ref, v_ref, r_ref, w2_ref, b2_ref, g_ref, o_ref,
                of_ref, ob_ref, qd_ref, ke_ref, dec_ref, sf_ref, sb_ref):
    ri = lax.broadcasted_iota(jnp.int32, (GLA_BLK, GLA_BLK), 0)
    rj = lax.broadcasted_iota(jnp.int32, (GLA_BLK, GLA_BLK), 1)
    same_chunk = (ri // GLA_CHUNK) == (rj // GLA_CHUNK)
    causal = (same_chunk & (ri >= rj), same_chunk & (ri <= rj))
    lane_head = lax.broadcasted_iota(jnp.int32, (GLA_BLK, GQ), 1) // GLA_DK
    outs = (of_ref, ob_ref)

    def precompute(blk, carry):
        rows = pl.ds(pl.multiple_of(blk * GLA_BLK, GLA_BLK), GLA_BLK)
        a = a_ref[rows, :]
        q = q_ref[rows, :].astype(F32) * (GLA_DK ** -0.5)
        k = k_ref[rows, :].astype(F32)
        v = v_ref[rows, :]
        for d in range(2):
            la = _log_sigmoid(_dot(a, w2_ref[d]) + b2_ref[d]) * (1.0 / GLA_TAU)
            sel = jnp.concatenate([causal[d], same_chunk], axis=0).astype(BF16)
            hi = la.astype(BF16)
            r1 = la - hi.astype(F32)
            mid = r1.astype(BF16)
            lo = (r1 - mid.astype(F32)).astype(BF16)
            sums = _dot(sel, jnp.concatenate([hi, mid, lo], axis=1))
            sums = sums[:, :GQ] + sums[:, GQ:2 * GQ] + sums[:, 2 * GQ:]
            b, tot = sums[:GLA_BLK], sums[GLA_BLK:]
            q_dec = q * jnp.exp(b)
            k_inv = (k * jnp.exp(-b)).astype(BF16)
            qd_ref[d, rows, :] = q_dec.astype(BF16)
            ke_ref[d, rows, :] = (k * jnp.exp(tot - b)).astype(BF16)
            decay = jnp.exp(tot)
            for j in range(GLA_BLK // GLA_CHUNK):
                dec_ref[d, pl.ds(blk * (GLA_BLK // GLA_CHUNK) + j, 1), :] = decay[j * GLA_CHUNK:j * GLA_CHUNK + 1]
            intra = []
            for h in range(GLA_HEADS):
                qh = jnp.where(lane_head == h, q_dec, 0.0).astype(BF16)
                att = jnp.where(causal[d], _dot_nt(qh, k_inv), 0.0)
                intra.append(_dot(att.astype(BF16), v[:, h * GLA_DV:(h + 1) * GLA_DV]))
            outs[d][rows, :] = jnp.concatenate(intra, axis=1)
        return carry

    lax.fori_loop(0, ROWS // GLA_BLK, precompute, 0, unroll=3)

    sf_ref[...] = jnp.zeros_like(sf_ref)
    sb_ref[...] = jnp.zeros_like(sb_ref)
    bi = lax.broadcasted_iota(jnp.int32, (GV, GQ), 0)
    bj = lax.broadcasted_iota(jnp.int32, (GV, GQ), 1)
    block_diag = (bi // GLA_DV) == (bj // GLA_DK)

    def scan(i, carry):
        order = (i, jnp.where(i < NCHUNK_CTX, NCHUNK_CTX - 1 - i, NCHUNK + NCHUNK_CTX - 1 - i))
        for d, s_ref in enumerate((sf_ref, sb_ref)):
            c = order[d]
            rows = pl.ds(pl.multiple_of(c * GLA_CHUNK, GLA_CHUNK), GLA_CHUNK)
            s = s_ref[...]
            outs[d][rows, :] += _dot_nt(qd_ref[d, rows, :], s.astype(BF16))
            ds = _dot_tn(v_ref[rows, :], ke_ref[d, rows, :])
            s_ref[...] = s * dec_ref[d, pl.ds(c, 1), :] + jnp.where(block_diag, ds, 0.0)
        return carry

    lax.fori_loop(0, NCHUNK, scan, 0, unroll=2)

    def finish(t, carry):
        rows = pl.ds(pl.multiple_of(t * TM, TM), TM)
        o = of_ref[rows, :] + ob_ref[rows, :]
        parts = []
        for h in range(GLA_HEADS):
            oh = o[:, h * GLA_DV:(h + 1) * GLA_DV]
            parts.append(oh * lax.rsqrt(jnp.mean(oh * oh, axis=-1, keepdims=True) + EPS))
        on = jnp.concatenate(parts, axis=1) * g_ref[...]
        rg = r_ref[rows, :].astype(F32)
        o_ref[rows, :] = (on * (rg * jax.nn.sigmoid(rg))).astype(BF16)
        return carry

    lax.fori_loop(0, ROWS // TM, finish, 0)


def _gla(p, w2p, b2, g, bsz):
    r = p.shape[0]
    blk = lambda c, w: pl.BlockSpec((ROWS, w), lambda b: (b, c // w))
    return pl.pallas_call(
        _gla_kernel,
        grid=(bsz,),
        in_specs=[
            blk(C_QG, GQ), blk(C_KG, GQ), blk(C_AG, AG_W), blk(C_VG, GV), blk(C_RG, GV),
            pl.BlockSpec((2, AG_W, GQ), lambda b: (0, 0, 0)),
            pl.BlockSpec((2, 1, GQ), lambda b: (0, 0, 0)),
            pl.BlockSpec((1, GV), lambda b: (0, 0)),
        ],
        out_specs=pl.BlockSpec((ROWS, GV), lambda b: (b, 0)),
        out_shape=jax.ShapeDtypeStruct((r, GV), BF16),
        scratch_shapes=[
            pltpu.VMEM((ROWS, GV), F32), pltpu.VMEM((ROWS, GV), F32),
            pltpu.VMEM((2, ROWS, GQ), BF16), pltpu.VMEM((2, ROWS, GQ), BF16),
            pltpu.VMEM((2, NCHUNK, GQ), F32),
            pltpu.VMEM((GV, GQ), F32), pltpu.VMEM((GV, GQ), F32),
        ],
        compiler_params=_params("parallel"),
        name="gla",
    )(p, p, p, p, p, w2p, b2, g)


def _merge_kernel(ya_ref, yb_ref, yc_ref, g0_ref, g1_ref, g2_ref, *refs):
    *stream, gm_ref, wb_ref, wo_ref, o_ref = refs
    acc = jax.nn.sigmoid(g0_ref[...].astype(F32)) * _dot(ya_ref[...], wb_ref[0])
    acc += jax.nn.sigmoid(g1_ref[...].astype(F32)) * _dot(yb_ref[...], wb_ref[1])
    acc += jax.nn.sigmoid(g2_ref[...].astype(F32)) * _dot(yc_ref[...], wb_ref[2])
    y = _dot(acc.astype(BF16), wo_ref[...])
    o_ref[...] = _stream_tile(stream) + gm_ref[...] * y


def _merge(ya, yb, yc, p, x, mod, wb, wo, layer, bsz):
    r = p.shape[0]
    br = pl.BlockSpec((TM, BRANCH_W), lambda i: (i, 0))
    gate = lambda k: pl.BlockSpec((TM, D), lambda i: (i, C_GATE // D + k))
    stream_specs, stream_args = _stream_specs(x)
    return pl.pallas_call(
        _merge_kernel,
        grid=(r // TM,),
        in_specs=[br, br, br, gate(0), gate(1), gate(2), *stream_specs,
                  _mod_spec(2, bsz),
                  pl.BlockSpec((None, 3, BRANCH_W, D), lambda i: (layer, 0, 0, 0), pipeline_mode=pl.Buffered(1)),
                  pl.BlockSpec((None, D, D), lambda i: (layer, 0, 0), pipeline_mode=pl.Buffered(1))],
        out_specs=pl.BlockSpec((TM, D), lambda i: (i, 0)),
        out_shape=jax.ShapeDtypeStruct((r, D), F32),
        compiler_params=_params("parallel"),
        name="merge",
    )(ya, yb, yc, p, p, p, *stream_args, mod, wb, wo)


ROUTER_W = 128


def _router_kernel(x_ref, g_ref, sh_ref, sc_ref, w_ref, o_ref, h_ref):
    h = _norm_mod(x_ref[...], g_ref[...], sh_ref[...], sc_ref[...])
    h_ref[...] = h
    o_ref[...] = jnp.dot(h, w_ref[...], preferred_element_type=F32, precision=lax.Precision.HIGHEST)


def _router(x, g, mod, w, bsz):
    r = x.shape[0]
    return pl.pallas_call(
        _router_kernel,
        grid=(r // TM,),
        in_specs=[pl.BlockSpec((TM, D), lambda i: (i, 0)),
                  pl.BlockSpec((1, D), lambda i: (0, 0)),
                  _mod_spec(3, bsz), _mod_spec(4, bsz),
                  pl.BlockSpec((D, ROUTER_W), lambda i: (0, 0))],
        out_specs=[pl.BlockSpec((TM, ROUTER_W), lambda i: (i, 0)),
                   pl.BlockSpec((TM, D), lambda i: (i, 0))],
        out_shape=[jax.ShapeDtypeStruct((r, ROUTER_W), F32), jax.ShapeDtypeStruct((r, D), F32)],
        compiler_params=_params("parallel"),
        name="router",
    )(x, g, mod, mod, w)


MOE_T = 896
MOE_TF = 512
MOE_NF = D_FF // MOE_TF
MOE_CH = MOE_T // MOE_NF
MOE_ID = 1 << 17


def _moe_num_tiles(r):
    return (TOP_K * r + N_EXPERTS * (MOE_T - 1)) // MOE_T


def _route(logits, row_of=lambda t: t):
    n = logits.shape[0]
    nt = _moe_num_tiles(n)
    top_v, top_i = lax.top_k(logits, TOP_K)
    top_p = jax.nn.softmax(top_v, axis=-1)
    na = TOP_K * n
    nfill = nt * MOE_T - na
    assert na + nfill <= MOE_ID and na > MOE_T
    e_flat = top_i.reshape(-1).astype(jnp.int32)
    onehot = (e_flat[:, None] == jnp.arange(N_EXPERTS)[None, :]).astype(jnp.int32)
    csum = jnp.cumsum(onehot, axis=0)
    counts = csum[-1]
    tiles_e = (counts + MOE_T - 1) // MOE_T
    tile_end = jnp.cumsum(tiles_e)
    n_active = tile_end[-1]
    pos = jnp.sum(onehot * ((tile_end - tiles_e)[None, :] * MOE_T + csum - 1), axis=1)
    tile_id = jnp.minimum(jnp.arange(nt), n_active - 1)
    tile_expert = jnp.sum((tile_id[:, None] >= tile_end[None, :]).astype(jnp.int32), axis=1)
    fill = jnp.arange(nfill, dtype=jnp.int32)
    pad_end = jnp.cumsum(tiles_e * MOE_T - counts)
    e_fill = jnp.sum((fill[:, None] >= pad_end[None, :]).astype(jnp.int32), axis=1)
    ident = jnp.sort(jnp.concatenate([e_flat * MOE_ID + jnp.arange(na, dtype=jnp.int32),
                                      e_fill * MOE_ID + na + fill])) % MOE_ID
    src = row_of(jnp.where(ident < na, ident // TOP_K, (ident - na) % n))
    return (tile_expert.astype(jnp.int32), n_active.reshape(1).astype(jnp.int32),
            src.reshape(nt, 1, MOE_T), pos.astype(jnp.int32), top_p)


def _moe_kernel(te_ref, na_ref, src0_ref, srcn_ref, h_hbm, w1_ref, w3_ref, w2_ref, o_ref,
                gbuf, hs_ref, acc_ref, sem):
    i = pl.program_id(0)
    f = pl.program_id(1)
    active = i < na_ref[0]
    has_next = i + 1 < na_ref[0]
    slot = i % 2

    def gather_row(src_ref, buf, row):
        pltpu.make_async_copy(h_hbm.at[pl.ds(src_ref[0, row], 1)], gbuf.at[buf, pl.ds(row, 1)], sem.at[buf]).start()

    @pl.when((i == 0) & (f == 0))
    def _():
        def body(r, carry):
            gather_row(src0_ref, 0, r)
            return carry
        lax.fori_loop(0, MOE_T, body, 0)

    @pl.when(active & (f == 0))
    def _():
        pltpu.make_async_copy(h_hbm.at[pl.ds(0, MOE_T)], gbuf.at[slot], sem.at[slot]).wait()
        hs_ref[...] = gbuf[slot].astype(BF16)
        acc_ref[...] = jnp.zeros_like(acc_ref)

    def step(prefetch):
        if prefetch:
            for r in range(MOE_CH):
                gather_row(srcn_ref, 1 - slot, f * MOE_CH + r)
        h = hs_ref[...]
        a = _dot(h, w1_ref[...].astype(BF16))
        t = a * jax.nn.sigmoid(a) * _dot(h, w3_ref[...].astype(BF16))
        acc_ref[...] += _dot(t.astype(BF16), w2_ref[...].astype(BF16))

    @pl.when(has_next)
    def _():
        step(True)

    @pl.when(active & jnp.logical_not(has_next))
    def _():
        step(False)

    @pl.when(f == MOE_NF - 1)
    def _():
        o_ref[...] = jnp.where(active, acc_ref[...], 0.0)


def _moe_experts(h, tile_expert, n_active, src, w1, w3, w2, layer):
    nt = src.shape[0]
    hidden = lambda i, f, te, na: jnp.where(i < na[0], f, MOE_NF - 1)
    table = lambda idx: pl.BlockSpec((None, 1, MOE_T), lambda i, f, te, na: (idx(i), 0, 0), memory_space=pltpu.SMEM)
    grid_spec = pltpu.PrefetchScalarGridSpec(
        num_scalar_prefetch=2,
        grid=(nt, MOE_NF),
        in_specs=[
            table(lambda i: 0),
            table(lambda i: jnp.minimum(i + 1, nt - 1)),
            pl.BlockSpec(memory_space=pl.ANY),
            pl.BlockSpec((None, None, D, MOE_TF), lambda i, f, te, na: (layer, te[i], 0, hidden(i, f, te, na))),
            pl.BlockSpec((None, None, D, MOE_TF), lambda i, f, te, na: (layer, te[i], 0, hidden(i, f, te, na))),
            pl.BlockSpec((None, None, MOE_TF, D), lambda i, f, te, na: (layer, te[i], hidden(i, f, te, na), 0)),
        ],
        out_specs=pl.BlockSpec((MOE_T, D), lambda i, f, te, na: (i, 0)),
        scratch_shapes=[pltpu.VMEM((2, MOE_T, D), F32), pltpu.VMEM((MOE_T, D), BF16),
                        pltpu.VMEM((MOE_T, D), F32), pltpu.SemaphoreType.DMA((2,))],
    )
    return pl.pallas_call(
        _moe_kernel,
        grid_spec=grid_spec,
        out_shape=jax.ShapeDtypeStruct((nt * MOE_T, D), F32),
        compiler_params=_params("arbitrary", "arbitrary"),
        name="moe_experts",
    )(tile_expert, n_active, src, src, h, w1, w3, w2)


def _expert_rows_start(pos_ref, y_hbm, ybuf, sem, buf, r):
    for k in range(TOP_K):
        pltpu.make_async_copy(y_hbm.at[pl.ds(pos_ref[0, TOP_K * r + k], 1)],
                              ybuf.at[buf, pl.ds(k * TM + r, 1)], sem.at[buf]).start()


def _expert_rows_wait(y_hbm, ybuf, sem, buf):
    pltpu.make_async_copy(y_hbm.at[pl.ds(0, TOP_K * TM)], ybuf.at[buf], sem.at[buf]).wait()


def _moe_residual(x_ref, ybuf, buf, p_ref, gm_ref):
    p = p_ref[...]
    y = p[:, 0:1] * ybuf[buf, 0:TM, :]
    for k in range(1, TOP_K):
        y += p[:, k:k + 1] * ybuf[buf, k * TM:(k + 1) * TM, :]
    return x_ref[...] + gm_ref[...] * y


def _ffn_kernel(x_ref, g_ref, sh_ref, sc_ref, gm_ref, w1_ref, w3_ref, w2_ref, o_ref):
    x = x_ref[...]
    h = _norm_mod(x, g_ref[...], sh_ref[...], sc_ref[...]).astype(BF16)
    acc = jnp.zeros((TM, D), F32)
    for f0 in range(0, D_FF, FFN_TF):
        a = _dot(h, w1_ref[:, f0:f0 + FFN_TF])
        t = a * jax.nn.sigmoid(a) * _dot(h, w3_ref[:, f0:f0 + FFN_TF])
        acc = acc + _dot(t.astype(BF16), w2_ref[f0:f0 + FFN_TF, :])
    o_ref[...] = x + gm_ref[...] * acc


def _ffn(x, g, mod, w1, w3, w2, layer, bsz):
    r = x.shape[0]
    row = pl.BlockSpec((TM, D), lambda i: (i, 0))
    resident = lambda shape: pl.BlockSpec((None,) + shape, lambda i: (layer, 0, 0), pipeline_mode=pl.Buffered(1))
    return pl.pallas_call(
        _ffn_kernel,
        grid=(r // TM,),
        in_specs=[row,
                  pl.BlockSpec((1, D), lambda i: (0, 0)),
                  _mod_spec(3, bsz), _mod_spec(4, bsz), _mod_spec(5, bsz),
                  resident((D, D_FF)), resident((D, D_FF)), resident((D_FF, D))],
        out_specs=row,
        out_shape=jax.ShapeDtypeStruct((r, D), F32),
        compiler_params=_params("parallel"),
        name="ffn",
    )(x, g, mod, mod, mod, w1, w3, w2)


def _final_kernel(moe_in, *refs):
    if moe_in:
        pos0_ref, posn_ref, y_hbm, x_ref, p_ref, gm_ref, g_ref, o_ref, ybuf, sem = refs
        t = pl.program_id(0) * pl.num_programs(1) + pl.program_id(1)
        buf = t % 2

        @pl.when(t == 0)
        def _():
            def body(r, carry):
                _expert_rows_start(pos0_ref, y_hbm, ybuf, sem, 0, r)
                return carry
            lax.fori_loop(0, TM, body, 0)

        @pl.when(t + 1 < pl.num_programs(0) * pl.num_programs(1))
        def _():
            for r in range(TM):
                _expert_rows_start(posn_ref, y_hbm, ybuf, sem, 1 - buf, r)

        _expert_rows_wait(y_hbm, ybuf, sem, buf)
        x = _moe_residual(x_ref, ybuf, buf, p_ref, gm_ref)
    else:
        x_ref, g_ref, o_ref = refs
        x = x_ref[...]
    o_ref[...] = x * lax.rsqrt(jnp.mean(x * x, axis=-1, keepdims=True) + EPS) * g_ref[...]


def _final_norm(x, g, bsz, moe_in=None):
    lat = SEQ // TM
    in_specs = [pl.BlockSpec((TM, D), lambda b, j: (b * TILES_PER_BATCH + CTX // TM + j, 0))]
    args, scratch = [x], []
    if moe_in is not None:
        y, pos, top_p, prev_mod = moe_in
        pos = pos.reshape(bsz * lat, 1, TOP_K * TM)
        head, tail = _moe_in_specs(bsz * lat, lambda b, j: b * lat + j,
                                   pl.BlockSpec((None, 1, D), lambda b, j: (b, 0, 5)))
        in_specs, args = head + in_specs + tail, [pos, pos, y, x, top_p, prev_mod]
        scratch = _moe_in_scratch()
    in_specs.append(pl.BlockSpec((1, D), lambda b, j: (0, 0)))
    args.append(g)
    sem = ("arbitrary", "arbitrary") if moe_in is not None else ("parallel", "parallel")
    return pl.pallas_call(
        functools.partial(_final_kernel, moe_in is not None),
        grid=(bsz, lat),
        in_specs=in_specs,
        out_specs=pl.BlockSpec((TM, D), lambda b, j: (b * lat + j, 0)),
        out_shape=jax.ShapeDtypeStruct((bsz * SEQ, D), F32),
        scratch_shapes=scratch,
        compiler_params=_params(*sem),
        name="final_norm",
    )(*args)


def _layout_w_in(w_in):
    assert w_in.shape[1:] == (D, D_IN)
    return jnp.pad(w_in.astype(BF16), ((0, 0), (0, 0), (0, WP - D_IN)))


def _layout_gla_w2(gla_w2):
    depth = gla_w2.shape[0]
    out = jnp.zeros((depth, 2, AG_W, GQ), F32)
    for d in range(2):
        out = out.at[:, d, d * GLA_RANK:(d + 1) * GLA_RANK, :].set(gla_w2[:, d])
    return out.astype(BF16)


def kernel(x, c, ctx, c_ctx, w_mod, b_mod, norm1_g, norm2_g, w_in, conv_w, attn_sink, gla_w2, gla_b,
           gla_norm_g, w_branch, w_out, ffn_w1, ffn_w3, ffn_w2, router_w, moe_w1, moe_w3, moe_w2,
           final_norm_g):
    bsz = x.shape[0]
    depth = w_in.shape[0]
    assert x.shape[1:] == (SEQ, D) and ctx.shape[1:] == (CTX, D)

    nrow = -(-(bsz + 1) // 8) * 8
    cond = jnp.zeros((nrow, D), F32).at[:bsz].set(c).at[bsz].set(c_ctx)
    mods = _modulation(cond, w_mod, b_mod).reshape(depth, nrow, 1, 6 * D)

    xs = (ctx.reshape(bsz * CTX, D), x.reshape(bsz * SEQ, D))
    w_in_p = _layout_w_in(w_in)
    w2p = _layout_gla_w2(gla_w2)
    cos, sin = _rope_tables()
    router_p = jnp.pad(router_w, ((0, 0), (0, 0), (0, ROUTER_W - N_EXPERTS)))

    wb, wo = w_branch.astype(BF16), w_out.astype(BF16)
    f1, f3, f2 = ffn_w1.astype(BF16), ffn_w3.astype(BF16), ffn_w2.astype(BF16)

    moe_in = None
    for l in range(depth):
        mod = mods[l]
        outs = _proj_in(xs, norm1_g[l].reshape(1, D), mod, cos, sin, w_in_p, l, bsz, moe_in)
        p, qr, kt = outs[:3]
        if moe_in is not None:
            xs, moe_in = outs[3], None
        ya = _conv(p, conv_w[l], bsz)
        yb = _attention(p, qr, kt, attn_sink[l], bsz)
        yc = _gla(p, w2p[l], gla_b[l].reshape(2, 1, GQ), gla_norm_g[l].reshape(1, GV), bsz)
        xs = _merge(ya, yb, yc, p, xs, mod, wb, wo, l, bsz)
        g2 = norm2_g[l].reshape(1, D)
        j = l // 2
        if l % 2 == 0:
            xs = _ffn(xs, g2, mod, f1, f3, f2, j, bsz)
        else:
            logits, h2 = _router(xs, g2, mod, router_p[j], bsz)
            logits = logits[:, :N_EXPERTS]
            if l == depth - 1:
                logits = logits.reshape(bsz, ROWS, N_EXPERTS)[:, CTX:].reshape(bsz * SEQ, N_EXPERTS)
                routed = _route(logits, lambda t: (t // SEQ) * ROWS + CTX + t % SEQ)
            else:
                routed = _route(logits)
            tile_expert, n_active, src, pos, top_p = routed
            y = _moe_experts(h2, tile_expert, n_active, src, moe_w1, moe_w3, moe_w2, j)
            moe_in = (y, pos, top_p, mod)
    out = _final_norm(xs, final_norm_g.reshape(1, D), bsz, moe_in)
    return out.reshape(bsz, SEQ, D)
```

```python
import functools

import jax
import jax.numpy as jnp
from jax import lax
from jax.experimental import pallas as pl
from jax.experimental.pallas import tpu as pltpu

F32 = jnp.float32
BF16 = jnp.bfloat16

D = 1024
SEQ = 2048
CTX = 256
ROWS = CTX + SEQ
GRID_W = 64
EPS = 1e-6
NEG_INF = -1e30
LOG2E = 1.4426950408889634

BRANCH_W = 512
ATT_HEADS = 8
ATT_KV = 2
ATT_GROUP = ATT_HEADS // ATT_KV
HEAD_DIM = 64
QB = 128
ROPE_BASE = 10000.0
GLA_HEADS = 4
GLA_DK = 64
GLA_DV = 128
GLA_RANK = 16
GLA_TAU = 16.0
GLA_CHUNK = 64
D_FF = 3584
N_EXPERTS = 8
TOP_K = 2

C_CB, C_CC, C_CU = 0, 512, 1024
C_QA, C_KA, C_VA = 1536, 2048, 2176
C_QG, C_KG, C_AG = 2304, 2560, 2816
C_VG, C_RG, C_GATE = 3072, 3584, 4096
DP = 7168
AG_W = 128

TM = 256
TILES_PER_BATCH = ROWS // TM
PROJ_TN = 1024
FFN_TF = 512
VMEM_LIMIT = 56 * 1024 * 1024


def _params(*sem):
    return pltpu.CompilerParams(dimension_semantics=sem, vmem_limit_bytes=VMEM_LIMIT)


def _dot(a, b):
    return jnp.dot(a, b, preferred_element_type=F32)


def _dot_nt(a, b):
    return lax.dot_general(a, b, (((1,), (1,)), ((), ())), preferred_element_type=F32)


def _dot_tn(a, b):
    return lax.dot_general(a, b, (((0,), (0,)), ((), ())), preferred_element_type=F32)


def _norm_mod(x, g, shift, scale):
    h = x * lax.rsqrt(jnp.mean(x * x, axis=-1, keepdims=True) + EPS) * g
    return h * (1.0 + scale) + shift


def _mod_kernel(s_ref, w_ref, b_ref, o_ref):
    s = s_ref[...]
    s = s * jax.nn.sigmoid(s)
    o_ref[...] = _dot(s.astype(BF16), w_ref[...].astype(BF16)) + b_ref[...]


def _modulation(cond, w_mod, b_mod):
    depth = w_mod.shape[0]
    nrow = cond.shape[0]
    return pl.pallas_call(
        _mod_kernel,
        grid=(depth, 6),
        in_specs=[
            pl.BlockSpec((nrow, D), lambda l, j: (0, 0)),
            pl.BlockSpec((None, D, D), lambda l, j: (l, 0, j)),
            pl.BlockSpec((None, 1, D), lambda l, j: (l, 0, j)),
        ],
        out_specs=pl.BlockSpec((None, nrow, D), lambda l, j: (l, 0, j)),
        out_shape=jax.ShapeDtypeStruct((depth, nrow, 6 * D), F32),
        compiler_params=_params("parallel", "parallel"),
        name="modulation",
    )(cond, w_mod, b_mod.reshape(depth, 1, 6 * D))


def _mod_row(i, bsz):
    return jnp.where(i % TILES_PER_BATCH == 0, bsz, i // TILES_PER_BATCH)


def _mod_spec(j, bsz):
    return pl.BlockSpec((None, 1, D), lambda i: (_mod_row(i, bsz), 0, j))


def _rope(x, cos, sin):
    w = x.shape[-1]
    lane = lax.broadcasted_iota(jnp.int32, x.shape, 1)
    partner = jnp.where((lane % 32) < 16, pltpu.roll(x, w - 16, 1), pltpu.roll(x, 16, 1))
    return x * cos + partner * sin


AQ = ATT_HEADS * HEAD_DIM
AK = ATT_KV * HEAD_DIM


def _stream_specs(x):
    if not isinstance(x, tuple):
        return [pl.BlockSpec((TM, D), lambda i: (i, 0))], [x]
    assert CTX == TM
    lat = SEQ // TM
    return ([pl.BlockSpec((TM, D), lambda i: (i // TILES_PER_BATCH, 0)),
             pl.BlockSpec((TM, D), lambda i: ((i // TILES_PER_BATCH) * lat
                                              + jnp.maximum(i % TILES_PER_BATCH - 1, 0), 0))],
            list(x))


def _stream_tile(refs):
    if len(refs) == 1:
        return refs[0][...]
    return jnp.where(pl.program_id(0) % TILES_PER_BATCH == 0, refs[0][...], refs[1][...])


def _proj_in_kernel(moe_in, *refs):
    if moe_in:
        (pos0_ref, posn_ref, y_hbm, x_ref, p_ref, gm_ref, g_ref, sh_ref, sc_ref, cos_ref, sin_ref, w_ref,
         o_ref, q_ref, kt_ref, xo_ref, ybuf, sem) = refs
        i = pl.program_id(0)
        buf = i % 2

        @pl.when(i == 0)
        def _():
            def body(r, carry):
                _expert_rows_start(pos0_ref, y_hbm, ybuf, sem, 0, r)
                return carry
            lax.fori_loop(0, TM, body, 0)

        _expert_rows_wait(y_hbm, ybuf, sem, buf)
        x = _moe_residual(x_ref, ybuf, buf, p_ref, gm_ref)
        xo_ref[...] = x
        for r in range(TM):
            _expert_rows_start(posn_ref, y_hbm, ybuf, sem, 1 - buf, r)
    else:
        *stream, g_ref, sh_ref, sc_ref, cos_ref, sin_ref, w_ref, o_ref, q_ref, kt_ref = refs
        x = _stream_tile(stream)
    h = _norm_mod(x, g_ref[...], sh_ref[...], sc_ref[...]).astype(BF16)
    cos, sin = cos_ref[...], sin_ref[...]
    for n0 in range(0, DP, PROJ_TN):
        r = _dot(h, w_ref[:, n0:n0 + PROJ_TN])
        o_ref[:, n0:n0 + PROJ_TN] = r.astype(BF16)
        if n0 <= C_QA < n0 + PROJ_TN:
            q = _rope(r[:, C_QA - n0:C_QA - n0 + AQ], jnp.tile(cos, (1, AQ // AK)), jnp.tile(sin, (1, AQ // AK)))
            q_ref[...] = (q * (HEAD_DIM ** -0.5 * LOG2E)).astype(BF16)
        if n0 <= C_KA < n0 + PROJ_TN:
            kt_ref[...] = _rope(r[:, C_KA - n0:C_KA - n0 + AK], cos, sin).T.astype(BF16)
    if moe_in:
        @pl.when(i == pl.num_programs(0) - 1)
        def _():
            _expert_rows_wait(y_hbm, ybuf, sem, 1 - buf)


def _moe_in_specs(n_tiles, tile, gate_spec):
    pos = lambda idx: pl.BlockSpec((None, 1, TOP_K * TM), lambda *ix: (idx(*ix), 0, 0), memory_space=pltpu.SMEM)
    return ([pos(lambda *ix: 0), pos(lambda *ix: jnp.minimum(tile(*ix) + 1, n_tiles - 1)),
             pl.BlockSpec(memory_space=pl.ANY)],
            [pl.BlockSpec((TM, TOP_K), lambda *ix: (tile(*ix), 0)), gate_spec])


def _moe_in_scratch():
    return [pltpu.VMEM((2, TOP_K * TM, D), F32), pltpu.SemaphoreType.DMA((2,))]


def _proj_in(x, g, mod, cos, sin, w, layer, bsz, moe_in=None):
    r = bsz * ROWS
    nt = r // TM
    table = pl.BlockSpec((TM, AK), lambda i: (i % TILES_PER_BATCH, 0))
    row = pl.BlockSpec((TM, D), lambda i: (i, 0))
    (in_specs, args), scratch = _stream_specs(x), []
    out_specs = [pl.BlockSpec((TM, DP), lambda i: (i, 0)),
                 pl.BlockSpec((TM, AQ), lambda i: (i, 0)),
                 pl.BlockSpec((AK, TM), lambda i: (0, i))]
    out_shape = [jax.ShapeDtypeStruct((r, DP), BF16), jax.ShapeDtypeStruct((r, AQ), BF16),
                 jax.ShapeDtypeStruct((AK, r), BF16)]
    if moe_in is not None:
        y, pos, top_p, prev_mod = moe_in
        pos = pos.reshape(nt, 1, TOP_K * TM)
        head, tail = _moe_in_specs(nt, lambda i: i, _mod_spec(5, bsz))
        in_specs, args = head + in_specs + tail, [pos, pos, y, x, top_p, prev_mod]
        out_specs.append(row)
        out_shape.append(jax.ShapeDtypeStruct((r, D), F32))
        scratch = _moe_in_scratch()
    in_specs += [pl.BlockSpec((1, D), lambda i: (0, 0)), _mod_spec(0, bsz), _mod_spec(1, bsz), table, table,
                 pl.BlockSpec((None, D, DP), lambda i: (layer, 0, 0), pipeline_mode=pl.Buffered(1))]
    args += [g, mod, mod, cos, sin, w]
    return pl.pallas_call(
        functools.partial(_proj_in_kernel, moe_in is not None),
        grid=(nt,),
        in_specs=in_specs,
        out_specs=out_specs,
        out_shape=out_shape,
        scratch_shapes=scratch,
        compiler_params=_params("arbitrary" if moe_in is not None else "parallel"),
        name="proj_in",
    )(*args)


CONV_W = 256


def _conv_kernel(cb_ref, cc_ref, cu_ref, w_ref, o_ref):
    z = cc_ref[...].astype(F32) * cu_ref[...].astype(F32)
    row = lax.broadcasted_iota(jnp.int32, z.shape, 0)
    z_prev = jnp.where((row == 0) | (row == CTX), 0.0, pltpu.roll(z, 1, 0))
    z_next = jnp.where((row == CTX - 1) | (row == ROWS - 1), 0.0, pltpu.roll(z, ROWS - 1, 0))
    w = w_ref[...]
    y = z_prev * w[0:1] + z * w[1:2] + z_next * w[2:3]
    o_ref[...] = (cb_ref[...].astype(F32) * y).astype(BF16)


def _conv(p, conv_w, bsz):
    r = p.shape[0]
    blk = lambda c: pl.BlockSpec((ROWS, CONV_W), lambda b, j: (b, c // CONV_W + j))
    return pl.pallas_call(
        _conv_kernel,
        grid=(bsz, BRANCH_W // CONV_W),
        in_specs=[blk(C_CB), blk(C_CC), blk(C_CU), pl.BlockSpec((3, CONV_W), lambda b, j: (0, j))],
        out_specs=pl.BlockSpec((ROWS, CONV_W), lambda b, j: (b, j)),
        out_shape=jax.ShapeDtypeStruct((r, BRANCH_W), BF16),
        compiler_params=_params("parallel", "parallel"),
        name="short_conv",
    )(p, p, p, conv_w)


NQB = ROWS // QB
NCB = CTX // QB
NLK = 3 * QB


def _attn_kernel(sink_ref, q_ref, kp_ref, ko_ref, kn_ref, kc_ref, vp_ref, vo_ref, vn_ref, vc_ref, o_ref):
    n = pl.program_id(1)
    is_ctx = n < NCB
    qr = q_ref[...]
    kt = jnp.concatenate([kp_ref[...], ko_ref[...], kn_ref[...], kc_ref[...]], axis=1)
    vcat = jnp.concatenate([vp_ref[...], vo_ref[...], vn_ref[...], vc_ref[...]], axis=0)
    i = lax.broadcasted_iota(jnp.int32, (ATT_GROUP * QB, QB), 0) % QB
    j = lax.broadcasted_iota(jnp.int32, (ATT_GROUP * QB, QB), 1)
    far = 4 * QB
    pen_p = jnp.where(n > NCB, 0, far)
    pen_n = jnp.where((n >= NCB) & (n < NQB - 1), 0, far)
    mask_p = j >= i + pen_p
    mask_n = j + pen_n <= i
    outs = [None] * ATT_HEADS
    for kv in range(ATT_KV):
        heads = range(kv * ATT_GROUP, (kv + 1) * ATT_GROUP)
        qg = jnp.concatenate([qr[:, h * HEAD_DIM:(h + 1) * HEAD_DIM] for h in heads], axis=0)
        sink = jnp.concatenate([jnp.full((QB, 1), sink_ref[h] * LOG2E, F32) for h in heads], axis=0)
        s = _dot(qg, kt[kv * HEAD_DIM:(kv + 1) * HEAD_DIM])
        tiles = [jnp.where(mask_p, s[:, :QB], NEG_INF),
                 jnp.where(is_ctx, NEG_INF, s[:, QB:2 * QB]),
                 jnp.where(mask_n, s[:, 2 * QB:NLK], NEG_INF)]
        tiles += [s[:, c:c + QB] for c in range(NLK, NLK + CTX, QB)]
        m = tiles[0]
        for t in tiles[1:]:
            m = jnp.maximum(m, t)
        m = jnp.maximum(jnp.max(m, axis=-1, keepdims=True), sink)
        tiles = [jnp.exp2(t - m) for t in tiles]
        acc = tiles[0]
        for t in tiles[1:]:
            acc = acc + t
        den = jnp.sum(acc, axis=-1, keepdims=True) + jnp.exp2(sink - m)
        p = jnp.concatenate(tiles, axis=1).astype(BF16)
        o = _dot(p, vcat[:, kv * HEAD_DIM:(kv + 1) * HEAD_DIM]) / den
        for g, h in enumerate(heads):
            outs[h] = o[g * QB:(g + 1) * QB]
    o_ref[...] = jnp.concatenate(outs, axis=1).astype(BF16)


def _rope_tables():
    half = HEAD_DIM // 2
    t = jnp.arange(SEQ)
    row = (t // GRID_W).astype(F32)
    col = (t % GRID_W).astype(F32)
    inv_freq = ROPE_BASE ** (-jnp.arange(0, half, 2, dtype=F32) / half)
    ar = row[:, None] * inv_freq
    ac = col[:, None] * inv_freq
    cos = jnp.concatenate([jnp.cos(ar), jnp.cos(ar), jnp.cos(ac), jnp.cos(ac)], axis=1)
    sin = jnp.concatenate([-jnp.sin(ar), jnp.sin(ar), -jnp.sin(ac), jnp.sin(ac)], axis=1)
    cos = jnp.concatenate([jnp.ones((CTX, HEAD_DIM), F32), cos], axis=0)
    sin = jnp.concatenate([jnp.zeros((CTX, HEAD_DIM), F32), sin], axis=0)
    return jnp.tile(cos, (1, ATT_KV)), jnp.tile(sin, (1, ATT_KV))


def _attention(p, qr, kt, sink, bsz):
    r = p.shape[0]
    lo, hi = NCB, NQB - 1
    near = lambda b, n, shift: b * NQB + jnp.clip(n + shift, lo, hi)
    kblk = lambda shift: pl.BlockSpec((AK, QB), lambda b, n: (0, near(b, n, shift)))
    vblk = lambda shift: pl.BlockSpec((QB, AK), lambda b, n: (near(b, n, shift), C_VA // AK))
    return pl.pallas_call(
        _attn_kernel,
        grid=(bsz, NQB),
        in_specs=[
            pl.BlockSpec(memory_space=pltpu.SMEM),
            pl.BlockSpec((QB, AQ), lambda b, n: (b * NQB + n, 0)),
            kblk(-1), kblk(0), kblk(1),
            pl.BlockSpec((AK, CTX), lambda b, n: (0, b * (ROWS // CTX))),
            vblk(-1), vblk(0), vblk(1),
            pl.BlockSpec((CTX, AK), lambda b, n: (b * (ROWS // CTX), C_VA // AK)),
        ],
        out_specs=pl.BlockSpec((QB, AQ), lambda b, n: (b * NQB + n, 0)),
        out_shape=jax.ShapeDtypeStruct((r, AQ), BF16),
        compiler_params=_params("parallel", "parallel"),
        name="window_attention",
    )(sink, qr, kt, kt, kt, kt, p, p, p, p)


NCHUNK = ROWS // GLA_CHUNK
NCHUNK_CTX = CTX // GLA_CHUNK
GQ = GLA_HEADS * GLA_DK
GV = GLA_HEADS * GLA_DV
GLA_BLK = 256


def _log_sigmoid(z):
    return jnp.minimum(z, 0.0) - jnp.log(1.0 + jnp.exp(-jnp.abs(z)))


def _gla_kernel(q_ref, k_ref, a_ref, v_ref, r_ref, w2_ref, b2_ref, g_ref, o_ref,
                of_ref, ob_ref, qd_ref, ke_ref, dec_ref, sf_ref, sb_ref):
    ri = lax.broadcasted_iota(jnp.int32, (GLA_BLK, GLA_BLK), 0)
    rj = lax.broadcasted_iota(jnp.int32, (GLA_BLK, GLA_BLK), 1)
    same_chunk = (ri // GLA_CHUNK) == (rj // GLA_CHUNK)
    causal = (same_chunk & (ri >= rj), same_chunk & (ri <= rj))
    lane_head = lax.broadcasted_iota(jnp.int32, (GLA_BLK, GQ), 1) // GLA_DK
    outs = (of_ref, ob_ref)

    def precompute(blk, carry):
        rows = pl.ds(pl.multiple_of(blk * GLA_BLK, GLA_BLK), GLA_BLK)
        a = a_ref[rows, :]
        q = q_ref[rows, :].astype(F32) * (GLA_DK ** -0.5)
        k = k_ref[rows, :].astype(F32)
        v = v_ref[rows, :]
        for d in range(2):
            la = _log_sigmoid(_dot(a, w2_ref[d]) + b2_ref[d]) * (1.0 / GLA_TAU)
            sel = jnp.concatenate([causal[d], same_chunk], axis=0).astype(BF16)
            hi = la.astype(BF16)
            r1 = la - hi.astype(F32)
            mid = r1.astype(BF16)
            lo = (r1 - mid.astype(F32)).astype(BF16)
            sums = _dot(sel, jnp.concatenate([hi, mid, lo], axis=1))
            sums = sums[:, :GQ] + sums[:, GQ:2 * GQ] + sums[:, 2 * GQ:]
            b, tot = sums[:GLA_BLK], sums[GLA_BLK:]
            q_dec = q * jnp.exp(b)
            k_inv = (k * jnp.exp(-b)).astype(BF16)
            qd_ref[d, rows, :] = q_dec.astype(BF16)
            ke_ref[d, rows, :] = (k * jnp.exp(tot - b)).astype(BF16)
            decay = jnp.exp(tot)
            for j in range(GLA_BLK // GLA_CHUNK):
                dec_ref[d, pl.ds(blk * (GLA_BLK // GLA_CHUNK) + j, 1), :] = decay[j * GLA_CHUNK:j * GLA_CHUNK + 1]
            intra = []
            for h in range(GLA_HEADS):
                qh = jnp.where(lane_head == h, q_dec, 0.0).astype(BF16)
                att = jnp.where(causal[d], _dot_nt(qh, k_inv), 0.0)
                intra.append(_dot(att.astype(BF16), v[:, h * GLA_DV:(h + 1) * GLA_DV]))
            outs[d][rows, :] = jnp.concatenate(intra, axis=1)
        return carry

    lax.fori_loop(0, ROWS // GLA_BLK, precompute, 0, unroll=3)

    sf_ref[...] = jnp.zeros_like(sf_ref)
    sb_ref[...] = jnp.zeros_like(sb_ref)
    bi = lax.broadcasted_iota(jnp.int32, (GV, GQ), 0)
    bj = lax.broadcasted_iota(jnp.int32, (GV, GQ), 1)
    block_diag = (bi // GLA_DV) == (bj // GLA_DK)

    def scan(i, carry):
        order = (i, jnp.where(i < NCHUNK_CTX, NCHUNK_CTX - 1 - i, NCHUNK + NCHUNK_CTX - 1 - i))
        for d, s_ref in enumerate((sf_ref, sb_ref)):
            c = order[d]
            rows = pl.ds(pl.multiple_of(c * GLA_CHUNK, GLA_CHUNK), GLA_CHUNK)
            s = s_ref[...]
            outs[d][rows, :] += _dot_nt(qd_ref[d, rows, :], s.astype(BF16))
            ds = _dot_tn(v_ref[rows, :], ke_ref[d, rows, :])
            s_ref[...] = s * dec_ref[d, pl.ds(c, 1), :] + jnp.where(block_diag, ds, 0.0)
        return carry

    lax.fori_loop(0, NCHUNK, scan, 0, unroll=2)

    def finish(t, carry):
        rows = pl.ds(pl.multiple_of(t * TM, TM), TM)
        o = of_ref[rows, :] + ob_ref[rows, :]
        parts = []
        for h in range(GLA_HEADS):
            oh = o[:, h * GLA_DV:(h + 1) * GLA_DV]
            parts.append(oh * lax.rsqrt(jnp.mean(oh * oh, axis=-1, keepdims=True) + EPS))
        on = jnp.concatenate(parts, axis=1) * g_ref[...]
        rg = r_ref[rows, :].astype(F32)
        o_ref[rows, :] = (on * (rg * jax.nn.sigmoid(rg))).astype(BF16)
        return carry

    lax.fori_loop(0, ROWS // TM, finish, 0)


def _gla(p, w2p, b2, g, bsz):
    r = p.shape[0]
    blk = lambda c, w: pl.BlockSpec((ROWS, w), lambda b: (b, c // w))
    return pl.pallas_call(
        _gla_kernel,
        grid=(bsz,),
        in_specs=[
            blk(C_QG, GQ), blk(C_KG, GQ), blk(C_AG, AG_W), blk(C_VG, GV), blk(C_RG, GV),
            pl.BlockSpec((2, AG_W, GQ), lambda b: (0, 0, 0)),
            pl.BlockSpec((2, 1, GQ), lambda b: (0, 0, 0)),
            pl.BlockSpec((1, GV), lambda b: (0, 0)),
        ],
        out_specs=pl.BlockSpec((ROWS, GV), lambda b: (b, 0)),
        out_shape=jax.ShapeDtypeStruct((r, GV), BF16),
        scratch_shapes=[
            pltpu.VMEM((ROWS, GV), F32), pltpu.VMEM((ROWS, GV), F32),
            pltpu.VMEM((2, ROWS, GQ), BF16), pltpu.VMEM((2, ROWS, GQ), BF16),
            pltpu.VMEM((2, NCHUNK, GQ), F32),
            pltpu.VMEM((GV, GQ), F32), pltpu.VMEM((GV, GQ), F32),
        ],
        compiler_params=_params("parallel"),
        name="gla",
    )(p, p, p, p, p, w2p, b2, g)


def _merge_kernel(ya_ref, yb_ref, yc_ref, g0_ref, g1_ref, g2_ref, *refs):
    *stream, gm_ref, wb_ref, wo_ref, o_ref = refs
    acc = jax.nn.sigmoid(g0_ref[...].astype(F32)) * _dot(ya_ref[...], wb_ref[0])
    acc += jax.nn.sigmoid(g1_ref[...].astype(F32)) * _dot(yb_ref[...], wb_ref[1])
    acc += jax.nn.sigmoid(g2_ref[...].astype(F32)) * _dot(yc_ref[...], wb_ref[2])
    y = _dot(acc.astype(BF16), wo_ref[...])
    o_ref[...] = _stream_tile(stream) + gm_ref[...] * y


def _merge(ya, yb, yc, p, x, mod, wb, wo, layer, bsz):
    r = p.shape[0]
    br = pl.BlockSpec((TM, BRANCH_W), lambda i: (i, 0))
    gate = lambda k: pl.BlockSpec((TM, D), lambda i: (i, C_GATE // D + k))
    stream_specs, stream_args = _stream_specs(x)
    return pl.pallas_call(
        _merge_kernel,
        grid=(r // TM,),
        in_specs=[br, br, br, gate(0), gate(1), gate(2), *stream_specs,
                  _mod_spec(2, bsz),
                  pl.BlockSpec((None, 3, BRANCH_W, D), lambda i: (layer, 0, 0, 0), pipeline_mode=pl.Buffered(1)),
                  pl.BlockSpec((None, D, D), lambda i: (layer, 0, 0), pipeline_mode=pl.Buffered(1))],
        out_specs=pl.BlockSpec((TM, D), lambda i: (i, 0)),
        out_shape=jax.ShapeDtypeStruct((r, D), F32),
        compiler_params=_params("parallel"),
        name="merge",
    )(ya, yb, yc, p, p, p, *stream_args, mod, wb, wo)


ROUTER_W = 128


def _router_kernel(x_ref, g_ref, sh_ref, sc_ref, w_ref, o_ref, h_ref):
    h = _norm_mod(x_ref[...], g_ref[...], sh_ref[...], sc_ref[...])
    h_ref[...] = h
    o_ref[...] = jnp.dot(h, w_ref[...], preferred_element_type=F32, precision=lax.Precision.HIGHEST)


def _router(x, g, mod, w, bsz):
    r = x.shape[0]
    return pl.pallas_call(
        _router_kernel,
        grid=(r // TM,),
        in_specs=[pl.BlockSpec((TM, D), lambda i: (i, 0)),
                  pl.BlockSpec((1, D), lambda i: (0, 0)),
                  _mod_spec(3, bsz), _mod_spec(4, bsz),
                  pl.BlockSpec((D, ROUTER_W), lambda i: (0, 0))],
        out_specs=[pl.BlockSpec((TM, ROUTER_W), lambda i: (i, 0)),
                   pl.BlockSpec((TM, D), lambda i: (i, 0))],
        out_shape=[jax.ShapeDtypeStruct((r, ROUTER_W), F32), jax.ShapeDtypeStruct((r, D), F32)],
        compiler_params=_params("parallel"),
        name="router",
    )(x, g, mod, mod, w)


MOE_T = 896
MOE_TF = 512
MOE_NF = D_FF // MOE_TF
MOE_CH = MOE_T // MOE_NF
MOE_ID = 1 << 17


def _moe_num_tiles(r):
    return (TOP_K * r + N_EXPERTS * (MOE_T - 1)) // MOE_T


def _route(logits, row_of=lambda t: t):
    n = logits.shape[0]
    nt = _moe_num_tiles(n)
    top_v, top_i = lax.top_k(logits, TOP_K)
    top_p = jax.nn.softmax(top_v, axis=-1)
    na = TOP_K * n
    nfill = nt * MOE_T - na
    assert na + nfill <= MOE_ID and na > MOE_T
    e_flat = top_i.reshape(-1).astype(jnp.int32)
    onehot = (e_flat[:, None] == jnp.arange(N_EXPERTS)[None, :]).astype(jnp.int32)
    csum = jnp.cumsum(onehot, axis=0)
    counts = csum[-1]
    tiles_e = (counts + MOE_T - 1) // MOE_T
    tile_end = jnp.cumsum(tiles_e)
    n_active = tile_end[-1]
    pos = jnp.sum(onehot * ((tile_end - tiles_e)[None, :] * MOE_T + csum - 1), axis=1)
    tile_id = jnp.minimum(jnp.arange(nt), n_active - 1)
    tile_expert = jnp.sum((tile_id[:, None] >= tile_end[None, :]).astype(jnp.int32), axis=1)
    fill = jnp.arange(nfill, dtype=jnp.int32)
    pad_end = jnp.cumsum(tiles_e * MOE_T - counts)
    e_fill = jnp.sum((fill[:, None] >= pad_end[None, :]).astype(jnp.int32), axis=1)
    ident = jnp.sort(jnp.concatenate([e_flat * MOE_ID + jnp.arange(na, dtype=jnp.int32),
                                      e_fill * MOE_ID + na + fill])) % MOE_ID
    src = row_of(jnp.where(ident < na, ident // TOP_K, (ident - na) % n))
    return (tile_expert.astype(jnp.int32), n_active.reshape(1).astype(jnp.int32),
            src.reshape(nt, 1, MOE_T), pos.astype(jnp.int32), top_p)


def _moe_kernel(te_ref, na_ref, src0_ref, srcn_ref, h_hbm, w1_ref, w3_ref, w2_ref, o_ref,
                gbuf, hs_ref, acc_ref, sem):
    i = pl.program_id(0)
    f = pl.program_id(1)
    active = i < na_ref[0]
    has_next = i + 1 < na_ref[0]
    slot = i % 2

    def gather_row(src_ref, buf, row):
        pltpu.make_async_copy(h_hbm.at[pl.ds(src_ref[0, row], 1)], gbuf.at[buf, pl.ds(row, 1)], sem.at[buf]).start()

    @pl.when((i == 0) & (f == 0))
    def _():
        def body(r, carry):
            gather_row(src0_ref, 0, r)
            return carry
        lax.fori_loop(0, MOE_T, body, 0)

    @pl.when(active & (f == 0))
    def _():
        pltpu.make_async_copy(h_hbm.at[pl.ds(0, MOE_T)], gbuf.at[slot], sem.at[slot]).wait()
        hs_ref[...] = gbuf[slot].astype(BF16)
        acc_ref[...] = jnp.zeros_like(acc_ref)

    def step(prefetch):
        if prefetch:
            for r in range(MOE_CH):
                gather_row(srcn_ref, 1 - slot, f * MOE_CH + r)
        h = hs_ref[...]
        a = _dot(h, w1_ref[...].astype(BF16))
        t = a * jax.nn.sigmoid(a) * _dot(h, w3_ref[...].astype(BF16))
        acc_ref[...] += _dot(t.astype(BF16), w2_ref[...].astype(BF16))

    @pl.when(has_next)
    def _():
        step(True)

    @pl.when(active & jnp.logical_not(has_next))
    def _():
        step(False)

    @pl.when(f == MOE_NF - 1)
    def _():
        o_ref[...] = jnp.where(active, acc_ref[...], 0.0)


def _moe_experts(h, tile_expert, n_active, src, w1, w3, w2, layer):
    nt = src.shape[0]
    hidden = lambda i, f, te, na: jnp.where(i < na[0], f, MOE_NF - 1)
    table = lambda idx: pl.BlockSpec((None, 1, MOE_T), lambda i, f, te, na: (idx(i), 0, 0), memory_space=pltpu.SMEM)
    grid_spec = pltpu.PrefetchScalarGridSpec(
        num_scalar_prefetch=2,
        grid=(nt, MOE_NF),
        in_specs=[
            table(lambda i: 0),
            table(lambda i: jnp.minimum(i + 1, nt - 1)),
            pl.BlockSpec(memory_space=pl.ANY),
            pl.BlockSpec((None, None, D, MOE_TF), lambda i, f, te, na: (layer, te[i], 0, hidden(i, f, te, na))),
            pl.BlockSpec((None, None, D, MOE_TF), lambda i, f, te, na: (layer, te[i], 0, hidden(i, f, te, na))),
            pl.BlockSpec((None, None, MOE_TF, D), lambda i, f, te, na: (layer, te[i], hidden(i, f, te, na), 0)),
        ],
        out_specs=pl.BlockSpec((MOE_T, D), lambda i, f, te, na: (i, 0)),
        scratch_shapes=[pltpu.VMEM((2, MOE_T, D), F32), pltpu.VMEM((MOE_T, D), BF16),
                        pltpu.VMEM((MOE_T, D), F32), pltpu.SemaphoreType.DMA((2,))],
    )
    return pl.pallas_call(
        _moe_kernel,
        grid_spec=grid_spec,
        out_shape=jax.ShapeDtypeStruct((nt * MOE_T, D), F32),
        compiler_params=_params("arbitrary", "arbitrary"),
        name="moe_experts",
    )(tile_expert, n_active, src, src, h, w1, w3, w2)


def _expert_rows_start(pos_ref, y_hbm, ybuf, sem, buf, r):
    for k in range(TOP_K):
        pltpu.make_async_copy(y_hbm.at[pl.ds(pos_ref[0, TOP_K * r + k], 1)],
                              ybuf.at[buf, pl.ds(k * TM + r, 1)], sem.at[buf]).start()


def _expert_rows_wait(y_hbm, ybuf, sem, buf):
    pltpu.make_async_copy(y_hbm.at[pl.ds(0, TOP_K * TM)], ybuf.at[buf], sem.at[buf]).wait()


def _moe_residual(x_ref, ybuf, buf, p_ref, gm_ref):
    p = p_ref[...]
    y = p[:, 0:1] * ybuf[buf, 0:TM, :]
    for k in range(1, TOP_K):
        y += p[:, k:k + 1] * ybuf[buf, k * TM:(k + 1) * TM, :]
    return x_ref[...] + gm_ref[...] * y


def _ffn_kernel(x_ref, g_ref, sh_ref, sc_ref, gm_ref, w1_ref, w3_ref, w2_ref, o_ref):
    x = x_ref[...]
    h = _norm_mod(x, g_ref[...], sh_ref[...], sc_ref[...]).astype(BF16)
    acc = jnp.zeros((TM, D), F32)
    for f0 in range(0, D_FF, FFN_TF):
        a = _dot(h, w1_ref[:, f0:f0 + FFN_TF])
        t = a * jax.nn.sigmoid(a) * _dot(h, w3_ref[:, f0:f0 + FFN_TF])
        acc = acc + _dot(t.astype(BF16), w2_ref[f0:f0 + FFN_TF, :])
    o_ref[...] = x + gm_ref[...] * acc


def _ffn(x, g, mod, w1, w3, w2, layer, bsz):
    r = x.shape[0]
    row = pl.BlockSpec((TM, D), lambda i: (i, 0))
    resident = lambda shape: pl.BlockSpec((None,) + shape, lambda i: (layer, 0, 0), pipeline_mode=pl.Buffered(1))
    return pl.pallas_call(
        _ffn_kernel,
        grid=(r // TM,),
        in_specs=[row,
                  pl.BlockSpec((1, D), lambda i: (0, 0)),
                  _mod_spec(3, bsz), _mod_spec(4, bsz), _mod_spec(5, bsz),
                  resident((D, D_FF)), resident((D, D_FF)), resident((D_FF, D))],
        out_specs=row,
        out_shape=jax.ShapeDtypeStruct((r, D), F32),
        compiler_params=_params("parallel"),
        name="ffn",
    )(x, g, mod, mod, mod, w1, w3, w2)


def _final_kernel(moe_in, *refs):
    if moe_in:
        pos0_ref, posn_ref, y_hbm, x_ref, p_ref, gm_ref, g_ref, o_ref, ybuf, sem = refs
        t = pl.program_id(0) * pl.num_programs(1) + pl.program_id(1)
        buf = t % 2

        @pl.when(t == 0)
        def _():
            def body(r, carry):
                _expert_rows_start(pos0_ref, y_hbm, ybuf, sem, 0, r)
                return carry
            lax.fori_loop(0, TM, body, 0)

        @pl.when(t + 1 < pl.num_programs(0) * pl.num_programs(1))
        def _():
            for r in range(TM):
                _expert_rows_start(posn_ref, y_hbm, ybuf, sem, 1 - buf, r)

        _expert_rows_wait(y_hbm, ybuf, sem, buf)
        x = _moe_residual(x_ref, ybuf, buf, p_ref, gm_ref)
    else:
        x_ref, g_ref, o_ref = refs
        x = x_ref[...]
    o_ref[...] = x * lax.rsqrt(jnp.mean(x * x, axis=-1, keepdims=True) + EPS) * g_ref[...]


def _final_norm(x, g, bsz, moe_in=None):
    lat = SEQ // TM
    in_specs = [pl.BlockSpec((TM, D), lambda b, j: (b * TILES_PER_BATCH + CTX // TM + j, 0))]
    args, scratch = [x], []
    if moe_in is not None:
        y, pos, top_p, prev_mod = moe_in
        pos = pos.reshape(bsz * lat, 1, TOP_K * TM)
        head, tail = _moe_in_specs(bsz * lat, lambda b, j: b * lat + j,
                                   pl.BlockSpec((None, 1, D), lambda b, j: (b, 0, 5)))
        in_specs, args = head + in_specs + tail, [pos, pos, y, x, top_p, prev_mod]
        scratch = _moe_in_scratch()
    in_specs.append(pl.BlockSpec((1, D), lambda b, j: (0, 0)))
    args.append(g)
    sem = ("arbitrary", "arbitrary") if moe_in is not None else ("parallel", "parallel")
    return pl.pallas_call(
        functools.partial(_final_kernel, moe_in is not None),
        grid=(bsz, lat),
        in_specs=in_specs,
        out_specs=pl.BlockSpec((TM, D), lambda b, j: (b * lat + j, 0)),
        out_shape=jax.ShapeDtypeStruct((bsz * SEQ, D), F32),
        scratch_shapes=scratch,
        compiler_params=_params(*sem),
        name="final_norm",
    )(*args)


D_IN = 6944
SRC_VG = 2816
SRC_AG = SRC_VG + 2 * GV
SRC_GATE = SRC_AG + 2 * GLA_RANK
LAYOUT_TM = 256


def _w_in_layout_kernel(w_ref, o_ref):
    o_ref[:, :C_AG] = w_ref[:, :SRC_VG].astype(BF16)
    o_ref[:, C_AG:C_AG + AG_W] = jnp.zeros((LAYOUT_TM, AG_W), BF16)
    o_ref[:, C_AG:C_AG + 2 * GLA_RANK] = w_ref[:, SRC_AG:SRC_GATE].astype(BF16)
    o_ref[:, C_AG + AG_W:C_VG] = jnp.zeros((LAYOUT_TM, C_VG - C_AG - AG_W), BF16)
    o_ref[:, C_VG:C_GATE] = w_ref[:, SRC_VG:SRC_AG].astype(BF16)
    o_ref[:, C_GATE:] = w_ref[:, SRC_GATE:].astype(BF16)


def _layout_w_in(w_in):
    depth = w_in.shape[0]
    assert w_in.shape[1:] == (D, D_IN)
    return pl.pallas_call(
        _w_in_layout_kernel,
        grid=(depth, D // LAYOUT_TM),
        in_specs=[pl.BlockSpec((None, LAYOUT_TM, D_IN), lambda l, i: (l, i, 0))],
        out_specs=pl.BlockSpec((None, LAYOUT_TM, DP), lambda l, i: (l, i, 0)),
        out_shape=jax.ShapeDtypeStruct((depth, D, DP), BF16),
        compiler_params=_params("parallel", "parallel"),
        name="w_in_layout",
    )(w_in)


def _layout_gla_w2(gla_w2):
    depth = gla_w2.shape[0]
    out = jnp.zeros((depth, 2, AG_W, GQ), F32)
    for d in range(2):
        out = out.at[:, d, d * GLA_RANK:(d + 1) * GLA_RANK, :].set(gla_w2[:, d])
    return out.astype(BF16)


def kernel(x, c, ctx, c_ctx, w_mod, b_mod, norm1_g, norm2_g, w_in, conv_w, attn_sink, gla_w2, gla_b,
           gla_norm_g, w_branch, w_out, ffn_w1, ffn_w3, ffn_w2, router_w, moe_w1, moe_w3, moe_w2,
           final_norm_g):
    bsz = x.shape[0]
    depth = w_in.shape[0]
    assert x.shape[1:] == (SEQ, D) and ctx.shape[1:] == (CTX, D)

    nrow = -(-(bsz + 1) // 8) * 8
    cond = jnp.zeros((nrow, D), F32).at[:bsz].set(c).at[bsz].set(c_ctx)
    mods = _modulation(cond, w_mod, b_mod).reshape(depth, nrow, 1, 6 * D)

    xs = (ctx.reshape(bsz * CTX, D), x.reshape(bsz * SEQ, D))
    w_in_p = _layout_w_in(w_in)
    w2p = _layout_gla_w2(gla_w2)
    cos, sin = _rope_tables()
    router_p = jnp.pad(router_w, ((0, 0), (0, 0), (0, ROUTER_W - N_EXPERTS)))

    wb, wo = w_branch.astype(BF16), w_out.astype(BF16)
    f1, f3, f2 = ffn_w1.astype(BF16), ffn_w3.astype(BF16), ffn_w2.astype(BF16)

    moe_in = None
    for l in range(depth):
        mod = mods[l]
        outs = _proj_in(xs, norm1_g[l].reshape(1, D), mod, cos, sin, w_in_p, l, bsz, moe_in)
        p, qr, kt = outs[:3]
        if moe_in is not None:
            xs, moe_in = outs[3], None
        ya = _conv(p, conv_w[l], bsz)
        yb = _attention(p, qr, kt, attn_sink[l], bsz)
        yc = _gla(p, w2p[l], gla_b[l].reshape(2, 1, GQ), gla_norm_g[l].reshape(1, GV), bsz)
        xs = _merge(ya, yb, yc, p, xs, mod, wb, wo, l, bsz)
        g2 = norm2_g[l].reshape(1, D)
        j = l // 2
        if l % 2 == 0:
            xs = _ffn(xs, g2, mod, f1, f3, f2, j, bsz)
        else:
            logits, h2 = _router(xs, g2, mod, router_p[j], bsz)
            logits = logits[:, :N_EXPERTS]
            if l == depth - 1:
                logits = logits.reshape(bsz, ROWS, N_EXPERTS)[:, CTX:].reshape(bsz * SEQ, N_EXPERTS)
                routed = _route(logits, lambda t: (t // SEQ) * ROWS + CTX + t % SEQ)
            else:
                routed = _route(logits)
            tile_expert, n_active, src, pos, top_p = routed
            y = _moe_experts(h2, tile_expert, n_active, src, moe_w1, moe_w3, moe_w2, j)
            moe_in = (y, pos, top_p, mod)
    out = _final_norm(xs, final_norm_g.reshape(1, D), bsz, moe_in)
    return out.reshape(bsz, SEQ, D)
```

```python
import functools

import jax
import jax.numpy as jnp
from jax import lax
from jax.experimental import pallas as pl
from jax.experimental.pallas import tpu as pltpu

F32 = jnp.float32
BF16 = jnp.bfloat16

D = 1024
SEQ = 2048
CTX = 256
ROWS = CTX + SEQ
GRID_W = 64
EPS = 1e-6
NEG_INF = -1e30
LOG2E = 1.4426950408889634

BRANCH_W = 512
ATT_HEADS = 8
ATT_KV = 2
ATT_GROUP = ATT_HEADS // ATT_KV
HEAD_DIM = 64
QB = 128
ROPE_BASE = 10000.0
GLA_HEADS = 4
GLA_DK = 64
GLA_DV = 128
GLA_RANK = 16
GLA_TAU = 16.0
GLA_CHUNK = 64
D_FF = 3584
N_EXPERTS = 8
TOP_K = 2

C_CB, C_CC, C_CU = 0, 512, 1024
C_QA, C_KA, C_VA = 1536, 2048, 2176
C_QG, C_KG, C_AG = 2304, 2560, 2816
C_VG, C_RG, C_GATE = 3072, 3584, 4096
DP = 7168
AG_W = 128

TM = 256
TILES_PER_BATCH = ROWS // TM
PROJ_TN = 1024
FFN_TF = 512
VMEM_LIMIT = 56 * 1024 * 1024


def _params(*sem):
    return pltpu.CompilerParams(dimension_semantics=sem, vmem_limit_bytes=VMEM_LIMIT)


def _dot(a, b):
    return jnp.dot(a, b, preferred_element_type=F32)


def _dot_nt(a, b):
    return lax.dot_general(a, b, (((1,), (1,)), ((), ())), preferred_element_type=F32)


def _dot_tn(a, b):
    return lax.dot_general(a, b, (((0,), (0,)), ((), ())), preferred_element_type=F32)


def _norm_mod(x, g, shift, scale):
    h = x * lax.rsqrt(jnp.mean(x * x, axis=-1, keepdims=True) + EPS) * g
    return h * (1.0 + scale) + shift


def _mod_kernel(s_ref, w_ref, b_ref, o_ref):
    s = s_ref[...]
    s = s * jax.nn.sigmoid(s)
    o_ref[...] = _dot(s.astype(BF16), w_ref[...].astype(BF16)) + b_ref[...]


def _modulation(cond, w_mod, b_mod):
    depth = w_mod.shape[0]
    nrow = cond.shape[0]
    return pl.pallas_call(
        _mod_kernel,
        grid=(depth, 6),
        in_specs=[
            pl.BlockSpec((nrow, D), lambda l, j: (0, 0)),
            pl.BlockSpec((None, D, D), lambda l, j: (l, 0, j)),
            pl.BlockSpec((None, 1, D), lambda l, j: (l, 0, j)),
        ],
        out_specs=pl.BlockSpec((None, nrow, D), lambda l, j: (l, 0, j)),
        out_shape=jax.ShapeDtypeStruct((depth, nrow, 6 * D), F32),
        compiler_params=_params("parallel", "parallel"),
        name="modulation",
    )(cond, w_mod, b_mod.reshape(depth, 1, 6 * D))


def _mod_row(i, bsz):
    return jnp.where(i % TILES_PER_BATCH == 0, bsz, i // TILES_PER_BATCH)


def _mod_spec(j, bsz):
    return pl.BlockSpec((None, 1, D), lambda i: (_mod_row(i, bsz), 0, j))


def _rope(x, cos, sin):
    w = x.shape[-1]
    lane = lax.broadcasted_iota(jnp.int32, x.shape, 1)
    partner = jnp.where((lane % 32) < 16, pltpu.roll(x, w - 16, 1), pltpu.roll(x, 16, 1))
    return x * cos + partner * sin


AQ = ATT_HEADS * HEAD_DIM
AK = ATT_KV * HEAD_DIM


def _stream_specs(x):
    if not isinstance(x, tuple):
        return [pl.BlockSpec((TM, D), lambda i: (i, 0))], [x]
    assert CTX == TM
    lat = SEQ // TM
    return ([pl.BlockSpec((TM, D), lambda i: (i // TILES_PER_BATCH, 0)),
             pl.BlockSpec((TM, D), lambda i: ((i // TILES_PER_BATCH) * lat
                                              + jnp.maximum(i % TILES_PER_BATCH - 1, 0), 0))],
            list(x))


def _stream_tile(refs):
    if len(refs) == 1:
        return refs[0][...]
    return jnp.where(pl.program_id(0) % TILES_PER_BATCH == 0, refs[0][...], refs[1][...])


def _proj_in_kernel(moe_in, *refs):
    if moe_in:
        (pos0_ref, posn_ref, y_hbm, x_ref, p_ref, gm_ref, g_ref, sh_ref, sc_ref, cos_ref, sin_ref, w_ref,
         o_ref, q_ref, kt_ref, xo_ref, ybuf, sem) = refs
        i = pl.program_id(0)
        buf = i % 2

        @pl.when(i == 0)
        def _():
            def body(r, carry):
                _expert_rows_start(pos0_ref, y_hbm, ybuf, sem, 0, r)
                return carry
            lax.fori_loop(0, TM, body, 0)

        _expert_rows_wait(y_hbm, ybuf, sem, buf)
        x = _moe_residual(x_ref, ybuf, buf, p_ref, gm_ref)
        xo_ref[...] = x
        for r in range(TM):
            _expert_rows_start(posn_ref, y_hbm, ybuf, sem, 1 - buf, r)
    else:
        *stream, g_ref, sh_ref, sc_ref, cos_ref, sin_ref, w_ref, o_ref, q_ref, kt_ref = refs
        x = _stream_tile(stream)
    h = _norm_mod(x, g_ref[...], sh_ref[...], sc_ref[...]).astype(BF16)
    cos, sin = cos_ref[...], sin_ref[...]
    for n0 in range(0, DP, PROJ_TN):
        r = _dot(h, w_ref[:, n0:n0 + PROJ_TN])
        o_ref[:, n0:n0 + PROJ_TN] = r.astype(BF16)
        if n0 <= C_QA < n0 + PROJ_TN:
            q = _rope(r[:, C_QA - n0:C_QA - n0 + AQ], jnp.tile(cos, (1, AQ // AK)), jnp.tile(sin, (1, AQ // AK)))
            q_ref[...] = (q * (HEAD_DIM ** -0.5 * LOG2E)).astype(BF16)
        if n0 <= C_KA < n0 + PROJ_TN:
            kt_ref[...] = _rope(r[:, C_KA - n0:C_KA - n0 + AK], cos, sin).T.astype(BF16)
    if moe_in:
        @pl.when(i == pl.num_programs(0) - 1)
        def _():
            _expert_rows_wait(y_hbm, ybuf, sem, 1 - buf)


def _moe_in_specs(n_tiles, tile, gate_spec):
    pos = lambda idx: pl.BlockSpec((None, 1, TOP_K * TM), lambda *ix: (idx(*ix), 0, 0), memory_space=pltpu.SMEM)
    return ([pos(lambda *ix: 0), pos(lambda *ix: jnp.minimum(tile(*ix) + 1, n_tiles - 1)),
             pl.BlockSpec(memory_space=pl.ANY)],
            [pl.BlockSpec((TM, TOP_K), lambda *ix: (tile(*ix), 0)), gate_spec])


def _moe_in_scratch():
    return [pltpu.VMEM((2, TOP_K * TM, D), F32), pltpu.SemaphoreType.DMA((2,))]


def _proj_in(x, g, mod, cos, sin, w, layer, bsz, moe_in=None):
    r = bsz * ROWS
    nt = r // TM
    table = pl.BlockSpec((TM, AK), lambda i: (i % TILES_PER_BATCH, 0))
    row = pl.BlockSpec((TM, D), lambda i: (i, 0))
    (in_specs, args), scratch = _stream_specs(x), []
    out_specs = [pl.BlockSpec((TM, DP), lambda i: (i, 0)),
                 pl.BlockSpec((TM, AQ), lambda i: (i, 0)),
                 pl.BlockSpec((AK, TM), lambda i: (0, i))]
    out_shape = [jax.ShapeDtypeStruct((r, DP), BF16), jax.ShapeDtypeStruct((r, AQ), BF16),
                 jax.ShapeDtypeStruct((AK, r), BF16)]
    if moe_in is not None:
        y, pos, top_p, prev_mod = moe_in
        pos = pos.reshape(nt, 1, TOP_K * TM)
        head, tail = _moe_in_specs(nt, lambda i: i, _mod_spec(5, bsz))
        in_specs, args = head + in_specs + tail, [pos, pos, y, x, top_p, prev_mod]
        out_specs.append(row)
        out_shape.append(jax.ShapeDtypeStruct((r, D), F32))
        scratch = _moe_in_scratch()
    in_specs += [pl.BlockSpec((1, D), lambda i: (0, 0)), _mod_spec(0, bsz), _mod_spec(1, bsz), table, table,
                 pl.BlockSpec((None, D, DP), lambda i: (layer, 0, 0), pipeline_mode=pl.Buffered(1))]
    args += [g, mod, mod, cos, sin, w]
    return pl.pallas_call(
        functools.partial(_proj_in_kernel, moe_in is not None),
        grid=(nt,),
        in_specs=in_specs,
        out_specs=out_specs,
        out_shape=out_shape,
        scratch_shapes=scratch,
        compiler_params=_params("arbitrary" if moe_in is not None else "parallel"),
        name="proj_in",
    )(*args)


CONV_W = 256


def _conv_kernel(cb_ref, cc_ref, cu_ref, w_ref, o_ref):
    z = cc_ref[...].astype(F32) * cu_ref[...].astype(F32)
    row = lax.broadcasted_iota(jnp.int32, z.shape, 0)
    z_prev = jnp.where((row == 0) | (row == CTX), 0.0, pltpu.roll(z, 1, 0))
    z_next = jnp.where((row == CTX - 1) | (row == ROWS - 1), 0.0, pltpu.roll(z, ROWS - 1, 0))
    w = w_ref[...]
    y = z_prev * w[0:1] + z * w[1:2] + z_next * w[2:3]
    o_ref[...] = (cb_ref[...].astype(F32) * y).astype(BF16)


def _conv(p, conv_w, bsz):
    r = p.shape[0]
    blk = lambda c: pl.BlockSpec((ROWS, CONV_W), lambda b, j: (b, c // CONV_W + j))
    return pl.pallas_call(
        _conv_kernel,
        grid=(bsz, BRANCH_W // CONV_W),
        in_specs=[blk(C_CB), blk(C_CC), blk(C_CU), pl.BlockSpec((3, CONV_W), lambda b, j: (0, j))],
        out_specs=pl.BlockSpec((ROWS, CONV_W), lambda b, j: (b, j)),
        out_shape=jax.ShapeDtypeStruct((r, BRANCH_W), BF16),
        compiler_params=_params("parallel", "parallel"),
        name="short_conv",
    )(p, p, p, conv_w)


NQB = ROWS // QB
NCB = CTX // QB
NLK = 3 * QB


def _attn_kernel(sink_ref, q_ref, kp_ref, ko_ref, kn_ref, kc_ref, vp_ref, vo_ref, vn_ref, vc_ref, o_ref):
    n = pl.program_id(1)
    is_ctx = n < NCB
    qr = q_ref[...]
    kt = jnp.concatenate([kp_ref[...], ko_ref[...], kn_ref[...], kc_ref[...]], axis=1)
    vcat = jnp.concatenate([vp_ref[...], vo_ref[...], vn_ref[...], vc_ref[...]], axis=0)
    i = lax.broadcasted_iota(jnp.int32, (ATT_GROUP * QB, QB), 0) % QB
    j = lax.broadcasted_iota(jnp.int32, (ATT_GROUP * QB, QB), 1)
    far = 4 * QB
    pen_p = jnp.where(n > NCB, 0, far)
    pen_n = jnp.where((n >= NCB) & (n < NQB - 1), 0, far)
    mask_p = j >= i + pen_p
    mask_n = j + pen_n <= i
    outs = [None] * ATT_HEADS
    for kv in range(ATT_KV):
        heads = range(kv * ATT_GROUP, (kv + 1) * ATT_GROUP)
        qg = jnp.concatenate([qr[:, h * HEAD_DIM:(h + 1) * HEAD_DIM] for h in heads], axis=0)
        sink = jnp.concatenate([jnp.full((QB, 1), sink_ref[h] * LOG2E, F32) for h in heads], axis=0)
        s = _dot(qg, kt[kv * HEAD_DIM:(kv + 1) * HEAD_DIM])
        tiles = [jnp.where(mask_p, s[:, :QB], NEG_INF),
                 jnp.where(is_ctx, NEG_INF, s[:, QB:2 * QB]),
                 jnp.where(mask_n, s[:, 2 * QB:NLK], NEG_INF)]
        tiles += [s[:, c:c + QB] for c in range(NLK, NLK + CTX, QB)]
        m = tiles[0]
        for t in tiles[1:]:
            m = jnp.maximum(m, t)
        m = jnp.maximum(jnp.max(m, axis=-1, keepdims=True), sink)
        tiles = [jnp.exp2(t - m) for t in tiles]
        acc = tiles[0]
        for t in tiles[1:]:
            acc = acc + t
        den = jnp.sum(acc, axis=-1, keepdims=True) + jnp.exp2(sink - m)
        p = jnp.concatenate(tiles, axis=1).astype(BF16)
        o = _dot(p, vcat[:, kv * HEAD_DIM:(kv + 1) * HEAD_DIM]) / den
        for g, h in enumerate(heads):
            outs[h] = o[g * QB:(g + 1) * QB]
    o_ref[...] = jnp.concatenate(outs, axis=1).astype(BF16)


def _rope_tables():
    half = HEAD_DIM // 2
    t = jnp.arange(SEQ)
    row = (t // GRID_W).astype(F32)
    col = (t % GRID_W).astype(F32)
    inv_freq = ROPE_BASE ** (-jnp.arange(0, half, 2, dtype=F32) / half)
    ar = row[:, None] * inv_freq
    ac = col[:, None] * inv_freq
    cos = jnp.concatenate([jnp.cos(ar), jnp.cos(ar), jnp.cos(ac), jnp.cos(ac)], axis=1)
    sin = jnp.concatenate([-jnp.sin(ar), jnp.sin(ar), -jnp.sin(ac), jnp.sin(ac)], axis=1)
    cos = jnp.concatenate([jnp.ones((CTX, HEAD_DIM), F32), cos], axis=0)
    sin = jnp.concatenate([jnp.zeros((CTX, HEAD_DIM), F32), sin], axis=0)
    return jnp.tile(cos, (1, ATT_KV)), jnp.tile(sin, (1, ATT_KV))


def _attention(p, qr, kt, sink, bsz):
    r = p.shape[0]
    lo, hi = NCB, NQB - 1
    near = lambda b, n, shift: b * NQB + jnp.clip(n + shift, lo, hi)
    kblk = lambda shift: pl.BlockSpec((AK, QB), lambda b, n: (0, near(b, n, shift)))
    vblk = lambda shift: pl.BlockSpec((QB, AK), lambda b, n: (near(b, n, shift), C_VA // AK))
    return pl.pallas_call(
        _attn_kernel,
        grid=(bsz, NQB),
        in_specs=[
            pl.BlockSpec(memory_space=pltpu.SMEM),
            pl.BlockSpec((QB, AQ), lambda b, n: (b * NQB + n, 0)),
            kblk(-1), kblk(0), kblk(1),
            pl.BlockSpec((AK, CTX), lambda b, n: (0, b * (ROWS // CTX))),
            vblk(-1), vblk(0), vblk(1),
            pl.BlockSpec((CTX, AK), lambda b, n: (b * (ROWS // CTX), C_VA // AK)),
        ],
        out_specs=pl.BlockSpec((QB, AQ), lambda b, n: (b * NQB + n, 0)),
        out_shape=jax.ShapeDtypeStruct((r, AQ), BF16),
        compiler_params=_params("parallel", "parallel"),
        name="window_attention",
    )(sink, qr, kt, kt, kt, kt, p, p, p, p)


NCHUNK = ROWS // GLA_CHUNK
NCHUNK_CTX = CTX // GLA_CHUNK
GQ = GLA_HEADS * GLA_DK
GV = GLA_HEADS * GLA_DV
GLA_BLK = 256


def _log_sigmoid(z):
    return jnp.minimum(z, 0.0) - jnp.log(1.0 + jnp.exp(-jnp.abs(z)))


def _gla_kernel(q_ref, k_ref, a_ref, v_ref, r_ref, w2_ref, b2_ref, g_ref, o_ref,
                of_ref, ob_ref, qd_ref, ke_ref, dec_ref, sf_ref, sb_ref):
    ri = lax.broadcasted_iota(jnp.int32, (GLA_BLK, GLA_BLK), 0)
    rj = lax.broadcasted_iota(jnp.int32, (GLA_BLK, GLA_BLK), 1)
    same_chunk = (ri // GLA_CHUNK) == (rj // GLA_CHUNK)
    causal = (same_chunk & (ri >= rj), same_chunk & (ri <= rj))
    lane_head = lax.broadcasted_iota(jnp.int32, (GLA_BLK, GQ), 1) // GLA_DK
    outs = (of_ref, ob_ref)

    def precompute(blk, carry):
        rows = pl.ds(pl.multiple_of(blk * GLA_BLK, GLA_BLK), GLA_BLK)
        a = a_ref[rows, :]
        q = q_ref[rows, :].astype(F32) * (GLA_DK ** -0.5)
        k = k_ref[rows, :].astype(F32)
        v = v_ref[rows, :]
        for d in range(2):
            la = _log_sigmoid(_dot(a, w2_ref[d]) + b2_ref[d]) * (1.0 / GLA_TAU)
            sel = jnp.concatenate([causal[d], same_chunk], axis=0).astype(BF16)
            hi = la.astype(BF16)
            r1 = la - hi.astype(F32)
            mid = r1.astype(BF16)
            lo = (r1 - mid.astype(F32)).astype(BF16)
            sums = _dot(sel, jnp.concatenate([hi, mid, lo], axis=1))
            sums = sums[:, :GQ] + sums[:, GQ:2 * GQ] + sums[:, 2 * GQ:]
            b, tot = sums[:GLA_BLK], sums[GLA_BLK:]
            q_dec = q * jnp.exp(b)
            k_inv = (k * jnp.exp(-b)).astype(BF16)
            qd_ref[d, rows, :] = q_dec.astype(BF16)
            ke_ref[d, rows, :] = (k * jnp.exp(tot - b)).astype(BF16)
            decay = jnp.exp(tot)
            for j in range(GLA_BLK // GLA_CHUNK):
                dec_ref[d, pl.ds(blk * (GLA_BLK // GLA_CHUNK) + j, 1), :] = decay[j * GLA_CHUNK:j * GLA_CHUNK + 1]
            intra = []
            for h in range(GLA_HEADS):
                qh = jnp.where(lane_head == h, q_dec, 0.0).astype(BF16)
                att = jnp.where(causal[d], _dot_nt(qh, k_inv), 0.0)
                intra.append(_dot(att.astype(BF16), v[:, h * GLA_DV:(h + 1) * GLA_DV]))
            outs[d][rows, :] = jnp.concatenate(intra, axis=1)
        return carry

    lax.fori_loop(0, ROWS // GLA_BLK, precompute, 0, unroll=3)

    sf_ref[...] = jnp.zeros_like(sf_ref)
    sb_ref[...] = jnp.zeros_like(sb_ref)
    bi = lax.broadcasted_iota(jnp.int32, (GV, GQ), 0)
    bj = lax.broadcasted_iota(jnp.int32, (GV, GQ), 1)
    block_diag = (bi // GLA_DV) == (bj // GLA_DK)

    def scan(i, carry):
        order = (i, jnp.where(i < NCHUNK_CTX, NCHUNK_CTX - 1 - i, NCHUNK + NCHUNK_CTX - 1 - i))
        for d, s_ref in enumerate((sf_ref, sb_ref)):
            c = order[d]
            rows = pl.ds(pl.multiple_of(c * GLA_CHUNK, GLA_CHUNK), GLA_CHUNK)
            s = s_ref[...]
            outs[d][rows, :] += _dot_nt(qd_ref[d, rows, :], s.astype(BF16))
            ds = _dot_tn(v_ref[rows, :], ke_ref[d, rows, :])
            s_ref[...] = s * dec_ref[d, pl.ds(c, 1), :] + jnp.where(block_diag, ds, 0.0)
        return carry

    lax.fori_loop(0, NCHUNK, scan, 0, unroll=2)

    def finish(t, carry):
        rows = pl.ds(pl.multiple_of(t * TM, TM), TM)
        o = of_ref[rows, :] + ob_ref[rows, :]
        parts = []
        for h in range(GLA_HEADS):
            oh = o[:, h * GLA_DV:(h + 1) * GLA_DV]
            parts.append(oh * lax.rsqrt(jnp.mean(oh * oh, axis=-1, keepdims=True) + EPS))
        on = jnp.concatenate(parts, axis=1) * g_ref[...]
        rg = r_ref[rows, :].astype(F32)
        o_ref[rows, :] = (on * (rg * jax.nn.sigmoid(rg))).astype(BF16)
        return carry

    lax.fori_loop(0, ROWS // TM, finish, 0)


def _gla(p, w2p, b2, g, bsz):
    r = p.shape[0]
    blk = lambda c, w: pl.BlockSpec((ROWS, w), lambda b: (b, c // w))
    return pl.pallas_call(
        _gla_kernel,
        grid=(bsz,),
        in_specs=[
            blk(C_QG, GQ), blk(C_KG, GQ), blk(C_AG, AG_W), blk(C_VG, GV), blk(C_RG, GV),
            pl.BlockSpec((2, AG_W, GQ), lambda b: (0, 0, 0)),
            pl.BlockSpec((2, 1, GQ), lambda b: (0, 0, 0)),
            pl.BlockSpec((1, GV), lambda b: (0, 0)),
        ],
        out_specs=pl.BlockSpec((ROWS, GV), lambda b: (b, 0)),
        out_shape=jax.ShapeDtypeStruct((r, GV), BF16),
        scratch_shapes=[
            pltpu.VMEM((ROWS, GV), F32), pltpu.VMEM((ROWS, GV), F32),
            pltpu.VMEM((2, ROWS, GQ), BF16), pltpu.VMEM((2, ROWS, GQ), BF16),
            pltpu.VMEM((2, NCHUNK, GQ), F32),
            pltpu.VMEM((GV, GQ), F32), pltpu.VMEM((GV, GQ), F32),
        ],
        compiler_params=_params("parallel"),
        name="gla",
    )(p, p, p, p, p, w2p, b2, g)


def _merge_kernel(ya_ref, yb_ref, yc_ref, g0_ref, g1_ref, g2_ref, *refs):
    *stream, gm_ref, wb_ref, wo_ref, o_ref = refs
    acc = jax.nn.sigmoid(g0_ref[...].astype(F32)) * _dot(ya_ref[...], wb_ref[0])
    acc += jax.nn.sigmoid(g1_ref[...].astype(F32)) * _dot(yb_ref[...], wb_ref[1])
    acc += jax.nn.sigmoid(g2_ref[...].astype(F32)) * _dot(yc_ref[...], wb_ref[2])
    y = _dot(acc.astype(BF16), wo_ref[...])
    o_ref[...] = _stream_tile(stream) + gm_ref[...] * y


def _merge(ya, yb, yc, p, x, mod, wb, wo, layer, bsz):
    r = p.shape[0]
    br = pl.BlockSpec((TM, BRANCH_W), lambda i: (i, 0))
    gate = lambda k: pl.BlockSpec((TM, D), lambda i: (i, C_GATE // D + k))
    stream_specs, stream_args = _stream_specs(x)
    return pl.pallas_call(
        _merge_kernel,
        grid=(r // TM,),
        in_specs=[br, br, br, gate(0), gate(1), gate(2), *stream_specs,
                  _mod_spec(2, bsz),
                  pl.BlockSpec((None, 3, BRANCH_W, D), lambda i: (layer, 0, 0, 0), pipeline_mode=pl.Buffered(1)),
                  pl.BlockSpec((None, D, D), lambda i: (layer, 0, 0), pipeline_mode=pl.Buffered(1))],
        out_specs=pl.BlockSpec((TM, D), lambda i: (i, 0)),
        out_shape=jax.ShapeDtypeStruct((r, D), F32),
        compiler_params=_params("parallel"),
        name="merge",
    )(ya, yb, yc, p, p, p, *stream_args, mod, wb, wo)


ROUTER_W = 128


def _router_kernel(x_ref, g_ref, sh_ref, sc_ref, w_ref, o_ref, h_ref):
    h = _norm_mod(x_ref[...], g_ref[...], sh_ref[...], sc_ref[...])
    h_ref[...] = h
    o_ref[...] = jnp.dot(h, w_ref[...], preferred_element_type=F32, precision=lax.Precision.HIGHEST)


def _router(x, g, mod, w, bsz):
    r = x.shape[0]
    return pl.pallas_call(
        _router_kernel,
        grid=(r // TM,),
        in_specs=[pl.BlockSpec((TM, D), lambda i: (i, 0)),
                  pl.BlockSpec((1, D), lambda i: (0, 0)),
                  _mod_spec(3, bsz), _mod_spec(4, bsz),
                  pl.BlockSpec((D, ROUTER_W), lambda i: (0, 0))],
        out_specs=[pl.BlockSpec((TM, ROUTER_W), lambda i: (i, 0)),
                   pl.BlockSpec((TM, D), lambda i: (i, 0))],
        out_shape=[jax.ShapeDtypeStruct((r, ROUTER_W), F32), jax.ShapeDtypeStruct((r, D), F32)],
        compiler_params=_params("parallel"),
        name="router",
    )(x, g, mod, mod, w)


MOE_T = 896
MOE_TF = 512
MOE_NF = D_FF // MOE_TF
MOE_CH = MOE_T // MOE_NF
MOE_ID = 1 << 17


def _moe_num_tiles(r):
    return (TOP_K * r + N_EXPERTS * (MOE_T - 1)) // MOE_T


def _route(logits, row_of=lambda t: t):
    n = logits.shape[0]
    nt = _moe_num_tiles(n)
    top_v, top_i = lax.top_k(logits, TOP_K)
    top_p = jax.nn.softmax(top_v, axis=-1)
    na = TOP_K * n
    nfill = nt * MOE_T - na
    assert na + nfill <= MOE_ID and na > MOE_T
    e_flat = top_i.reshape(-1).astype(jnp.int32)
    onehot = (e_flat[:, None] == jnp.arange(N_EXPERTS)[None, :]).astype(jnp.int32)
    csum = jnp.cumsum(onehot, axis=0)
    counts = csum[-1]
    tiles_e = (counts + MOE_T - 1) // MOE_T
    tile_end = jnp.cumsum(tiles_e)
    n_active = tile_end[-1]
    pos = jnp.sum(onehot * ((tile_end - tiles_e)[None, :] * MOE_T + csum - 1), axis=1)
    tile_id = jnp.minimum(jnp.arange(nt), n_active - 1)
    tile_expert = jnp.sum((tile_id[:, None] >= tile_end[None, :]).astype(jnp.int32), axis=1)
    fill = jnp.arange(nfill, dtype=jnp.int32)
    pad_end = jnp.cumsum(tiles_e * MOE_T - counts)
    e_fill = jnp.sum((fill[:, None] >= pad_end[None, :]).astype(jnp.int32), axis=1)
    ident = jnp.sort(jnp.concatenate([e_flat * MOE_ID + jnp.arange(na, dtype=jnp.int32),
                                      e_fill * MOE_ID + na + fill])) % MOE_ID
    src = row_of(jnp.where(ident < na, ident // TOP_K, (ident - na) % n))
    return (tile_expert.astype(jnp.int32), n_active.reshape(1).astype(jnp.int32),
            src.reshape(nt, 1, MOE_T), pos.astype(jnp.int32), top_p)


def _moe_kernel(te_ref, na_ref, src0_ref, srcn_ref, h_hbm, w1_ref, w3_ref, w2_ref, o_ref,
                gbuf, hs_ref, acc_ref, sem):
    i = pl.program_id(0)
    f = pl.program_id(1)
    active = i < na_ref[0]
    has_next = i + 1 < na_ref[0]
    slot = i % 2

    def gather_row(src_ref, buf, row):
        pltpu.make_async_copy(h_hbm.at[pl.ds(src_ref[0, row], 1)], gbuf.at[buf, pl.ds(row, 1)], sem.at[buf]).start()

    @pl.when((i == 0) & (f == 0))
    def _():
        def body(r, carry):
            gather_row(src0_ref, 0, r)
            return carry
        lax.fori_loop(0, MOE_T, body, 0)

    @pl.when(active & (f == 0))
    def _():
        pltpu.make_async_copy(h_hbm.at[pl.ds(0, MOE_T)], gbuf.at[slot], sem.at[slot]).wait()
        hs_ref[...] = gbuf[slot].astype(BF16)
        acc_ref[...] = jnp.zeros_like(acc_ref)

    def step(prefetch):
        if prefetch:
            for r in range(MOE_CH):
                gather_row(srcn_ref, 1 - slot, f * MOE_CH + r)
        h = hs_ref[...]
        a = _dot(h, w1_ref[...].astype(BF16))
        t = a * jax.nn.sigmoid(a) * _dot(h, w3_ref[...].astype(BF16))
        acc_ref[...] += _dot(t.astype(BF16), w2_ref[...].astype(BF16))

    @pl.when(has_next)
    def _():
        step(True)

    @pl.when(active & jnp.logical_not(has_next))
    def _():
        step(False)

    @pl.when(f == MOE_NF - 1)
    def _():
        o_ref[...] = jnp.where(active, acc_ref[...], 0.0)


def _moe_experts(h, tile_expert, n_active, src, w1, w3, w2, layer):
    nt = src.shape[0]
    hidden = lambda i, f, te, na: jnp.where(i < na[0], f, MOE_NF - 1)
    table = lambda idx: pl.BlockSpec((None, 1, MOE_T), lambda i, f, te, na: (idx(i), 0, 0), memory_space=pltpu.SMEM)
    grid_spec = pltpu.PrefetchScalarGridSpec(
        num_scalar_prefetch=2,
        grid=(nt, MOE_NF),
        in_specs=[
            table(lambda i: 0),
            table(lambda i: jnp.minimum(i + 1, nt - 1)),
            pl.BlockSpec(memory_space=pl.ANY),
            pl.BlockSpec((None, None, D, MOE_TF), lambda i, f, te, na: (layer, te[i], 0, hidden(i, f, te, na))),
            pl.BlockSpec((None, None, D, MOE_TF), lambda i, f, te, na: (layer, te[i], 0, hidden(i, f, te, na))),
            pl.BlockSpec((None, None, MOE_TF, D), lambda i, f, te, na: (layer, te[i], hidden(i, f, te, na), 0)),
        ],
        out_specs=pl.BlockSpec((MOE_T, D), lambda i, f, te, na: (i, 0)),
        scratch_shapes=[pltpu.VMEM((2, MOE_T, D), F32), pltpu.VMEM((MOE_T, D), BF16),
                        pltpu.VMEM((MOE_T, D), F32), pltpu.SemaphoreType.DMA((2,))],
    )
    return pl.pallas_call(
        _moe_kernel,
        grid_spec=grid_spec,
        out_shape=jax.ShapeDtypeStruct((nt * MOE_T, D), F32),
        compiler_params=_params("arbitrary", "arbitrary"),
        name="moe_experts",
    )(tile_expert, n_active, src, src, h, w1, w3, w2)


def _expert_rows_start(pos_ref, y_hbm, ybuf, sem, buf, r):
    for k in range(TOP_K):
        pltpu.make_async_copy(y_hbm.at[pl.ds(pos_ref[0, TOP_K * r + k], 1)],
                              ybuf.at[buf, pl.ds(k * TM + r, 1)], sem.at[buf]).start(priority=k % 2)


def _expert_rows_wait(y_hbm, ybuf, sem, buf):
    pltpu.make_async_copy(y_hbm.at[pl.ds(0, TOP_K * TM)], ybuf.at[buf], sem.at[buf]).wait()


def _moe_residual(x_ref, ybuf, buf, p_ref, gm_ref):
    p = p_ref[...]
    y = p[:, 0:1] * ybuf[buf, 0:TM, :]
    for k in range(1, TOP_K):
        y += p[:, k:k + 1] * ybuf[buf, k * TM:(k + 1) * TM, :]
    return x_ref[...] + gm_ref[...] * y


def _ffn_kernel(x_ref, g_ref, sh_ref, sc_ref, gm_ref, w1_ref, w3_ref, w2_ref, o_ref):
    x = x_ref[...]
    h = _norm_mod(x, g_ref[...], sh_ref[...], sc_ref[...]).astype(BF16)
    acc = jnp.zeros((TM, D), F32)
    for f0 in range(0, D_FF, FFN_TF):
        a = _dot(h, w1_ref[:, f0:f0 + FFN_TF])
        t = a * jax.nn.sigmoid(a) * _dot(h, w3_ref[:, f0:f0 + FFN_TF])
        acc = acc + _dot(t.astype(BF16), w2_ref[f0:f0 + FFN_TF, :])
    o_ref[...] = x + gm_ref[...] * acc


def _ffn(x, g, mod, w1, w3, w2, layer, bsz):
    r = x.shape[0]
    row = pl.BlockSpec((TM, D), lambda i: (i, 0))
    resident = lambda shape: pl.BlockSpec((None,) + shape, lambda i: (layer, 0, 0), pipeline_mode=pl.Buffered(1))
    return pl.pallas_call(
        _ffn_kernel,
        grid=(r // TM,),
        in_specs=[row,
                  pl.BlockSpec((1, D), lambda i: (0, 0)),
                  _mod_spec(3, bsz), _mod_spec(4, bsz), _mod_spec(5, bsz),
                  resident((D, D_FF)), resident((D, D_FF)), resident((D_FF, D))],
        out_specs=row,
        out_shape=jax.ShapeDtypeStruct((r, D), F32),
        compiler_params=_params("parallel"),
        name="ffn",
    )(x, g, mod, mod, mod, w1, w3, w2)


def _final_kernel(moe_in, *refs):
    if moe_in:
        pos0_ref, posn_ref, y_hbm, x_ref, p_ref, gm_ref, g_ref, o_ref, ybuf, sem = refs
        t = pl.program_id(0) * pl.num_programs(1) + pl.program_id(1)
        buf = t % 2

        @pl.when(t == 0)
        def _():
            def body(r, carry):
                _expert_rows_start(pos0_ref, y_hbm, ybuf, sem, 0, r)
                return carry
            lax.fori_loop(0, TM, body, 0)

        @pl.when(t + 1 < pl.num_programs(0) * pl.num_programs(1))
        def _():
            for r in range(TM):
                _expert_rows_start(posn_ref, y_hbm, ybuf, sem, 1 - buf, r)

        _expert_rows_wait(y_hbm, ybuf, sem, buf)
        x = _moe_residual(x_ref, ybuf, buf, p_ref, gm_ref)
    else:
        x_ref, g_ref, o_ref = refs
        x = x_ref[...]
    o_ref[...] = x * lax.rsqrt(jnp.mean(x * x, axis=-1, keepdims=True) + EPS) * g_ref[...]


def _final_norm(x, g, bsz, moe_in=None):
    lat = SEQ // TM
    in_specs = [pl.BlockSpec((TM, D), lambda b, j: (b * TILES_PER_BATCH + CTX // TM + j, 0))]
    args, scratch = [x], []
    if moe_in is not None:
        y, pos, top_p, prev_mod = moe_in
        pos = pos.reshape(bsz * lat, 1, TOP_K * TM)
        head, tail = _moe_in_specs(bsz * lat, lambda b, j: b * lat + j,
                                   pl.BlockSpec((None, 1, D), lambda b, j: (b, 0, 5)))
        in_specs, args = head + in_specs + tail, [pos, pos, y, x, top_p, prev_mod]
        scratch = _moe_in_scratch()
    in_specs.append(pl.BlockSpec((1, D), lambda b, j: (0, 0)))
    args.append(g)
    sem = ("arbitrary", "arbitrary") if moe_in is not None else ("parallel", "parallel")
    return pl.pallas_call(
        functools.partial(_final_kernel, moe_in is not None),
        grid=(bsz, lat),
        in_specs=in_specs,
        out_specs=pl.BlockSpec((TM, D), lambda b, j: (b * lat + j, 0)),
        out_shape=jax.ShapeDtypeStruct((bsz * SEQ, D), F32),
        scratch_shapes=scratch,
        compiler_params=_params(*sem),
        name="final_norm",
    )(*args)


D_IN = 6944
SRC_VG = 2816
SRC_AG = SRC_VG + 2 * GV
SRC_GATE = SRC_AG + 2 * GLA_RANK
LAYOUT_TM = 256


def _w_in_layout_kernel(w_ref, o_ref):
    o_ref[:, :C_AG] = w_ref[:, :SRC_VG].astype(BF16)
    o_ref[:, C_AG:C_AG + AG_W] = jnp.zeros((LAYOUT_TM, AG_W), BF16)
    o_ref[:, C_AG:C_AG + 2 * GLA_RANK] = w_ref[:, SRC_AG:SRC_GATE].astype(BF16)
    o_ref[:, C_AG + AG_W:C_VG] = jnp.zeros((LAYOUT_TM, C_VG - C_AG - AG_W), BF16)
    o_ref[:, C_VG:C_GATE] = w_ref[:, SRC_VG:SRC_AG].astype(BF16)
    o_ref[:, C_GATE:] = w_ref[:, SRC_GATE:].astype(BF16)


def _layout_w_in(w_in):
    depth = w_in.shape[0]
    assert w_in.shape[1:] == (D, D_IN)
    return pl.pallas_call(
        _w_in_layout_kernel,
        grid=(depth, D // LAYOUT_TM),
        in_specs=[pl.BlockSpec((None, LAYOUT_TM, D_IN), lambda l, i: (l, i, 0))],
        out_specs=pl.BlockSpec((None, LAYOUT_TM, DP), lambda l, i: (l, i, 0)),
        out_shape=jax.ShapeDtypeStruct((depth, D, DP), BF16),
        compiler_params=_params("parallel", "parallel"),
        name="w_in_layout",
    )(w_in)


def _layout_gla_w2(gla_w2):
    depth = gla_w2.shape[0]
    out = jnp.zeros((depth, 2, AG_W, GQ), F32)
    for d in range(2):
        out = out.at[:, d, d * GLA_RANK:(d + 1) * GLA_RANK, :].set(gla_w2[:, d])
    return out.astype(BF16)


def kernel(x, c, ctx, c_ctx, w_mod, b_mod, norm1_g, norm2_g, w_in, conv_w, attn_sink, gla_w2, gla_b,
           gla_norm_g, w_branch, w_out, ffn_w1, ffn_w3, ffn_w2, router_w, moe_w1, moe_w3, moe_w2,
           final_norm_g):
    bsz = x.shape[0]
    depth = w_in.shape[0]
    assert x.shape[1:] == (SEQ, D) and ctx.shape[1:] == (CTX, D)

    nrow = -(-(bsz + 1) // 8) * 8
    cond = jnp.zeros((nrow, D), F32).at[:bsz].set(c).at[bsz].set(c_ctx)
    mods = _modulation(cond, w_mod, b_mod).reshape(depth, nrow, 1, 6 * D)

    xs = (ctx.reshape(bsz * CTX, D), x.reshape(bsz * SEQ, D))
    w_in_p = _layout_w_in(w_in)
    w2p = _layout_gla_w2(gla_w2)
    cos, sin = _rope_tables()
    router_p = jnp.pad(router_w, ((0, 0), (0, 0), (0, ROUTER_W - N_EXPERTS)))

    wb, wo = w_branch.astype(BF16), w_out.astype(BF16)
    f1, f3, f2 = ffn_w1.astype(BF16), ffn_w3.astype(BF16), ffn_w2.astype(BF16)

    moe_in = None
    for l in range(depth):
        mod = mods[l]
        outs = _proj_in(xs, norm1_g[l].reshape(1, D), mod, cos, sin, w_in_p, l, bsz, moe_in)
        p, qr, kt = outs[:3]
        if moe_in is not None:
            xs, moe_in = outs[3], None
        ya = _conv(p, conv_w[l], bsz)
        yb = _attention(p, qr, kt, attn_sink[l], bsz)
        yc = _gla(p, w2p[l], gla_b[l].reshape(2, 1, GQ), gla_norm_g[l].reshape(1, GV), bsz)
        xs = _merge(ya, yb, yc, p, xs, mod, wb, wo, l, bsz)
        g2 = norm2_g[l].reshape(1, D)
        j = l // 2
        if l % 2 == 0:
            xs = _ffn(xs, g2, mod, f1, f3, f2, j, bsz)
        else:
            logits, h2 = _router(xs, g2, mod, router_p[j], bsz)
            logits = logits[:, :N_EXPERTS]
            if l == depth - 1:
                logits = logits.reshape(bsz, ROWS, N_EXPERTS)[:, CTX:].reshape(bsz * SEQ, N_EXPERTS)
                routed = _route(logits, lambda t: (t // SEQ) * ROWS + CTX + t % SEQ)
            else:
                routed = _route(logits)
            tile_expert, n_active, src, pos, top_p = routed
            y = _moe_experts(h2, tile_expert, n_active, src, moe_w1, moe_w3, moe_w2, j)
            moe_in = (y, pos, top_p, mod)
    out = _final_norm(xs, final_norm_g.reshape(1, D), bsz, moe_in)
    return out.reshape(bsz, SEQ, D)
```
